```python
import math
import jax, jax.numpy as jnp
from jax import lax
import numpy as np

D_MODEL = 1024
BATCH = 32
SEQ = 2048
DEPTH = 1
DEC_BATCH = 32
DEC_SEQ = 16
PAST_LEN = 1024

CHUNK = 64
Q_BLOCK = 128
MIX_WIDTH = D_MODEL
ATTN_WIDTH = MIX_WIDTH // 2
CONV_WIDTH = MIX_WIDTH - ATTN_WIDTH
N_HEADS_A = 4
V_HEAD_DIM = ATTN_WIDTH // N_HEADS_A
QK_HEAD_DIM = V_HEAD_DIM // 2
ROT_DIM = QK_HEAD_DIM // 4
ROPE_THETA = 500000.0
SC_WIDTH = 3
N_MEM = 256
N_HEADS_X = 4
X_HEAD_DIM = D_MODEL // N_HEADS_X
D_FF = 2816
FFN_CONV_WIDTH = 3
EPS = 1e-6
Q_COLS = N_HEADS_A * 2 * QK_HEAD_DIM
K_COLS = N_HEADS_A * 2 * QK_HEAD_DIM
V_COLS = N_HEADS_A * V_HEAD_DIM
IN_PROJ = Q_COLS + K_COLS + V_COLS + 3 * CONV_WIDTH

kernel_name = "hybrid_diffattn_shortconv_streaming_step"


def lambda_init_fn(layer_idx):
    return 0.8 - 0.6 * math.exp(-0.3 * layer_idx)


def rms_norm(x, g):
    xf = x.astype(jnp.float32)
    y = xf * lax.rsqrt(jnp.mean(xf * xf, axis=-1, keepdims=True) + EPS)
    return (y * g.astype(jnp.float32)).astype(x.dtype)


def rope_partial(t, pos):
    half = ROT_DIM // 2
    inv = 1.0 / (ROPE_THETA ** (jnp.arange(half, dtype=jnp.float32) * 2.0 / ROT_DIM))
    ang = pos.astype(jnp.float32)[:, None] * inv[None, :]
    cos = jnp.cos(ang)[None, :, None, None, :]
    sin = jnp.sin(ang)[None, :, None, None, :]
    tr = t[..., :ROT_DIM].astype(jnp.float32)
    x1, x2 = tr[..., :half], tr[..., half:]
    rot = jnp.concatenate([x1 * cos - x2 * sin, x2 * cos + x1 * sin], axis=-1).astype(t.dtype)
    return jnp.concatenate([rot, t[..., ROT_DIM:]], axis=-1)


def causal_dwconv(u, state, w):
    k_w = w.shape[0]
    t_len = u.shape[1]
    full = jnp.concatenate([state.astype(u.dtype), u], axis=1)
    y = w[0] * full[:, 0:t_len]
    for j in range(1, k_w):
        y = y + w[j] * full[:, j:j + t_len]
    return y, full[:, -(k_w - 1):]


def diff_lambda(lq1, lk1, lq2, lk2, lam_init):
    f = jnp.float32
    return (jnp.exp(jnp.sum(lq1.astype(f) * lk1.astype(f)))
            - jnp.exp(jnp.sum(lq2.astype(f) * lk2.astype(f))) + lam_init)


def diff_combine(s, v, lam):
    p = jax.nn.softmax(s, axis=-1)
    a = p[:, :, 0] - lam * p[:, :, 1]
    return jnp.einsum('bhqk,bkhe->bqhe', a.astype(v.dtype), v)


def diff_attn_prompt(q, k, v, lam):
    b, s_len = q.shape[0], q.shape[1]
    nblk = s_len // Q_BLOCK
    scale = QK_HEAD_DIM ** -0.5
    key_chunk = jnp.arange(s_len) // CHUNK
    qb = q.reshape(b, nblk, Q_BLOCK, N_HEADS_A, 2, QK_HEAD_DIM).swapaxes(0, 1)

    def block(args):
        q_blk, i = args
        s = jnp.einsum('bqhmd,bkhmd->bhmqk', q_blk, k,
                       preferred_element_type=jnp.float32) * scale
        q_chunk = (i * Q_BLOCK + jnp.arange(Q_BLOCK)) // CHUNK
        mask = q_chunk[:, None] >= key_chunk[None, :]
        s = jnp.where(mask, s, -jnp.inf)
        return diff_combine(s, v, lam)

    out = lax.map(block, (qb, jnp.arange(nblk)))
    return out.swapaxes(0, 1).reshape(b, s_len, N_HEADS_A, V_HEAD_DIM)


def diff_attn_sample(q, k_all, v_all, lam):
    scale = QK_HEAD_DIM ** -0.5
    s = jnp.einsum('bqhmd,bkhmd->bhmqk', q, k_all,
                   preferred_element_type=jnp.float32) * scale
    return diff_combine(s, v_all, lam)


def mem_kv(mem, g_mem, w_xk, w_xv):
    b, n = mem.shape[0], mem.shape[1]
    m = rms_norm(mem, g_mem)
    mk = (m @ w_xk).reshape(b, n, N_HEADS_X, X_HEAD_DIM)
    mv = (m @ w_xv).reshape(b, n, N_HEADS_X, X_HEAD_DIM)
    return mk, mv


def layer(x, pos, past_k, past_v, sc_state, ffn_state, mem_k, mem_v, lam_init,
          g_mix, w_in, lq1, lk1, lq2, lk2, g_sub, w_sc, w_out,
          g_x, w_xq, w_xo, g_ffn, w_up, w_gate, w_ffconv, w_down):
    b, t_len = x.shape[0], x.shape[1]
    h = rms_norm(x, g_mix)
    proj = h @ w_in
    c1 = Q_COLS
    c2 = c1 + K_COLS
    c3 = c2 + V_COLS
    c4 = c3 + CONV_WIDTH
    c5 = c4 + CONV_WIDTH
    q, k, v, bg, cg, xh = jnp.split(proj, [c1, c2, c3, c4, c5], axis=-1)
    q = rope_partial(q.reshape(b, t_len, N_HEADS_A, 2, QK_HEAD_DIM), pos)
    k = rope_partial(k.reshape(b, t_len, N_HEADS_A, 2, QK_HEAD_DIM), pos)
    v = v.reshape(b, t_len, N_HEADS_A, V_HEAD_DIM)
    lam = diff_lambda(lq1, lk1, lq2, lk2, lam_init)
    if past_k is None:
        o = diff_attn_prompt(q, k, v, lam)
    else:
        k_all = jnp.concatenate([past_k.astype(k.dtype), k], axis=1)
        v_all = jnp.concatenate([past_v.astype(v.dtype), v], axis=1)
        o = diff_attn_sample(q, k_all, v_all, lam)
    o = (rms_norm(o, g_sub) * (1.0 - lam_init)).reshape(b, t_len, ATTN_WIDTH)
    conv_y, new_sc = causal_dwconv(cg * xh, sc_state, w_sc)
    y_sc = bg * conv_y
    x = x + jnp.concatenate([o, y_sc], axis=-1) @ w_out
    hq = (rms_norm(x, g_x) @ w_xq).reshape(b, t_len, N_HEADS_X, X_HEAD_DIM)
    s = jnp.einsum('bthd,bmhd->bhtm', hq, mem_k.astype(hq.dtype),
                   preferred_element_type=jnp.float32) * (X_HEAD_DIM ** -0.5)
    p = jax.nn.softmax(s, axis=-1)
    xo = jnp.einsum('bhtm,bmhd->bthd', p.astype(x.dtype), mem_v.astype(x.dtype))
    x = x + xo.reshape(b, t_len, D_MODEL) @ w_xo
    hf = rms_norm(x, g_ffn)
    u_c, new_ffn = causal_dwconv(hf @ w_up, ffn_state, w_ffconv)
    x = x + (jax.nn.silu(u_c) * (hf @ w_gate)) @ w_down
    return x, k, v, new_sc, new_ffn


def setup_inputs(seed: int = 0) -> dict:
    key = jax.random.key(seed)
    ks = jax.random.split(key, 40)
    f = jnp.float32

    def nrm(i, shape, scale=1.0):
        return jax.random.normal(ks[i], shape, f) * scale

    def gain(i, n):
        return 1.0 + 0.02 * jax.random.normal(ks[i], (DEPTH, n), f)

    return {
        "x_prompt": nrm(0, (BATCH, SEQ, D_MODEL)),
        "x_sample": nrm(1, (DEC_BATCH, DEC_SEQ, D_MODEL)),
        "cache_attn_k": nrm(2, (DEPTH, DEC_BATCH, PAST_LEN, N_HEADS_A, 2, QK_HEAD_DIM)),
        "cache_attn_v": nrm(3, (DEPTH, DEC_BATCH, PAST_LEN, N_HEADS_A, V_HEAD_DIM)),
        "state_short_conv": nrm(4, (DEPTH, DEC_BATCH, SC_WIDTH - 1, CONV_WIDTH)),
        "state_ffn_conv": nrm(5, (DEPTH, DEC_BATCH, FFN_CONV_WIDTH - 1, D_FF)),
        "cache_mem_k": nrm(6, (DEPTH, DEC_BATCH, N_MEM, N_HEADS_X, X_HEAD_DIM)),
        "cache_mem_v": nrm(7, (DEPTH, DEC_BATCH, N_MEM, N_HEADS_X, X_HEAD_DIM)),
        "mem_prompt": nrm(8, (BATCH, N_MEM, D_MODEL)),
        "g_mix": gain(9, D_MODEL),
        "w_in": nrm(10, (DEPTH, D_MODEL, IN_PROJ), D_MODEL ** -0.5),
        "lam_q1": nrm(11, (DEPTH, QK_HEAD_DIM), 0.1),
        "lam_k1": nrm(12, (DEPTH, QK_HEAD_DIM), 0.1),
        "lam_q2": nrm(13, (DEPTH, QK_HEAD_DIM), 0.1),
        "lam_k2": nrm(14, (DEPTH, QK_HEAD_DIM), 0.1),
        "g_sub": gain(15, V_HEAD_DIM),
        "w_sc": nrm(16, (DEPTH, SC_WIDTH, CONV_WIDTH), SC_WIDTH ** -0.5),
        "w_out": nrm(17, (DEPTH, MIX_WIDTH, D_MODEL), MIX_WIDTH ** -0.5),
        "g_mem": gain(18, D_MODEL),
        "g_x": gain(19, D_MODEL),
        "w_xq": nrm(20, (DEPTH, D_MODEL, D_MODEL), D_MODEL ** -0.5),
        "w_xk": nrm(21, (DEPTH, D_MODEL, D_MODEL), D_MODEL ** -0.5),
        "w_xv": nrm(22, (DEPTH, D_MODEL, D_MODEL), D_MODEL ** -0.5),
        "w_xo": nrm(23, (DEPTH, D_MODEL, D_MODEL), D_MODEL ** -0.5),
        "g_ffn": gain(24, D_MODEL),
        "w_up": nrm(25, (DEPTH, D_MODEL, D_FF), D_MODEL ** -0.5),
        "w_gate": nrm(26, (DEPTH, D_MODEL, D_FF), D_MODEL ** -0.5),
        "w_ffconv": nrm(27, (DEPTH, FFN_CONV_WIDTH, D_FF), FFN_CONV_WIDTH ** -0.5),
        "w_down": nrm(28, (DEPTH, D_FF, D_MODEL), D_FF ** -0.5),
        "g_final": 1.0 + 0.02 * jax.random.normal(ks[29], (D_MODEL,), f),
    }


def reference(x_prompt, x_sample, cache_attn_k, cache_attn_v, state_short_conv, state_ffn_conv,
              cache_mem_k, cache_mem_v, mem_prompt,
              g_mix, w_in, lam_q1, lam_k1, lam_q2, lam_k2, g_sub, w_sc, w_out,
              g_mem, g_x, w_xq, w_xk, w_xv, w_xo, g_ffn, w_up, w_gate, w_ffconv, w_down,
              g_final):
    b_p, s_len = x_prompt.shape[0], x_prompt.shape[1]
    b_s, t_len = x_sample.shape[0], x_sample.shape[1]
    past_len = cache_attn_k.shape[2]
    pos_p = jnp.arange(s_len, dtype=jnp.int32)
    pos_s = past_len + jnp.arange(t_len, dtype=jnp.int32)

    hp, hs = x_prompt, x_sample
    kp_l, vp_l, scp_l, ffp_l, mkp_l, mvp_l = [], [], [], [], [], []
    ks_l, vs_l, scs_l, ffs_l = [], [], [], []
    for l in range(DEPTH):
        lam_init = lambda_init_fn(l)
        shared = (g_mix[l], w_in[l], lam_q1[l], lam_k1[l], lam_q2[l], lam_k2[l], g_sub[l],
                  w_sc[l], w_out[l], g_x[l], w_xq[l], w_xo[l], g_ffn[l], w_up[l], w_gate[l],
                  w_ffconv[l], w_down[l])
        mk_p, mv_p = mem_kv(mem_prompt, g_mem[l], w_xk[l], w_xv[l])
        sc0 = jnp.zeros((b_p, SC_WIDTH - 1, CONV_WIDTH), hp.dtype)
        ff0 = jnp.zeros((b_p, FFN_CONV_WIDTH - 1, D_FF), hp.dtype)
        hp, k_p, v_p, sc_p, ff_p = layer(hp, pos_p, None, None, sc0, ff0, mk_p, mv_p,
                                         lam_init, *shared)
        kp_l.append(k_p); vp_l.append(v_p); scp_l.append(sc_p); ffp_l.append(ff_p)
        mkp_l.append(mk_p); mvp_l.append(mv_p)
        hs, k_s, v_s, sc_s, ff_s = layer(hs, pos_s, cache_attn_k[l], cache_attn_v[l],
                                         state_short_conv[l], state_ffn_conv[l],
                                         cache_mem_k[l], cache_mem_v[l], lam_init, *shared)
        ks_l.append(k_s); vs_l.append(v_s); scs_l.append(sc_s); ffs_l.append(ff_s)

    y_prompt = rms_norm(hp, g_final)
    y_sample = rms_norm(hs, g_final)
    return (y_prompt, y_sample,
            jnp.stack(kp_l), jnp.stack(vp_l), jnp.stack(scp_l), jnp.stack(ffp_l),
            jnp.stack(mkp_l), jnp.stack(mvp_l),
            jnp.stack(ks_l), jnp.stack(vs_l), jnp.stack(scs_l), jnp.stack(ffs_l))
```

```python
import functools
import math

import jax
import jax.numpy as jnp
from jax import lax
from jax.experimental import pallas as pl
from jax.experimental.pallas import tpu as pltpu

EPS = 1e-6
CHUNK = 64
ROPE_THETA = 500000.0
LANES = 128
SUBLANES = 8
VMEM_LIMIT = 56 * 1024 * 1024
BF16 = jnp.bfloat16
F32 = jnp.float32
NEG_INF = float("-inf")


def _lambda_init(layer_idx):
    return 0.8 - 0.6 * math.exp(-0.3 * layer_idx)


def _rms(x, g):
    return x * lax.rsqrt(jnp.mean(x * x, axis=-1, keepdims=True) + EPS) * g


def _dot(a, b):
    return jnp.dot(a, b, preferred_element_type=F32)


def _dot_nt(a, b):
    return lax.dot_general(a, b, (((1,), (1,)), ((), ())), preferred_element_type=F32)


def _params(n_grid):
    return pltpu.CompilerParams(dimension_semantics=("arbitrary",) * n_grid,
                                vmem_limit_bytes=VMEM_LIMIT)


def _const_spec(shape):
    nd = len(shape)
    return pl.BlockSpec(shape, lambda *_: (0,) * nd, pipeline_mode=pl.Buffered(1))


def _conv3(u, w, fix):
    p1 = pltpu.roll(u, 1, axis=0)
    p2 = pltpu.roll(u, 2, axis=0)
    p1, p2 = fix(p1, p2)
    return w[0:1, :] * p2 + w[1:2, :] * p1 + w[2:3, :] * u


def _carry_fix(carry_ref, cols, shape):
    row = lax.broadcasted_iota(jnp.int32, shape, 0)
    c0 = carry_ref[SUBLANES - 2:SUBLANES - 1, cols]
    c1 = carry_ref[SUBLANES - 1:SUBLANES, cols]

    def fix(p1, p2):
        p1 = jnp.where(row == 0, c1, p1)
        p2 = jnp.where(row == 0, c0, jnp.where(row == 1, c1, p2))
        return p1, p2
    return fix


def _state_fix(sp1, sp2, seq_len, shape):
    row = lax.broadcasted_iota(jnp.int32, shape, 0)
    t = lax.rem(row, seq_len)

    def fix(p1, p2):
        return jnp.where(t == 0, sp1, p1), jnp.where(t < 2, sp2, p2)
    return fix


def _memkv_kernel(m_ref, g_ref, w_ref, k_ref, v_ref):
    d = k_ref.shape[-1]
    h = _rms(m_ref[...], g_ref[...]).astype(BF16)
    kv = _dot(h, w_ref[...])
    k_ref[...] = kv[:, :d]
    v_ref[...] = kv[:, d:]


def _mem_kv(mem2d, g_mem, w_xkv, tm):
    n, d = mem2d.shape
    return pl.pallas_call(
        _memkv_kernel,
        grid=(n // tm,),
        in_specs=[pl.BlockSpec((tm, d), lambda i: (i, 0)),
                  _const_spec((1, d)),
                  _const_spec(w_xkv.shape)],
        out_specs=[pl.BlockSpec((tm, d), lambda i: (i, 0)),
                   pl.BlockSpec((tm, d), lambda i: (i, 0))],
        out_shape=[jax.ShapeDtypeStruct((n, d), F32)] * 2,
        compiler_params=_params(1),
        name="mem_kv",
    )(mem2d, g_mem, w_xkv)


def _inproj_body(x_ref, g_ref, w_ref, cos_ref, sa_ref, sb_ref, wsc_ref,
                 q_ref, k_ref, v_ref, y_ref, fix, dims):
    qc, kc, vc, cw = dims
    h = _rms(x_ref[...], g_ref[...]).astype(BF16)
    cos, sa, sb = cos_ref[...], sa_ref[...], sb_ref[...]

    def rope(t):
        return (t * cos + pltpu.roll(t, LANES - 8, axis=1) * sa
                + pltpu.roll(t, 8, axis=1) * sb)

    qk = _dot(h, w_ref[:, :qc + kc])
    for c in range(qc // LANES):
        blk = rope(qk[:, c * LANES:(c + 1) * LANES])
        q_ref[:, c * LANES:(c + 1) * LANES] = (blk * 0.125).astype(BF16)
    for c in range(kc // LANES):
        lo = qc + c * LANES
        k_ref[:, c * LANES:(c + 1) * LANES] = rope(qk[:, lo:lo + LANES])
    o = qc + kc
    vb = _dot(h, w_ref[:, o:o + vc + cw])
    v_ref[...] = vb[:, :vc]
    bg = vb[:, vc:]
    cx = _dot(h, w_ref[:, o + vc + cw:])
    u = cx[:, :cw] * cx[:, cw:]
    conv = _conv3(u, wsc_ref[...], fix(u.shape))
    y_ref[...] = (bg * conv).astype(BF16)
    return u


def _inproj_carry_kernel(x_ref, g_ref, w_ref, cos_ref, sa_ref, sb_ref, wsc_ref,
                         q_ref, k_ref, v_ref, y_ref, tail_ref, carry_ref, *, dims):
    @pl.when(pl.program_id(1) == 0)
    def _():
        carry_ref[...] = jnp.zeros_like(carry_ref)

    fix = lambda shape: _carry_fix(carry_ref, slice(None), shape)
    u = _inproj_body(x_ref, g_ref, w_ref, cos_ref, sa_ref, sb_ref, wsc_ref,
                     q_ref, k_ref, v_ref, y_ref, fix, dims)
    last = u[u.shape[0] - SUBLANES:, :]
    carry_ref[...] = last
    tail_ref[0] = last


def _inproj_state_kernel(x_ref, g_ref, w_ref, cos_ref, sa_ref, sb_ref, wsc_ref,
                         sp1_ref, sp2_ref, q_ref, k_ref, v_ref, y_ref, u_ref,
                         *, dims, seq_len):
    fix = lambda shape: _state_fix(sp1_ref[...], sp2_ref[...], seq_len, shape)
    u_ref[...] = _inproj_body(x_ref, g_ref, w_ref, cos_ref, sa_ref, sb_ref, wsc_ref,
                              q_ref, k_ref, v_ref, y_ref, fix, dims)


def _in_proj(x2d, g, w, tabs, w_sc, dims, *, batch, seq_len, tm, state=None):
    n, d = x2d.shape
    qc, kc, vc, cw = dims
    common_in = [None, _const_spec((1, d)), _const_spec(w.shape), None, None, None,
                 _const_spec(w_sc.shape)]
    out_shape = [jax.ShapeDtypeStruct((n, qc), BF16), jax.ShapeDtypeStruct((n, kc), F32),
                 jax.ShapeDtypeStruct((n, vc), F32), jax.ShapeDtypeStruct((n, cw), BF16)]
    if state is None:
        nj = seq_len // tm
        row = lambda b, j: (b * nj + j, 0)
        tab = pl.BlockSpec((tm, LANES), lambda b, j: (j, 0))
        in_specs = list(common_in)
        in_specs[0] = pl.BlockSpec((tm, d), row)
        in_specs[3:6] = [tab, tab, tab]
        out_specs = [pl.BlockSpec((tm, c), row) for c in (qc, kc, vc, cw)]
        out_specs.append(pl.BlockSpec((1, SUBLANES, cw), lambda b, j: (b, 0, 0)))
        out_shape.append(jax.ShapeDtypeStruct((batch, SUBLANES, cw), F32))
        return pl.pallas_call(
            functools.partial(_inproj_carry_kernel, dims=dims),
            grid=(batch, nj), in_specs=in_specs, out_specs=out_specs, out_shape=out_shape,
            scratch_shapes=[pltpu.VMEM((SUBLANES, cw), F32)],
            compiler_params=_params(2), name="in_proj_prompt",
        )(x2d, g, w, *tabs, w_sc)
    sp1, sp2 = state
    row = lambda i: (i, 0)
    tab = pl.BlockSpec((tm, LANES), row)
    in_specs = list(common_in)
    in_specs[0] = pl.BlockSpec((tm, d), row)
    in_specs[3:6] = [tab, tab, tab]
    in_specs += [pl.BlockSpec((tm, cw), row)] * 2
    out_specs = [pl.BlockSpec((tm, c), row) for c in (qc, kc, vc, cw, cw)]
    out_shape.append(jax.ShapeDtypeStruct((n, cw), F32))
    return pl.pallas_call(
        functools.partial(_inproj_state_kernel, dims=dims, seq_len=seq_len),
        grid=(n // tm,), in_specs=in_specs, out_specs=out_specs, out_shape=out_shape,
        compiler_params=_params(1), name="in_proj_sample",
    )(x2d, g, w, *tabs, w_sc, sp1, sp2)


def _diff_lambda(lq1_ref, lk1_ref, lq2_ref, lk2_ref, lam_init):
    a = jnp.sum(lq1_ref[...] * lk1_ref[...], axis=-1, keepdims=True)
    b = jnp.sum(lq2_ref[...] * lk2_ref[...], axis=-1, keepdims=True)
    return jnp.exp(a) - jnp.exp(b) + lam_init


def _stack_maps(qt):
    lane = lax.broadcasted_iota(jnp.int32, qt.shape, 1)
    first = lane < (LANES // 2)
    zero = jnp.zeros_like(qt)
    return jnp.concatenate([jnp.where(first, qt, zero), jnp.where(first, zero, qt)], axis=0)


def _finish_head(acc, l, lam, g, lam_init, tq):
    o = acc[:tq] / l[:tq] - lam * (acc[tq:] / l[tq:])
    return _rms(o, g) * (1.0 - lam_init)


def _attn_prompt_kernel(q_ref, k_ref, v_ref, lq1_ref, lk1_ref, lq2_ref, lk2_ref, g_ref,
                        o_ref, kb_ref, vb_ref, *, tq, lam_init):
    seq = q_ref.shape[1]
    kb_ref[...] = k_ref[0].astype(BF16)
    vb_ref[...] = v_ref[0].astype(BF16)
    lam = _diff_lambda(lq1_ref, lk1_ref, lq2_ref, lk2_ref, lam_init)
    g = g_ref[...]
    r = lax.broadcasted_iota(jnp.int32, (2 * tq, tq), 0)
    c = lax.broadcasted_iota(jnp.int32, (2 * tq, tq), 1)
    shift = CHUNK.bit_length() - 1
    diag_mask = ((r & (tq - 1)) >> shift) >= (c >> shift)

    def step(qm, ks, carry, masked):
        m, l, acc = carry
        s = _dot_nt(qm, kb_ref[pl.ds(ks, tq), :])
        if masked:
            s = jnp.where(diag_mask, s, NEG_INF)
        m_new = jnp.maximum(m, jnp.max(s, axis=-1, keepdims=True))
        alpha = jnp.exp(m - m_new)
        e = jnp.exp(s - m_new)
        l = alpha * l + jnp.sum(e, axis=-1, keepdims=True)
        acc = alpha * acc + _dot(e.astype(BF16), vb_ref[pl.ds(ks, tq), :])
        return m_new, l, acc

    def q_body(i, _):
        qs = pl.multiple_of(i * tq, tq)
        qm = _stack_maps(q_ref[0, pl.ds(qs, tq), :])
        init = (jnp.full((2 * tq, 1), NEG_INF, F32), jnp.zeros((2 * tq, 1), F32),
                jnp.zeros((2 * tq, LANES), F32))
        carry = lax.fori_loop(
            0, i, lambda j, cr: step(qm, pl.multiple_of(j * tq, tq), cr, False), init)
        _, l, acc = step(qm, qs, carry, True)
        o_ref[0, pl.ds(qs, tq), :] = _finish_head(acc, l, lam, g, lam_init, tq).astype(BF16)
        return 0

    lax.fori_loop(0, seq // tq, q_body, 0)


def _attn_prompt(q3, k3, v3, lams, g_sub, *, tq, lam_init):
    b, seq, width = q3.shape
    heads = width // LANES
    blk = lambda: pl.BlockSpec((1, seq, LANES), lambda i, h: (i, 0, h))
    lam_spec = _const_spec(lams[0].shape)
    return pl.pallas_call(
        functools.partial(_attn_prompt_kernel, tq=tq, lam_init=lam_init),
        grid=(b, heads),
        in_specs=[blk(), blk(), blk(), lam_spec, lam_spec, lam_spec, lam_spec,
                  _const_spec(g_sub.shape)],
        out_specs=blk(),
        out_shape=jax.ShapeDtypeStruct((b, seq, width), BF16),
        scratch_shapes=[pltpu.VMEM((seq, LANES), BF16), pltpu.VMEM((seq, LANES), BF16)],
        compiler_params=_params(2), name="attn_prompt",
    )(q3, k3, v3, *lams, g_sub)


def _attn_sample_kernel(q_ref, kn_ref, vn_ref, kp_ref, vp_ref, lq1_ref, lk1_ref, lq2_ref,
                        lk2_ref, g_ref, o_ref, *, lam_init):
    t_new = q_ref.shape[1]
    heads = q_ref.shape[2] // LANES
    lam = _diff_lambda(lq1_ref, lk1_ref, lq2_ref, lk2_ref, lam_init)
    g = g_ref[...]
    pad = jnp.zeros((LANES - t_new, LANES), BF16)
    lane = lax.broadcasted_iota(jnp.int32, (2 * t_new, LANES), 1)
    for h in range(heads):
        cols = slice(h * LANES, (h + 1) * LANES)
        qm = _stack_maps(q_ref[0, :, cols])
        kn = jnp.concatenate([kn_ref[0, :, cols].astype(BF16), pad], axis=0)
        vn = jnp.concatenate([vn_ref[0, :, cols].astype(BF16), pad], axis=0)
        s_p = _dot_nt(qm, kp_ref[0, :, cols].astype(BF16))
        s_n = jnp.where(lane < t_new, _dot_nt(qm, kn), NEG_INF)
        m = jnp.maximum(jnp.max(s_p, axis=-1, keepdims=True),
                        jnp.max(s_n, axis=-1, keepdims=True))
        e_p = jnp.exp(s_p - m)
        e_n = jnp.exp(s_n - m)
        l = jnp.sum(e_p, axis=-1, keepdims=True) + jnp.sum(e_n, axis=-1, keepdims=True)
        acc = (_dot(e_p.astype(BF16), vp_ref[0, :, cols].astype(BF16))
               + _dot(e_n.astype(BF16), vn))
        o_ref[0, :, cols] = _finish_head(acc, l, lam, g, lam_init, t_new).astype(BF16)


def _attn_sample(q3, kn3, vn3, kp3, vp3, lams, g_sub, *, lam_init):
    b, t_new, width = q3.shape
    past = kp3.shape[1]
    new = lambda: pl.BlockSpec((1, t_new, width), lambda i: (i, 0, 0))
    old = lambda: pl.BlockSpec((1, past, width), lambda i: (i, 0, 0))
    lam_spec = _const_spec(lams[0].shape)
    return pl.pallas_call(
        functools.partial(_attn_sample_kernel, lam_init=lam_init),
        grid=(b,),
        in_specs=[new(), new(), new(), old(), old(), lam_spec, lam_spec, lam_spec, lam_spec,
                  _const_spec(g_sub.shape)],
        out_specs=new(),
        out_shape=jax.ShapeDtypeStruct((b, t_new, width), BF16),
        compiler_params=_params(1), name="attn_sample",
    )(q3, kn3, vn3, kp3, vp3, *lams, g_sub)


def _post_kernel(x_ref, o_ref, y_ref, wout_ref, gx_ref, wxq_ref, mk_ref, mv_ref, wxo_ref,
                 out_ref, xo_ref, *, heads):
    aw = o_ref.shape[-1]
    x1 = x_ref[...] + _dot(o_ref[...], wout_ref[:aw, :]) + _dot(y_ref[...], wout_ref[aw:, :])
    hd = x1.shape[-1] // heads
    hq = (_dot(_rms(x1, gx_ref[...]).astype(BF16), wxq_ref[...]) * (hd ** -0.5)).astype(BF16)
    for h in range(heads):
        cols = slice(h * hd, (h + 1) * hd)
        s = _dot_nt(hq[:, cols], mk_ref[0, :, cols].astype(BF16))
        e = jnp.exp(s - jnp.max(s, axis=-1, keepdims=True))
        l = jnp.sum(e, axis=-1, keepdims=True)
        xo = _dot(e.astype(BF16), mv_ref[0, :, cols].astype(BF16)) / l
        xo_ref[:, cols] = xo.astype(BF16)
    out_ref[...] = x1 + _dot(xo_ref[...], wxo_ref[...])


def _post(x2d, o2d, y2d, w_out, g_x, w_xq, mk3, mv3, w_xo, *, batch, seq_len, tm, heads):
    n, d = x2d.shape
    aw, cw = o2d.shape[-1], y2d.shape[-1]
    nj = seq_len // tm
    row = lambda b, j: (b * nj + j, 0)
    mem = pl.BlockSpec((1,) + mk3.shape[1:], lambda b, j: (b, 0, 0))
    return pl.pallas_call(
        functools.partial(_post_kernel, heads=heads),
        grid=(batch, nj),
        in_specs=[pl.BlockSpec((tm, d), row), pl.BlockSpec((tm, aw), row),
                  pl.BlockSpec((tm, cw), row), _const_spec(w_out.shape), _const_spec((1, d)),
                  _const_spec(w_xq.shape), mem, mem, _const_spec(w_xo.shape)],
        out_specs=pl.BlockSpec((tm, d), row),
        out_shape=jax.ShapeDtypeStruct((n, d), F32),
        scratch_shapes=[pltpu.VMEM((tm, d), BF16)],
        compiler_params=_params(2), name="post",
    )(x2d, o2d, y2d, w_out, g_x, w_xq, mk3, mv3, w_xo)


FF_CHUNK = 256


def _ffn_body(x_ref, g_ref, wup_ref, wgate_ref, wconv_ref, wdown_ref, gfin_ref, y_ref,
              a_ref, fix, emit_up, final_norm):
    x = x_ref[...]
    hf = _rms(x, g_ref[...]).astype(BF16)
    dff = wup_ref.shape[-1]
    for c in range(dff // FF_CHUNK):
        cols = slice(c * FF_CHUNK, (c + 1) * FF_CHUNK)
        up = _dot(hf, wup_ref[:, cols])
        emit_up(cols, up)
        uc = _conv3(up, wconv_ref[:, cols], fix(cols, up.shape))
        gate = _dot(hf, wgate_ref[:, cols])
        a_ref[:, cols] = (uc * jax.nn.sigmoid(uc) * gate).astype(BF16)
    x3 = x + _dot(a_ref[...], wdown_ref[...])
    y_ref[...] = _rms(x3, gfin_ref[...]) if final_norm else x3


def _ffn_carry_kernel(x_ref, g_ref, wup_ref, wgate_ref, wconv_ref, wdown_ref, gfin_ref,
                      y_ref, tail_ref, a_ref, carry_ref, new_ref, *, final_norm):
    @pl.when(pl.program_id(1) == 0)
    def _():
        carry_ref[...] = jnp.zeros_like(carry_ref)

    def emit_up(cols, up):
        new_ref[:, cols] = up[up.shape[0] - SUBLANES:, :]

    fix = lambda cols, shape: _carry_fix(carry_ref, cols, shape)
    _ffn_body(x_ref, g_ref, wup_ref, wgate_ref, wconv_ref, wdown_ref, gfin_ref, y_ref,
              a_ref, fix, emit_up, final_norm)
    carry_ref[...] = new_ref[...]
    tail_ref[0] = new_ref[...]


def _ffn_state_kernel(x_ref, g_ref, wup_ref, wgate_ref, wconv_ref, wdown_ref, gfin_ref,
                      sp1_ref, sp2_ref, y_ref, up_ref, a_ref, *, seq_len, final_norm):
    def emit_up(cols, up):
        up_ref[:, cols] = up

    fix = lambda cols, shape: _state_fix(sp1_ref[:, cols], sp2_ref[:, cols], seq_len, shape)
    _ffn_body(x_ref, g_ref, wup_ref, wgate_ref, wconv_ref, wdown_ref, gfin_ref, y_ref,
              a_ref, fix, emit_up, final_norm)


def _ffn(x2d, g_ffn, w_up, w_gate, w_conv, w_down, g_final, *, batch, seq_len, tm, final_norm,
         state=None):
    n, d = x2d.shape
    dff = w_up.shape[-1]
    weights = [_const_spec((1, d)), _const_spec(w_up.shape), _const_spec(w_gate.shape),
               _const_spec(w_conv.shape), _const_spec(w_down.shape), _const_spec((1, d))]
    if state is None:
        nj = seq_len // tm
        row = lambda b, j: (b * nj + j, 0)
        return pl.pallas_call(
            functools.partial(_ffn_carry_kernel, final_norm=final_norm),
            grid=(batch, nj),
            in_specs=[pl.BlockSpec((tm, d), row)] + weights,
            out_specs=[pl.BlockSpec((tm, d), row),
                       pl.BlockSpec((1, SUBLANES, dff), lambda b, j: (b, 0, 0))],
            out_shape=[jax.ShapeDtypeStruct((n, d), F32),
                       jax.ShapeDtypeStruct((batch, SUBLANES, dff), F32)],
            scratch_shapes=[pltpu.VMEM((tm, dff), BF16), pltpu.VMEM((SUBLANES, dff), F32),
                            pltpu.VMEM((SUBLANES, dff), F32)],
            compiler_params=_params(2), name="ffn_prompt",
        )(x2d, g_ffn, w_up, w_gate, w_conv, w_down, g_final)
    sp1, sp2 = state
    row = lambda i: (i, 0)
    return pl.pallas_call(
        functools.partial(_ffn_state_kernel, seq_len=seq_len, final_norm=final_norm),
        grid=(n // tm,),
        in_specs=[pl.BlockSpec((tm, d), row)] + weights + [pl.BlockSpec((tm, dff), row)] * 2,
        out_specs=[pl.BlockSpec((tm, d), row), pl.BlockSpec((tm, dff), row)],
        out_shape=[jax.ShapeDtypeStruct((n, d), F32), jax.ShapeDtypeStruct((n, dff), F32)],
        scratch_shapes=[pltpu.VMEM((tm, dff), BF16)],
        compiler_params=_params(1), name="ffn_sample",
    )(x2d, g_ffn, w_up, w_gate, w_conv, w_down, g_final, sp1, sp2)


def _rope_tables(pos, qk_head_dim):
    rot = qk_head_dim // 4
    half = rot // 2
    inv = 1.0 / (ROPE_THETA ** (jnp.arange(half, dtype=F32) * 2.0 / rot))
    ang = pos.astype(F32)[:, None] * inv[None, :]
    cos, sin = jnp.cos(ang), jnp.sin(ang)
    t = pos.shape[0]
    zeros = lambda w: jnp.zeros((t, w), F32)
    c = jnp.concatenate([cos, cos, jnp.ones((t, qk_head_dim - rot), F32)], axis=1)
    a = jnp.concatenate([-sin, zeros(qk_head_dim - half)], axis=1)
    b = jnp.concatenate([zeros(half), sin, zeros(qk_head_dim - rot)], axis=1)
    rep = LANES // qk_head_dim
    return tuple(jnp.tile(m, (1, rep)) for m in (c, a, b))


def _expand_state(state, seq_len):
    b, _, c = state.shape
    z = jnp.zeros((b, seq_len, c), state.dtype)
    sp1 = z.at[:, 0].set(state[:, 1])
    sp2 = z.at[:, 0].set(state[:, 0]).at[:, 1].set(state[:, 1])
    return sp1.reshape(b * seq_len, c), sp2.reshape(b * seq_len, c)


def kernel(x_prompt, x_sample, cache_attn_k, cache_attn_v, state_short_conv, state_ffn_conv,
           cache_mem_k, cache_mem_v, mem_prompt, g_mix, w_in, lam_q1, lam_k1, lam_q2, lam_k2,
           g_sub, w_sc, w_out, g_mem, g_x, w_xq, w_xk, w_xv, w_xo, g_ffn, w_up, w_gate,
           w_ffconv, w_down, g_final):
    depth = w_in.shape[0]
    bp, seq, d = x_prompt.shape
    bs, t_new, _ = x_sample.shape
    past = cache_attn_k.shape[2]
    heads_a, qk_dim = cache_attn_k.shape[3], cache_attn_k.shape[5]
    v_dim = cache_attn_v.shape[4]
    cw = state_short_conv.shape[-1]
    dff = state_ffn_conv.shape[-1]
    n_mem, heads_x, x_dim = cache_mem_k.shape[2:]
    qc = kc = heads_a * 2 * qk_dim
    vc = heads_a * v_dim
    dims = (qc, kc, vc, cw)
    assert 2 * qk_dim == LANES and v_dim == LANES and qk_dim == CHUNK

    tabs_p = _rope_tables(jnp.arange(seq, dtype=jnp.int32), qk_dim)
    tabs_s = tuple(jnp.tile(m, (bs, 1)) for m in
                   _rope_tables(past + jnp.arange(t_new, dtype=jnp.int32), qk_dim))
    row = lambda v: v.reshape(1, -1)

    hp = x_prompt.reshape(bp * seq, d)
    hs = x_sample.reshape(bs * t_new, d)
    outs_p = [[] for _ in range(6)]
    outs_s = [[] for _ in range(4)]
    for l in range(depth):
        lam_init = _lambda_init(l)
        wi, wo = w_in[l].astype(BF16), w_out[l].astype(BF16)
        wq, wxo_b = w_xq[l].astype(BF16), w_xo[l].astype(BF16)
        wkv = jnp.concatenate([w_xk[l], w_xv[l]], axis=1).astype(BF16)
        wu, wg, wd = w_up[l].astype(BF16), w_gate[l].astype(BF16), w_down[l].astype(BF16)
        lams = tuple(row(v[l]) for v in (lam_q1, lam_k1, lam_q2, lam_k2))
        gsub = row(g_sub[l])

        mk, mv = _mem_kv(mem_prompt.reshape(bp * n_mem, d), row(g_mem[l]), wkv, 512)
        q, k, v, ysc, sc_tail = _in_proj(hp, row(g_mix[l]), wi, tabs_p, w_sc[l], dims,
                                         batch=bp, seq_len=seq, tm=512)
        o = _attn_prompt(q.reshape(bp, seq, qc), k.reshape(bp, seq, kc),
                         v.reshape(bp, seq, vc), lams, gsub, tq=256, lam_init=lam_init)
        x2 = _post(hp, o.reshape(bp * seq, vc), ysc, wo, row(g_x[l]), wq,
                   mk.reshape(bp, n_mem, d), mv.reshape(bp, n_mem, d), wxo_b,
                   batch=bp, seq_len=seq, tm=512, heads=heads_x)
        hp, ff_tail = _ffn(x2, row(g_ffn[l]), wu, wg, w_ffconv[l], wd, row(g_final),
                           batch=bp, seq_len=seq, tm=512, final_norm=l == depth - 1)
        outs_p[0].append(k.reshape(bp, seq, heads_a, 2, qk_dim))
        outs_p[1].append(v.reshape(bp, seq, heads_a, v_dim))
        outs_p[2].append(sc_tail[:, SUBLANES - 2:])
        outs_p[3].append(ff_tail[:, SUBLANES - 2:])
        outs_p[4].append(mk.reshape(bp, n_mem, heads_x, x_dim))
        outs_p[5].append(mv.reshape(bp, n_mem, heads_x, x_dim))

        n_s = bs * t_new
        qs, ks, vs, yscs, u_s = _in_proj(
            hs, row(g_mix[l]), wi, tabs_s, w_sc[l], dims, batch=bs, seq_len=t_new, tm=n_s,
            state=_expand_state(state_short_conv[l], t_new))
        o_s = _attn_sample(qs.reshape(bs, t_new, qc), ks.reshape(bs, t_new, kc),
                           vs.reshape(bs, t_new, vc),
                           cache_attn_k[l].reshape(bs, past, kc),
                           cache_attn_v[l].reshape(bs, past, vc), lams, gsub,
                           lam_init=lam_init)
        x2s = _post(hs, o_s.reshape(n_s, vc), yscs, wo, row(g_x[l]), wq,
                    cache_mem_k[l].reshape(bs, n_mem, d), cache_mem_v[l].reshape(bs, n_mem, d),
                    wxo_b, batch=bs, seq_len=t_new, tm=t_new, heads=heads_x)
        hs, up_s = _ffn(x2s, row(g_ffn[l]), wu, wg, w_ffconv[l], wd, row(g_final),
                        batch=bs, seq_len=t_new, tm=n_s, final_norm=l == depth - 1,
                        state=_expand_state(state_ffn_conv[l], t_new))
        outs_s[0].append(ks.reshape(bs, t_new, heads_a, 2, qk_dim))
        outs_s[1].append(vs.reshape(bs, t_new, heads_a, v_dim))
        outs_s[2].append(u_s.reshape(bs, t_new, cw)[:, t_new - 2:])
        outs_s[3].append(up_s.reshape(bs, t_new, dff)[:, t_new - 2:])

    return (hp.reshape(bp, seq, d), hs.reshape(bs, t_new, d),
            *(jnp.stack(o) for o in outs_p), *(jnp.stack(o) for o in outs_s))
```

```python
import functools
import math

import jax
import jax.numpy as jnp
from jax import lax
from jax.experimental import pallas as pl
from jax.experimental.pallas import tpu as pltpu

EPS = 1e-6
CHUNK = 64
ROPE_THETA = 500000.0
LANES = 128
SUBLANES = 8
VMEM_LIMIT = 56 * 1024 * 1024
BF16 = jnp.bfloat16
F32 = jnp.float32
NEG_INF = float("-inf")


def _lambda_init(layer_idx):
    return 0.8 - 0.6 * math.exp(-0.3 * layer_idx)


def _rms(x, g):
    return x * lax.rsqrt(jnp.mean(x * x, axis=-1, keepdims=True) + EPS) * g


def _dot(a, b):
    return jnp.dot(a, b, preferred_element_type=F32)


def _dot_nt(a, b):
    return lax.dot_general(a, b, (((1,), (1,)), ((), ())), preferred_element_type=F32)


def _params(n_grid):
    return pltpu.CompilerParams(dimension_semantics=("arbitrary",) * n_grid,
                                vmem_limit_bytes=VMEM_LIMIT)


def _const_spec(shape):
    nd = len(shape)
    return pl.BlockSpec(shape, lambda *_: (0,) * nd, pipeline_mode=pl.Buffered(1))


def _conv3(u, w, fix):
    p1 = pltpu.roll(u, 1, axis=0)
    p2 = pltpu.roll(u, 2, axis=0)
    p1, p2 = fix(p1, p2)
    return w[0:1, :] * p2 + w[1:2, :] * p1 + w[2:3, :] * u


def _carry_fix(carry_ref, cols, shape):
    row = lax.broadcasted_iota(jnp.int32, shape, 0)
    c0 = carry_ref[SUBLANES - 2:SUBLANES - 1, cols]
    c1 = carry_ref[SUBLANES - 1:SUBLANES, cols]

    def fix(p1, p2):
        p1 = jnp.where(row == 0, c1, p1)
        p2 = jnp.where(row == 0, c0, jnp.where(row == 1, c1, p2))
        return p1, p2
    return fix


def _state_fix(sp1, sp2, seq_len, shape):
    row = lax.broadcasted_iota(jnp.int32, shape, 0)
    t = lax.rem(row, seq_len)

    def fix(p1, p2):
        return jnp.where(t == 0, sp1, p1), jnp.where(t < 2, sp2, p2)
    return fix


def _memkv_kernel(m_ref, g_ref, w_ref, k_ref, v_ref):
    d = k_ref.shape[-1]
    h = _rms(m_ref[...], g_ref[...]).astype(BF16)
    kv = _dot(h, w_ref[...])
    k_ref[...] = kv[:, :d]
    v_ref[...] = kv[:, d:]


def _mem_kv(mem2d, g_mem, w_xkv, tm):
    n, d = mem2d.shape
    return pl.pallas_call(
        _memkv_kernel,
        grid=(n // tm,),
        in_specs=[pl.BlockSpec((tm, d), lambda i: (i, 0)),
                  _const_spec((1, d)),
                  _const_spec(w_xkv.shape)],
        out_specs=[pl.BlockSpec((tm, d), lambda i: (i, 0)),
                   pl.BlockSpec((tm, d), lambda i: (i, 0))],
        out_shape=[jax.ShapeDtypeStruct((n, d), F32)] * 2,
        compiler_params=_params(1),
        name="mem_kv",
    )(mem2d, g_mem, w_xkv)


def _inproj_body(x_ref, g_ref, w_ref, cos_ref, sa_ref, sb_ref, wsc_ref,
                 q_ref, put_k, put_v, y_ref, fix, dims):
    qc, kc, vc, cw = dims
    h = _rms(x_ref[...], g_ref[...]).astype(BF16)
    cos, sa, sb = cos_ref[...], sa_ref[...], sb_ref[...]

    def rope(t):
        return (t * cos + pltpu.roll(t, LANES - 8, axis=1) * sa
                + pltpu.roll(t, 8, axis=1) * sb)

    qk = _dot(h, w_ref[:, :qc + kc])
    for c in range(qc // LANES):
        blk = rope(qk[:, c * LANES:(c + 1) * LANES])
        q_ref[:, c * LANES:(c + 1) * LANES] = (blk * 0.125).astype(BF16)
    for c in range(kc // LANES):
        lo = qc + c * LANES
        put_k(c, rope(qk[:, lo:lo + LANES]))
    o = qc + kc
    vb = _dot(h, w_ref[:, o:o + vc + cw])
    put_v(vb[:, :vc])
    bg = vb[:, vc:]
    cx = _dot(h, w_ref[:, o + vc + cw:])
    u = cx[:, :cw] * cx[:, cw:]
    conv = _conv3(u, wsc_ref[...], fix(u.shape))
    y_ref[...] = (bg * conv).astype(BF16)
    return u


def _inproj_carry_kernel(x_ref, g_ref, w_ref, cos_ref, sa_ref, sb_ref, wsc_ref,
                         q_ref, kt_ref, v4_ref, vb_ref, y_ref, tail_ref, carry_ref, *, dims):
    @pl.when(pl.program_id(1) == 0)
    def _():
        carry_ref[...] = jnp.zeros_like(carry_ref)

    def put_k(c, blk):
        kt_ref[0, c * LANES:(c + 1) * LANES, :] = blk.T

    def put_v(v):
        vb_ref[...] = v.astype(BF16)
        heads = v.shape[1] // LANES
        for hd in range(heads):
            v4_ref[pl.ds(hd, v.shape[0], stride=heads), :] = v[:, hd * LANES:(hd + 1) * LANES]

    fix = lambda shape: _carry_fix(carry_ref, slice(None), shape)
    u = _inproj_body(x_ref, g_ref, w_ref, cos_ref, sa_ref, sb_ref, wsc_ref,
                     q_ref, put_k, put_v, y_ref, fix, dims)
    last = u[u.shape[0] - SUBLANES:, :]
    carry_ref[...] = last
    tail_ref[0] = last


def _inproj_state_kernel(x_ref, g_ref, w_ref, cos_ref, sa_ref, sb_ref, wsc_ref,
                         sp1_ref, sp2_ref, q_ref, k_ref, v_ref, y_ref, u_ref,
                         *, dims, seq_len):
    def put_k(c, blk):
        k_ref[:, c * LANES:(c + 1) * LANES] = blk

    def put_v(v):
        v_ref[...] = v

    fix = lambda shape: _state_fix(sp1_ref[...], sp2_ref[...], seq_len, shape)
    u_ref[...] = _inproj_body(x_ref, g_ref, w_ref, cos_ref, sa_ref, sb_ref, wsc_ref,
                              q_ref, put_k, put_v, y_ref, fix, dims)


def _in_proj(x2d, g, w, tabs, w_sc, dims, *, batch, seq_len, tm, state=None):
    n, d = x2d.shape
    qc, kc, vc, cw = dims
    common_in = [None, _const_spec((1, d)), _const_spec(w.shape), None, None, None,
                 _const_spec(w_sc.shape)]
    if state is None:
        nj = seq_len // tm
        heads = vc // LANES
        row = lambda b, j: (b * nj + j, 0)
        tab = pl.BlockSpec((tm, LANES), lambda b, j: (j, 0))
        in_specs = list(common_in)
        in_specs[0] = pl.BlockSpec((tm, d), row)
        in_specs[3:6] = [tab, tab, tab]
        out_specs = [pl.BlockSpec((tm, qc), row),
                     pl.BlockSpec((1, kc, tm), lambda b, j: (b, 0, j)),
                     pl.BlockSpec((tm * heads, LANES), row),
                     pl.BlockSpec((tm, vc), row), pl.BlockSpec((tm, cw), row),
                     pl.BlockSpec((1, SUBLANES, cw), lambda b, j: (b, 0, 0))]
        out_shape = [jax.ShapeDtypeStruct((n, qc), BF16),
                     jax.ShapeDtypeStruct((batch, kc, seq_len), F32),
                     jax.ShapeDtypeStruct((n * heads, LANES), F32),
                     jax.ShapeDtypeStruct((n, vc), BF16), jax.ShapeDtypeStruct((n, cw), BF16),
                     jax.ShapeDtypeStruct((batch, SUBLANES, cw), F32)]
        return pl.pallas_call(
            functools.partial(_inproj_carry_kernel, dims=dims),
            grid=(batch, nj), in_specs=in_specs, out_specs=out_specs, out_shape=out_shape,
            scratch_shapes=[pltpu.VMEM((SUBLANES, cw), F32)],
            compiler_params=_params(2), name="in_proj_prompt",
        )(x2d, g, w, *tabs, w_sc)
    sp1, sp2 = state
    row = lambda i: (i, 0)
    tab = pl.BlockSpec((tm, LANES), row)
    in_specs = list(common_in)
    in_specs[0] = pl.BlockSpec((tm, d), row)
    in_specs[3:6] = [tab, tab, tab]
    in_specs += [pl.BlockSpec((tm, cw), row)] * 2
    out_specs = [pl.BlockSpec((tm, c), row) for c in (qc, kc, vc, cw, cw)]
    out_shape = [jax.ShapeDtypeStruct((n, qc), BF16), jax.ShapeDtypeStruct((n, kc), F32),
                 jax.ShapeDtypeStruct((n, vc), F32), jax.ShapeDtypeStruct((n, cw), BF16),
                 jax.ShapeDtypeStruct((n, cw), F32)]
    return pl.pallas_call(
        functools.partial(_inproj_state_kernel, dims=dims, seq_len=seq_len),
        grid=(n // tm,), in_specs=in_specs, out_specs=out_specs, out_shape=out_shape,
        compiler_params=_params(1), name="in_proj_sample",
    )(x2d, g, w, *tabs, w_sc, sp1, sp2)


def _diff_lambda(lq1_ref, lk1_ref, lq2_ref, lk2_ref, lam_init):
    a = jnp.sum(lq1_ref[...] * lk1_ref[...], axis=-1, keepdims=True)
    b = jnp.sum(lq2_ref[...] * lk2_ref[...], axis=-1, keepdims=True)
    return jnp.exp(a) - jnp.exp(b) + lam_init


def _stack_maps(qt):
    lane = lax.broadcasted_iota(jnp.int32, qt.shape, 1)
    first = lane < (LANES // 2)
    zero = jnp.zeros_like(qt)
    return jnp.concatenate([jnp.where(first, qt, zero), jnp.where(first, zero, qt)], axis=0)


def _finish_head(acc, l, lam, g, lam_init, tq):
    o = acc[:tq] / l[:tq] - lam * (acc[tq:] / l[tq:])
    return _rms(o, g) * (1.0 - lam_init)


def _attn_prompt_kernel(q_ref, k_ref, v_ref, lq1_ref, lk1_ref, lq2_ref, lk2_ref, g_ref,
                        o_ref, kb_ref, *, tq, lam_init):
    seq = q_ref.shape[1]
    kb_ref[...] = k_ref[0].astype(BF16)
    lam = _diff_lambda(lq1_ref, lk1_ref, lq2_ref, lk2_ref, lam_init)
    g = g_ref[...]
    r = lax.broadcasted_iota(jnp.int32, (2 * tq, tq), 0)
    c = lax.broadcasted_iota(jnp.int32, (2 * tq, tq), 1)
    shift = CHUNK.bit_length() - 1
    diag_mask = ((r & (tq - 1)) >> shift) >= (c >> shift)

    for i in range(seq // tq):
        lo = i * tq
        qm = _stack_maps(q_ref[0, lo:lo + tq, :])
        s_d = jnp.where(diag_mask, _dot(qm, kb_ref[:, lo:lo + tq]), NEG_INF)
        m = jnp.max(s_d, axis=-1, keepdims=True)
        if i:
            s_o = _dot(qm, kb_ref[:, :lo])
            m = jnp.maximum(m, jnp.max(s_o, axis=-1, keepdims=True))
        e_d = jnp.exp(s_d - m)
        l = jnp.sum(e_d, axis=-1, keepdims=True)
        acc = _dot(e_d.astype(BF16), v_ref[0, lo:lo + tq, :])
        if i:
            e_o = jnp.exp(s_o - m)
            l = l + jnp.sum(e_o, axis=-1, keepdims=True)
            acc = acc + _dot(e_o.astype(BF16), v_ref[0, :lo, :])
        o_ref[0, lo:lo + tq, :] = _finish_head(acc, l, lam, g, lam_init, tq).astype(BF16)


def _attn_prompt(q3, kt3, v3, lams, g_sub, *, tq, lam_init):
    b, seq, width = q3.shape
    heads = width // LANES
    blk = lambda: pl.BlockSpec((1, seq, LANES), lambda i, h: (i, 0, h))
    lam_spec = _const_spec(lams[0].shape)
    return pl.pallas_call(
        functools.partial(_attn_prompt_kernel, tq=tq, lam_init=lam_init),
        grid=(b, heads),
        in_specs=[blk(), pl.BlockSpec((1, LANES, seq), lambda i, h: (i, h, 0)), blk(),
                  lam_spec, lam_spec, lam_spec, lam_spec, _const_spec(g_sub.shape)],
        out_specs=blk(),
        out_shape=jax.ShapeDtypeStruct((b, seq, width), BF16),
        scratch_shapes=[pltpu.VMEM((LANES, seq), BF16)],
        compiler_params=_params(2), name="attn_prompt",
    )(q3, kt3, v3, *lams, g_sub)


def _attn_sample_kernel(q_ref, kn_ref, vn_ref, kp_ref, vp_ref, lq1_ref, lk1_ref, lq2_ref,
                        lk2_ref, g_ref, o_ref, *, lam_init):
    t_new = q_ref.shape[1]
    heads = q_ref.shape[2] // LANES
    past = kp_ref.shape[2]
    lam = _diff_lambda(lq1_ref, lk1_ref, lq2_ref, lk2_ref, lam_init)
    g = g_ref[...]
    pad = jnp.zeros((LANES - t_new, LANES), BF16)
    lane = lax.broadcasted_iota(jnp.int32, (2 * t_new, LANES), 1)
    for h in range(heads):
        cols = slice(h * LANES, (h + 1) * LANES)
        qm = _stack_maps(q_ref[0, :, cols])
        kn = jnp.concatenate([kn_ref[0, :, cols].astype(BF16), pad], axis=0)
        vn = jnp.concatenate([vn_ref[0, :, cols].astype(BF16), pad], axis=0)
        s_p = _dot(qm, kp_ref[0, cols, :].astype(BF16))
        vp = vp_ref[0, pl.ds(h, past, stride=heads), :].astype(BF16)
        s_n = jnp.where(lane < t_new, _dot_nt(qm, kn), NEG_INF)
        m = jnp.maximum(jnp.max(s_p, axis=-1, keepdims=True),
                        jnp.max(s_n, axis=-1, keepdims=True))
        e_p = jnp.exp(s_p - m)
        e_n = jnp.exp(s_n - m)
        l = jnp.sum(e_p, axis=-1, keepdims=True) + jnp.sum(e_n, axis=-1, keepdims=True)
        acc = _dot(e_p.astype(BF16), vp) + _dot(e_n.astype(BF16), vn)
        o_ref[0, :, cols] = _finish_head(acc, l, lam, g, lam_init, t_new).astype(BF16)


def _attn_sample(q3, kn3, vn3, kpt3, vp4, lams, g_sub, *, lam_init):
    b, t_new, width = q3.shape
    new = lambda: pl.BlockSpec((1, t_new, width), lambda i: (i, 0, 0))
    whole = lambda a: pl.BlockSpec((1,) + a.shape[1:], lambda i: (i, 0, 0))
    lam_spec = _const_spec(lams[0].shape)
    return pl.pallas_call(
        functools.partial(_attn_sample_kernel, lam_init=lam_init),
        grid=(b,),
        in_specs=[new(), new(), new(), whole(kpt3), whole(vp4), lam_spec, lam_spec, lam_spec,
                  lam_spec, _const_spec(g_sub.shape)],
        out_specs=new(),
        out_shape=jax.ShapeDtypeStruct((b, t_new, width), BF16),
        compiler_params=_params(1), name="attn_sample",
    )(q3, kn3, vn3, kpt3, vp4, *lams, g_sub)


def _post_kernel(x_ref, o_ref, y_ref, wout_ref, gx_ref, wxq_ref, mk_ref, mv_ref, wxo_ref,
                 out_ref, xo_ref, *, heads):
    aw = o_ref.shape[-1]
    x1 = x_ref[...] + _dot(o_ref[...], wout_ref[:aw, :]) + _dot(y_ref[...], wout_ref[aw:, :])
    hd = x1.shape[-1] // heads
    hq = (_dot(_rms(x1, gx_ref[...]).astype(BF16), wxq_ref[...]) * (hd ** -0.5)).astype(BF16)
    for h in range(heads):
        cols = slice(h * hd, (h + 1) * hd)
        s = _dot_nt(hq[:, cols], mk_ref[0, :, cols].astype(BF16))
        e = jnp.exp(s - jnp.max(s, axis=-1, keepdims=True))
        l = jnp.sum(e, axis=-1, keepdims=True)
        xo = _dot(e.astype(BF16), mv_ref[0, :, cols].astype(BF16)) / l
        xo_ref[:, cols] = xo.astype(BF16)
    out_ref[...] = x1 + _dot(xo_ref[...], wxo_ref[...])


def _post(x2d, o2d, y2d, w_out, g_x, w_xq, mk3, mv3, w_xo, *, batch, seq_len, tm, heads):
    n, d = x2d.shape
    aw, cw = o2d.shape[-1], y2d.shape[-1]
    nj = seq_len // tm
    row = lambda b, j: (b * nj + j, 0)
    mem = pl.BlockSpec((1,) + mk3.shape[1:], lambda b, j: (b, 0, 0))
    return pl.pallas_call(
        functools.partial(_post_kernel, heads=heads),
        grid=(batch, nj),
        in_specs=[pl.BlockSpec((tm, d), row), pl.BlockSpec((tm, aw), row),
                  pl.BlockSpec((tm, cw), row), _const_spec(w_out.shape), _const_spec((1, d)),
                  _const_spec(w_xq.shape), mem, mem, _const_spec(w_xo.shape)],
        out_specs=pl.BlockSpec((tm, d), row),
        out_shape=jax.ShapeDtypeStruct((n, d), F32),
        scratch_shapes=[pltpu.VMEM((tm, d), BF16)],
        compiler_params=_params(2), name="post",
    )(x2d, o2d, y2d, w_out, g_x, w_xq, mk3, mv3, w_xo)


FF_CHUNK = 256


def _ffn_body(x_ref, g_ref, wup_ref, wgate_ref, wconv_ref, wdown_ref, gfin_ref, y_ref,
              a_ref, fix, emit_up, final_norm):
    x = x_ref[...]
    hf = _rms(x, g_ref[...]).astype(BF16)
    dff = wup_ref.shape[-1]
    for c in range(dff // FF_CHUNK):
        cols = slice(c * FF_CHUNK, (c + 1) * FF_CHUNK)
        up = _dot(hf, wup_ref[:, cols])
        emit_up(cols, up)
        uc = _conv3(up, wconv_ref[:, cols], fix(cols, up.shape))
        gate = _dot(hf, wgate_ref[:, cols])
        a_ref[:, cols] = (uc * jax.nn.sigmoid(uc) * gate).astype(BF16)
    x3 = x + _dot(a_ref[...], wdown_ref[...])
    y_ref[...] = _rms(x3, gfin_ref[...]) if final_norm else x3


def _ffn_carry_kernel(x_ref, g_ref, wup_ref, wgate_ref, wconv_ref, wdown_ref, gfin_ref,
                      y_ref, tail_ref, a_ref, carry_ref, new_ref, *, final_norm):
    @pl.when(pl.program_id(1) == 0)
    def _():
        carry_ref[...] = jnp.zeros_like(carry_ref)

    def emit_up(cols, up):
        new_ref[:, cols] = up[up.shape[0] - SUBLANES:, :]

    fix = lambda cols, shape: _carry_fix(carry_ref, cols, shape)
    _ffn_body(x_ref, g_ref, wup_ref, wgate_ref, wconv_ref, wdown_ref, gfin_ref, y_ref,
              a_ref, fix, emit_up, final_norm)
    carry_ref[...] = new_ref[...]
    tail_ref[0] = new_ref[...]


def _ffn_state_kernel(x_ref, g_ref, wup_ref, wgate_ref, wconv_ref, wdown_ref, gfin_ref,
                      sp1_ref, sp2_ref, y_ref, up_ref, a_ref, *, seq_len, final_norm):
    def emit_up(cols, up):
        up_ref[:, cols] = up

    fix = lambda cols, shape: _state_fix(sp1_ref[:, cols], sp2_ref[:, cols], seq_len, shape)
    _ffn_body(x_ref, g_ref, wup_ref, wgate_ref, wconv_ref, wdown_ref, gfin_ref, y_ref,
              a_ref, fix, emit_up, final_norm)


def _ffn(x2d, g_ffn, w_up, w_gate, w_conv, w_down, g_final, *, batch, seq_len, tm, final_norm,
         state=None):
    n, d = x2d.shape
    dff = w_up.shape[-1]
    weights = [_const_spec((1, d)), _const_spec(w_up.shape), _const_spec(w_gate.shape),
               _const_spec(w_conv.shape), _const_spec(w_down.shape), _const_spec((1, d))]
    if state is None:
        nj = seq_len // tm
        row = lambda b, j: (b * nj + j, 0)
        return pl.pallas_call(
            functools.partial(_ffn_carry_kernel, final_norm=final_norm),
            grid=(batch, nj),
            in_specs=[pl.BlockSpec((tm, d), row)] + weights,
            out_specs=[pl.BlockSpec((tm, d), row),
                       pl.BlockSpec((1, SUBLANES, dff), lambda b, j: (b, 0, 0))],
            out_shape=[jax.ShapeDtypeStruct((n, d), F32),
                       jax.ShapeDtypeStruct((batch, SUBLANES, dff), F32)],
            scratch_shapes=[pltpu.VMEM((tm, dff), BF16), pltpu.VMEM((SUBLANES, dff), F32),
                            pltpu.VMEM((SUBLANES, dff), F32)],
            compiler_params=_params(2), name="ffn_prompt",
        )(x2d, g_ffn, w_up, w_gate, w_conv, w_down, g_final)
    sp1, sp2 = state
    row = lambda i: (i, 0)
    return pl.pallas_call(
        functools.partial(_ffn_state_kernel, seq_len=seq_len, final_norm=final_norm),
        grid=(n // tm,),
        in_specs=[pl.BlockSpec((tm, d), row)] + weights + [pl.BlockSpec((tm, dff), row)] * 2,
        out_specs=[pl.BlockSpec((tm, d), row), pl.BlockSpec((tm, dff), row)],
        out_shape=[jax.ShapeDtypeStruct((n, d), F32), jax.ShapeDtypeStruct((n, dff), F32)],
        scratch_shapes=[pltpu.VMEM((tm, dff), BF16)],
        compiler_params=_params(1), name="ffn_sample",
    )(x2d, g_ffn, w_up, w_gate, w_conv, w_down, g_final, sp1, sp2)


def _rope_tables(pos, qk_head_dim):
    rot = qk_head_dim // 4
    half = rot // 2
    inv = 1.0 / (ROPE_THETA ** (jnp.arange(half, dtype=F32) * 2.0 / rot))
    ang = pos.astype(F32)[:, None] * inv[None, :]
    cos, sin = jnp.cos(ang), jnp.sin(ang)
    t = pos.shape[0]
    zeros = lambda w: jnp.zeros((t, w), F32)
    c = jnp.concatenate([cos, cos, jnp.ones((t, qk_head_dim - rot), F32)], axis=1)
    a = jnp.concatenate([-sin, zeros(qk_head_dim - half)], axis=1)
    b = jnp.concatenate([zeros(half), sin, zeros(qk_head_dim - rot)], axis=1)
    rep = LANES // qk_head_dim
    return tuple(jnp.tile(m, (1, rep)) for m in (c, a, b))


def _expand_state(state, seq_len):
    b, _, c = state.shape
    z = jnp.zeros((b, seq_len, c), state.dtype)
    sp1 = z.at[:, 0].set(state[:, 1])
    sp2 = z.at[:, 0].set(state[:, 0]).at[:, 1].set(state[:, 1])
    return sp1.reshape(b * seq_len, c), sp2.reshape(b * seq_len, c)


def kernel(x_prompt, x_sample, cache_attn_k, cache_attn_v, state_short_conv, state_ffn_conv,
           cache_mem_k, cache_mem_v, mem_prompt, g_mix, w_in, lam_q1, lam_k1, lam_q2, lam_k2,
           g_sub, w_sc, w_out, g_mem, g_x, w_xq, w_xk, w_xv, w_xo, g_ffn, w_up, w_gate,
           w_ffconv, w_down, g_final):
    depth = w_in.shape[0]
    bp, seq, d = x_prompt.shape
    bs, t_new, _ = x_sample.shape
    past = cache_attn_k.shape[2]
    heads_a, qk_dim = cache_attn_k.shape[3], cache_attn_k.shape[5]
    v_dim = cache_attn_v.shape[4]
    cw = state_short_conv.shape[-1]
    dff = state_ffn_conv.shape[-1]
    n_mem, heads_x, x_dim = cache_mem_k.shape[2:]
    qc = kc = heads_a * 2 * qk_dim
    vc = heads_a * v_dim
    dims = (qc, kc, vc, cw)
    assert 2 * qk_dim == LANES and v_dim == LANES and qk_dim == CHUNK

    tabs_p = _rope_tables(jnp.arange(seq, dtype=jnp.int32), qk_dim)
    tabs_s = tuple(jnp.tile(m, (bs, 1)) for m in
                   _rope_tables(past + jnp.arange(t_new, dtype=jnp.int32), qk_dim))
    row = lambda v: v.reshape(1, -1)

    hp = x_prompt.reshape(bp * seq, d)
    hs = x_sample.reshape(bs * t_new, d)
    outs_p = [[] for _ in range(6)]
    outs_s = [[] for _ in range(4)]
    for l in range(depth):
        lam_init = _lambda_init(l)
        wi, wo = w_in[l].astype(BF16), w_out[l].astype(BF16)
        wq, wxo_b = w_xq[l].astype(BF16), w_xo[l].astype(BF16)
        wkv = jnp.concatenate([w_xk[l], w_xv[l]], axis=1).astype(BF16)
        wu, wg, wd = w_up[l].astype(BF16), w_gate[l].astype(BF16), w_down[l].astype(BF16)
        lams = tuple(row(v[l]) for v in (lam_q1, lam_k1, lam_q2, lam_k2))
        gsub = row(g_sub[l])

        mk, mv = _mem_kv(mem_prompt.reshape(bp * n_mem, d), row(g_mem[l]), wkv, 512)
        q, kt, v4, vb, ysc, sc_tail = _in_proj(hp, row(g_mix[l]), wi, tabs_p, w_sc[l], dims,
                                               batch=bp, seq_len=seq, tm=512)
        o = _attn_prompt(q.reshape(bp, seq, qc), kt, vb.reshape(bp, seq, vc), lams, gsub,
                         tq=256, lam_init=lam_init)
        x2 = _post(hp, o.reshape(bp * seq, vc), ysc, wo, row(g_x[l]), wq,
                   mk.reshape(bp, n_mem, d), mv.reshape(bp, n_mem, d), wxo_b,
                   batch=bp, seq_len=seq, tm=512, heads=heads_x)
        hp, ff_tail = _ffn(x2, row(g_ffn[l]), wu, wg, w_ffconv[l], wd, row(g_final),
                           batch=bp, seq_len=seq, tm=512, final_norm=l == depth - 1)
        outs_p[0].append(jnp.transpose(kt.reshape(bp, heads_a, 2, qk_dim, seq), (0, 4, 1, 2, 3)))
        outs_p[1].append(v4.reshape(bp, seq, heads_a, v_dim))
        outs_p[2].append(sc_tail[:, SUBLANES - 2:])
        outs_p[3].append(ff_tail[:, SUBLANES - 2:])
        outs_p[4].append(mk.reshape(bp, n_mem, heads_x, x_dim))
        outs_p[5].append(mv.reshape(bp, n_mem, heads_x, x_dim))

        n_s = bs * t_new
        qs, ks, vs, yscs, u_s = _in_proj(
            hs, row(g_mix[l]), wi, tabs_s, w_sc[l], dims, batch=bs, seq_len=t_new, tm=n_s,
            state=_expand_state(state_short_conv[l], t_new))
        o_s = _attn_sample(qs.reshape(bs, t_new, qc), ks.reshape(bs, t_new, kc),
                           vs.reshape(bs, t_new, vc),
                           jnp.transpose(cache_attn_k[l].reshape(bs, past, kc), (0, 2, 1)),
                           cache_attn_v[l].reshape(bs, past * heads_a, v_dim), lams, gsub,
                           lam_init=lam_init)
        x2s = _post(hs, o_s.reshape(n_s, vc), yscs, wo, row(g_x[l]), wq,
                    cache_mem_k[l].reshape(bs, n_mem, d), cache_mem_v[l].reshape(bs, n_mem, d),
                    wxo_b, batch=bs, seq_len=t_new, tm=t_new, heads=heads_x)
        hs, up_s = _ffn(x2s, row(g_ffn[l]), wu, wg, w_ffconv[l], wd, row(g_final),
                        batch=bs, seq_len=t_new, tm=n_s, final_norm=l == depth - 1,
                        state=_expand_state(state_ffn_conv[l], t_new))
        outs_s[0].append(ks.reshape(bs, t_new, heads_a, 2, qk_dim))
        outs_s[1].append(vs.reshape(bs, t_new, heads_a, v_dim))
        outs_s[2].append(u_s.reshape(bs, t_new, cw)[:, t_new - 2:])
        outs_s[3].append(up_s.reshape(bs, t_new, dff)[:, t_new - 2:])

    return (hp.reshape(bp, seq, d), hs.reshape(bs, t_new, d),
            *(jnp.stack(o) for o in outs_p), *(jnp.stack(o) for o in outs_s))
```

```python
import functools
import math

import jax
import jax.numpy as jnp
from jax import lax
from jax.experimental import pallas as pl
from jax.experimental.pallas import tpu as pltpu

EPS = 1e-6
CHUNK = 64
ROPE_THETA = 500000.0
LANES = 128
SUBLANES = 8
VMEM_LIMIT = 56 * 1024 * 1024
BF16 = jnp.bfloat16
F32 = jnp.float32
NEG_INF = float("-inf")


def _lambda_init(layer_idx):
    return 0.8 - 0.6 * math.exp(-0.3 * layer_idx)


def _rms(x, g):
    return x * lax.rsqrt(jnp.mean(x * x, axis=-1, keepdims=True) + EPS) * g


def _dot(a, b):
    return jnp.dot(a, b, preferred_element_type=F32)


def _dot_nt(a, b):
    return lax.dot_general(a, b, (((1,), (1,)), ((), ())), preferred_element_type=F32)


def _params(n_grid):
    return pltpu.CompilerParams(dimension_semantics=("arbitrary",) * n_grid,
                                vmem_limit_bytes=VMEM_LIMIT)


def _const_spec(shape):
    nd = len(shape)
    return pl.BlockSpec(shape, lambda *_: (0,) * nd, pipeline_mode=pl.Buffered(1))


def _conv3(u, w, fix):
    p1 = pltpu.roll(u, 1, axis=0)
    p2 = pltpu.roll(u, 2, axis=0)
    p1, p2 = fix(p1, p2)
    return w[0:1, :] * p2 + w[1:2, :] * p1 + w[2:3, :] * u


def _carry_fix(carry_ref, cols, shape):
    row = lax.broadcasted_iota(jnp.int32, shape, 0)
    c0 = carry_ref[SUBLANES - 2:SUBLANES - 1, cols]
    c1 = carry_ref[SUBLANES - 1:SUBLANES, cols]

    def fix(p1, p2):
        p1 = jnp.where(row == 0, c1, p1)
        p2 = jnp.where(row == 0, c0, jnp.where(row == 1, c1, p2))
        return p1, p2
    return fix


def _state_fix(sp1, sp2, seq_len, shape):
    row = lax.broadcasted_iota(jnp.int32, shape, 0)
    t = lax.rem(row, seq_len)

    def fix(p1, p2):
        return jnp.where(t == 0, sp1, p1), jnp.where(t < 2, sp2, p2)
    return fix


def _store_mem_layout(ref, blk, heads):
    rows, width = blk.shape
    lt = width // heads // LANES
    for h in range(heads):
        for t in range(lt):
            c0 = (h * lt + t) * LANES
            ref[pl.ds(t * heads + h, rows, stride=heads * lt), :] = blk[:, c0:c0 + LANES]


def _load_mem_head(ref, h, heads, hd):
    if ref.shape[-1] != LANES:
        return ref[0, :, h * hd:(h + 1) * hd]
    lt = hd // LANES
    rows = ref.shape[1] // (heads * lt)
    parts = [ref[0, pl.ds(t * heads + h, rows, stride=heads * lt), :] for t in range(lt)]
    return jnp.concatenate(parts, axis=1).astype(BF16)


def _to_mem_layout(x, lt):
    b, n, heads, hd = x.shape
    x = jnp.transpose(x.reshape(b, n, heads, lt, LANES), (0, 1, 3, 2, 4))
    return x.reshape(b, n * heads * lt, LANES)


def _from_mem_layout(x8, b, n, heads, hd):
    lt = hd // LANES
    x = jnp.transpose(x8.reshape(b, n, lt, heads, LANES), (0, 1, 3, 2, 4))
    return x.reshape(b, n, heads, hd)


def _memkv_kernel(m_ref, g_ref, w_ref, k8_ref, v8_ref, kb_ref, vb_ref, *, heads):
    d = kb_ref.shape[-1]
    h = _rms(m_ref[...], g_ref[...]).astype(BF16)
    kv = _dot(h, w_ref[...])
    for part, (o8, ob) in enumerate(((k8_ref, kb_ref), (v8_ref, vb_ref))):
        blk = kv[:, part * d:(part + 1) * d]
        ob[...] = blk.astype(BF16)
        _store_mem_layout(o8, blk, heads)


def _mem_kv(mem2d, g_mem, w_xkv, tm, heads):
    n, d = mem2d.shape
    group = d // LANES
    row = lambda i: (i, 0)
    return pl.pallas_call(
        functools.partial(_memkv_kernel, heads=heads),
        grid=(n // tm,),
        in_specs=[pl.BlockSpec((tm, d), row), _const_spec((1, d)), _const_spec(w_xkv.shape)],
        out_specs=[pl.BlockSpec((tm * group, LANES), row)] * 2 + [pl.BlockSpec((tm, d), row)] * 2,
        out_shape=[jax.ShapeDtypeStruct((n * group, LANES), F32)] * 2
        + [jax.ShapeDtypeStruct((n, d), BF16)] * 2,
        compiler_params=_params(1),
        name="mem_kv",
    )(mem2d, g_mem, w_xkv)


def _inproj_body(x_ref, g_ref, w_ref, cos_ref, sa_ref, sb_ref, wsc_ref,
                 q_ref, put_k, put_v, y_ref, fix, dims):
    qc, kc, vc, cw = dims
    h = _rms(x_ref[...], g_ref[...]).astype(BF16)
    cos, sa, sb = cos_ref[...], sa_ref[...], sb_ref[...]

    def rope(t):
        return (t * cos + pltpu.roll(t, LANES - 8, axis=1) * sa
                + pltpu.roll(t, 8, axis=1) * sb)

    qk = _dot(h, w_ref[:, :qc + kc])
    for c in range(qc // LANES):
        blk = rope(qk[:, c * LANES:(c + 1) * LANES])
        q_ref[:, c * LANES:(c + 1) * LANES] = (blk * 0.125).astype(BF16)
    for c in range(kc // LANES):
        lo = qc + c * LANES
        put_k(c, rope(qk[:, lo:lo + LANES]))
    o = qc + kc
    vb = _dot(h, w_ref[:, o:o + vc + cw])
    put_v(vb[:, :vc])
    bg = vb[:, vc:]
    cx = _dot(h, w_ref[:, o + vc + cw:])
    u = cx[:, :cw] * cx[:, cw:]
    conv = _conv3(u, wsc_ref[...], fix(u.shape))
    y_ref[...] = (bg * conv).astype(BF16)
    return u


def _inproj_carry_kernel(x_ref, g_ref, w_ref, cos_ref, sa_ref, sb_ref, wsc_ref,
                         q_ref, kt_ref, v4_ref, vb_ref, y_ref, tail_ref, carry_ref, *, dims):
    @pl.when(pl.program_id(1) == 0)
    def _():
        carry_ref[...] = jnp.zeros_like(carry_ref)

    def put_k(c, blk):
        kt_ref[0, c * LANES:(c + 1) * LANES, :] = blk.T

    def put_v(v):
        vb_ref[...] = v.astype(BF16)
        heads = v.shape[1] // LANES
        for hd in range(heads):
            v4_ref[pl.ds(hd, v.shape[0], stride=heads), :] = v[:, hd * LANES:(hd + 1) * LANES]

    fix = lambda shape: _carry_fix(carry_ref, slice(None), shape)
    u = _inproj_body(x_ref, g_ref, w_ref, cos_ref, sa_ref, sb_ref, wsc_ref,
                     q_ref, put_k, put_v, y_ref, fix, dims)
    last = u[u.shape[0] - SUBLANES:, :]
    carry_ref[...] = last
    tail_ref[0] = last


def _inproj_state_kernel(x_ref, g_ref, w_ref, cos_ref, sa_ref, sb_ref, wsc_ref,
                         sp1_ref, sp2_ref, q_ref, k_ref, v_ref, y_ref, u_ref,
                         *, dims, seq_len):
    def put_k(c, blk):
        k_ref[:, c * LANES:(c + 1) * LANES] = blk

    def put_v(v):
        v_ref[...] = v

    fix = lambda shape: _state_fix(sp1_ref[...], sp2_ref[...], seq_len, shape)
    u_ref[...] = _inproj_body(x_ref, g_ref, w_ref, cos_ref, sa_ref, sb_ref, wsc_ref,
                              q_ref, put_k, put_v, y_ref, fix, dims)


def _in_proj(x2d, g, w, tabs, w_sc, dims, *, batch, seq_len, tm, state=None):
    n, d = x2d.shape
    qc, kc, vc, cw = dims
    common_in = [None, _const_spec((1, d)), _const_spec(w.shape), None, None, None,
                 _const_spec(w_sc.shape)]
    if state is None:
        nj = seq_len // tm
        heads = vc // LANES
        row = lambda b, j: (b * nj + j, 0)
        tab = pl.BlockSpec((tm, LANES), lambda b, j: (j, 0))
        in_specs = list(common_in)
        in_specs[0] = pl.BlockSpec((tm, d), row)
        in_specs[3:6] = [tab, tab, tab]
        out_specs = [pl.BlockSpec((tm, qc), row),
                     pl.BlockSpec((1, kc, tm), lambda b, j: (b, 0, j)),
                     pl.BlockSpec((tm * heads, LANES), row),
                     pl.BlockSpec((tm, vc), row), pl.BlockSpec((tm, cw), row),
                     pl.BlockSpec((1, SUBLANES, cw), lambda b, j: (b, 0, 0))]
        out_shape = [jax.ShapeDtypeStruct((n, qc), BF16),
                     jax.ShapeDtypeStruct((batch, kc, seq_len), F32),
                     jax.ShapeDtypeStruct((n * heads, LANES), F32),
                     jax.ShapeDtypeStruct((n, vc), BF16), jax.ShapeDtypeStruct((n, cw), BF16),
                     jax.ShapeDtypeStruct((batch, SUBLANES, cw), F32)]
        return pl.pallas_call(
            functools.partial(_inproj_carry_kernel, dims=dims),
            grid=(batch, nj), in_specs=in_specs, out_specs=out_specs, out_shape=out_shape,
            scratch_shapes=[pltpu.VMEM((SUBLANES, cw), F32)],
            compiler_params=_params(2), name="in_proj_prompt",
        )(x2d, g, w, *tabs, w_sc)
    sp1, sp2 = state
    row = lambda i: (i, 0)
    tab = pl.BlockSpec((tm, LANES), row)
    in_specs = list(common_in)
    in_specs[0] = pl.BlockSpec((tm, d), row)
    in_specs[3:6] = [tab, tab, tab]
    in_specs += [pl.BlockSpec((tm, cw), row)] * 2
    out_specs = [pl.BlockSpec((tm, c), row) for c in (qc, kc, vc, cw, cw)]
    out_shape = [jax.ShapeDtypeStruct((n, qc), BF16), jax.ShapeDtypeStruct((n, kc), F32),
                 jax.ShapeDtypeStruct((n, vc), F32), jax.ShapeDtypeStruct((n, cw), BF16),
                 jax.ShapeDtypeStruct((n, cw), F32)]
    return pl.pallas_call(
        functools.partial(_inproj_state_kernel, dims=dims, seq_len=seq_len),
        grid=(n // tm,), in_specs=in_specs, out_specs=out_specs, out_shape=out_shape,
        compiler_params=_params(1), name="in_proj_sample",
    )(x2d, g, w, *tabs, w_sc, sp1, sp2)


def _diff_lambda(lq1_ref, lk1_ref, lq2_ref, lk2_ref, lam_init):
    a = jnp.sum(lq1_ref[...] * lk1_ref[...], axis=-1, keepdims=True)
    b = jnp.sum(lq2_ref[...] * lk2_ref[...], axis=-1, keepdims=True)
    return jnp.exp(a) - jnp.exp(b) + lam_init


def _stack_maps(qt):
    lane = lax.broadcasted_iota(jnp.int32, qt.shape, 1)
    first = lane < (LANES // 2)
    zero = jnp.zeros_like(qt)
    return jnp.concatenate([jnp.where(first, qt, zero), jnp.where(first, zero, qt)], axis=0)


def _finish_head(acc, l, lam, g, lam_init, tq):
    o = acc[:tq] / l[:tq] - lam * (acc[tq:] / l[tq:])
    return _rms(o, g) * (1.0 - lam_init)


def _attn_prompt_kernel(q_ref, k_ref, v_ref, lq1_ref, lk1_ref, lq2_ref, lk2_ref, g_ref,
                        o_ref, kb_ref, *, tq, lam_init):
    seq = q_ref.shape[1]
    kb_ref[...] = k_ref[0].astype(BF16)
    lam = _diff_lambda(lq1_ref, lk1_ref, lq2_ref, lk2_ref, lam_init)
    g = g_ref[...]
    r = lax.broadcasted_iota(jnp.int32, (2 * tq, tq), 0)
    c = lax.broadcasted_iota(jnp.int32, (2 * tq, tq), 1)
    shift = CHUNK.bit_length() - 1
    diag_mask = ((r & (tq - 1)) >> shift) >= (c >> shift)

    nq = seq // tq
    order = list(range(0, nq, 2)) + list(range(nq - 1 - (nq % 2), 0, -2))
    for i in order:
        lo = i * tq
        qm = _stack_maps(q_ref[0, lo:lo + tq, :])
        s_d = jnp.where(diag_mask, _dot(qm, kb_ref[:, lo:lo + tq]), NEG_INF)
        m = jnp.max(s_d, axis=-1, keepdims=True)
        if i:
            s_o = _dot(qm, kb_ref[:, :lo])
            m = jnp.maximum(m, jnp.max(s_o, axis=-1, keepdims=True))
        e_d = jnp.exp(s_d - m)
        l = jnp.sum(e_d, axis=-1, keepdims=True)
        acc = _dot(e_d.astype(BF16), v_ref[0, lo:lo + tq, :])
        if i:
            e_o = jnp.exp(s_o - m)
            l = l + jnp.sum(e_o, axis=-1, keepdims=True)
            acc = acc + _dot(e_o.astype(BF16), v_ref[0, :lo, :])
        o_ref[0, lo:lo + tq, :] = _finish_head(acc, l, lam, g, lam_init, tq).astype(BF16)


def _attn_prompt(q3, kt3, v3, lams, g_sub, *, tq, lam_init):
    b, seq, width = q3.shape
    heads = width // LANES
    blk = lambda: pl.BlockSpec((1, seq, LANES), lambda i, h: (i, 0, h))
    lam_spec = _const_spec(lams[0].shape)
    return pl.pallas_call(
        functools.partial(_attn_prompt_kernel, tq=tq, lam_init=lam_init),
        grid=(b, heads),
        in_specs=[blk(), pl.BlockSpec((1, LANES, seq), lambda i, h: (i, h, 0)), blk(),
                  lam_spec, lam_spec, lam_spec, lam_spec, _const_spec(g_sub.shape)],
        out_specs=blk(),
        out_shape=jax.ShapeDtypeStruct((b, seq, width), BF16),
        scratch_shapes=[pltpu.VMEM((LANES, seq), BF16)],
        compiler_params=_params(2), name="attn_prompt",
    )(q3, kt3, v3, *lams, g_sub)


def _attn_sample_kernel(q_ref, kn_ref, vn_ref, kp_ref, vp_ref, lq1_ref, lk1_ref, lq2_ref,
                        lk2_ref, g_ref, o_ref, *, lam_init):
    t_new = q_ref.shape[1]
    heads = q_ref.shape[2] // LANES
    past = kp_ref.shape[2]
    lam = _diff_lambda(lq1_ref, lk1_ref, lq2_ref, lk2_ref, lam_init)
    g = g_ref[...]
    pad = jnp.zeros((LANES - t_new, LANES), BF16)
    lane = lax.broadcasted_iota(jnp.int32, (2 * t_new, LANES), 1)
    for h in range(heads):
        cols = slice(h * LANES, (h + 1) * LANES)
        qm = _stack_maps(q_ref[0, :, cols])
        kn = jnp.concatenate([kn_ref[0, :, cols].astype(BF16), pad], axis=0)
        vn = jnp.concatenate([vn_ref[0, :, cols].astype(BF16), pad], axis=0)
        s_p = _dot(qm, kp_ref[0, cols, :].astype(BF16))
        vp = vp_ref[0, pl.ds(h, past, stride=heads), :].astype(BF16)
        s_n = jnp.where(lane < t_new, _dot_nt(qm, kn), NEG_INF)
        m = jnp.maximum(jnp.max(s_p, axis=-1, keepdims=True),
                        jnp.max(s_n, axis=-1, keepdims=True))
        e_p = jnp.exp(s_p - m)
        e_n = jnp.exp(s_n - m)
        l = jnp.sum(e_p, axis=-1, keepdims=True) + jnp.sum(e_n, axis=-1, keepdims=True)
        acc = _dot(e_p.astype(BF16), vp) + _dot(e_n.astype(BF16), vn)
        o_ref[0, :, cols] = _finish_head(acc, l, lam, g, lam_init, t_new).astype(BF16)


def _attn_sample(q3, kn3, vn3, kpt3, vp4, lams, g_sub, *, lam_init):
    b, t_new, width = q3.shape
    new = lambda: pl.BlockSpec((1, t_new, width), lambda i: (i, 0, 0))
    whole = lambda a: pl.BlockSpec((1,) + a.shape[1:], lambda i: (i, 0, 0))
    lam_spec = _const_spec(lams[0].shape)
    return pl.pallas_call(
        functools.partial(_attn_sample_kernel, lam_init=lam_init),
        grid=(b,),
        in_specs=[new(), new(), new(), whole(kpt3), whole(vp4), lam_spec, lam_spec, lam_spec,
                  lam_spec, _const_spec(g_sub.shape)],
        out_specs=new(),
        out_shape=jax.ShapeDtypeStruct((b, t_new, width), BF16),
        compiler_params=_params(1), name="attn_sample",
    )(q3, kn3, vn3, kpt3, vp4, *lams, g_sub)


def _mix_and_query(x_ref, o_ref, y_ref, wout_ref, gx_ref, wxq_ref, heads):
    aw = o_ref.shape[-1]
    x1 = x_ref[...] + _dot(o_ref[...], wout_ref[:aw, :]) + _dot(y_ref[...], wout_ref[aw:, :])
    hd = x1.shape[-1] // heads
    hq = (_dot(_rms(x1, gx_ref[...]).astype(BF16), wxq_ref[...]) * (hd ** -0.5)).astype(BF16)
    return x1, hq


def _cross_attend(hq, mk_ref, mv_ref, heads, put):
    hd = hq.shape[-1] // heads
    for h in range(heads):
        cols = slice(h * hd, (h + 1) * hd)
        s = _dot_nt(hq[:, cols], _load_mem_head(mk_ref, h, heads, hd))
        e = jnp.exp(s - jnp.max(s, axis=-1, keepdims=True))
        l = jnp.sum(e, axis=-1, keepdims=True)
        xo = _dot(e.astype(BF16), _load_mem_head(mv_ref, h, heads, hd)) / l
        put(cols, xo.astype(BF16))


def _post_kernel(x_ref, o_ref, y_ref, wout_ref, gx_ref, wxq_ref, mk_ref, mv_ref, wxo_ref,
                 out_ref, xo_ref, *, heads):
    x1, hq = _mix_and_query(x_ref, o_ref, y_ref, wout_ref, gx_ref, wxq_ref, heads)

    def put(cols, xo):
        xo_ref[:, cols] = xo

    _cross_attend(hq, mk_ref, mv_ref, heads, put)
    out_ref[...] = x1 + _dot(xo_ref[...], wxo_ref[...])


def _post_rows_kernel(x_ref, o_ref, y_ref, wout_ref, gx_ref, wxq_ref, mk_ref, mv_ref, wxo_ref,
                      out_ref, x1_ref, hq_ref, xo_ref, *, heads, seq_len):
    b = pl.program_id(0)

    @pl.when(b == 0)
    def _():
        x1_ref[...], hq_ref[...] = _mix_and_query(x_ref, o_ref, y_ref, wout_ref, gx_ref,
                                                  wxq_ref, heads)

    rows = pl.ds(pl.multiple_of(b * seq_len, seq_len), seq_len)

    def put(cols, xo):
        xo_ref[rows, cols] = xo

    _cross_attend(hq_ref[rows, :], mk_ref, mv_ref, heads, put)

    @pl.when(b == pl.num_programs(0) - 1)
    def _():
        out_ref[...] = x1_ref[...] + _dot(xo_ref[...], wxo_ref[...])


def _post_rows(x2d, o2d, y2d, w_out, g_x, w_xq, mk3, mv3, w_xo, *, batch, seq_len, heads):
    n, d = x2d.shape
    mem = pl.BlockSpec((1,) + mk3.shape[1:], lambda b: (b, 0, 0))
    return pl.pallas_call(
        functools.partial(_post_rows_kernel, heads=heads, seq_len=seq_len),
        grid=(batch,),
        in_specs=[_const_spec(x2d.shape), _const_spec(o2d.shape), _const_spec(y2d.shape),
                  _const_spec(w_out.shape), _const_spec((1, d)), _const_spec(w_xq.shape),
                  mem, mem, _const_spec(w_xo.shape)],
        out_specs=pl.BlockSpec((n, d), lambda b: (0, 0)),
        out_shape=jax.ShapeDtypeStruct((n, d), F32),
        scratch_shapes=[pltpu.VMEM((n, d), F32), pltpu.VMEM((n, d), BF16),
                        pltpu.VMEM((n, d), BF16)],
        compiler_params=_params(1), name="post_sample",
    )(x2d, o2d, y2d, w_out, g_x, w_xq, mk3, mv3, w_xo)


def _post(x2d, o2d, y2d, w_out, g_x, w_xq, mk3, mv3, w_xo, *, batch, seq_len, tm, heads):
    n, d = x2d.shape
    aw, cw = o2d.shape[-1], y2d.shape[-1]
    nj = seq_len // tm
    row = lambda b, j: (b * nj + j, 0)
    mem = pl.BlockSpec((1,) + mk3.shape[1:], lambda b, j: (b, 0, 0))
    return pl.pallas_call(
        functools.partial(_post_kernel, heads=heads),
        grid=(batch, nj),
        in_specs=[pl.BlockSpec((tm, d), row), pl.BlockSpec((tm, aw), row),
                  pl.BlockSpec((tm, cw), row), _const_spec(w_out.shape), _const_spec((1, d)),
                  _const_spec(w_xq.shape), mem, mem, _const_spec(w_xo.shape)],
        out_specs=pl.BlockSpec((tm, d), row),
        out_shape=jax.ShapeDtypeStruct((n, d), F32),
        scratch_shapes=[pltpu.VMEM((tm, d), BF16)],
        compiler_params=_params(2), name="post",
    )(x2d, o2d, y2d, w_out, g_x, w_xq, mk3, mv3, w_xo)


FF_CHUNK = 256


def _ffn_body(x_ref, g_ref, wup_ref, wgate_ref, wconv_ref, wdown_ref, gfin_ref, y_ref,
              a_ref, fix, emit_up, final_norm):
    x = x_ref[...]
    hf = _rms(x, g_ref[...]).astype(BF16)
    dff = wup_ref.shape[-1]
    for c in range(dff // FF_CHUNK):
        cols = slice(c * FF_CHUNK, (c + 1) * FF_CHUNK)
        up = _dot(hf, wup_ref[:, cols])
        emit_up(cols, up)
        uc = _conv3(up, wconv_ref[:, cols], fix(cols, up.shape))
        gate = _dot(hf, wgate_ref[:, cols])
        a_ref[:, cols] = (uc * jax.nn.sigmoid(uc) * gate).astype(BF16)
    x3 = x + _dot(a_ref[...], wdown_ref[...])
    y_ref[...] = _rms(x3, gfin_ref[...]) if final_norm else x3


def _ffn_carry_kernel(x_ref, g_ref, wup_ref, wgate_ref, wconv_ref, wdown_ref, gfin_ref,
                      y_ref, tail_ref, a_ref, carry_ref, new_ref, *, final_norm):
    @pl.when(pl.program_id(1) == 0)
    def _():
        carry_ref[...] = jnp.zeros_like(carry_ref)

    def emit_up(cols, up):
        new_ref[:, cols] = up[up.shape[0] - SUBLANES:, :]

    fix = lambda cols, shape: _carry_fix(carry_ref, cols, shape)
    _ffn_body(x_ref, g_ref, wup_ref, wgate_ref, wconv_ref, wdown_ref, gfin_ref, y_ref,
              a_ref, fix, emit_up, final_norm)
    carry_ref[...] = new_ref[...]
    tail_ref[0] = new_ref[...]


def _ffn_state_kernel(x_ref, g_ref, wup_ref, wgate_ref, wconv_ref, wdown_ref, gfin_ref,
                      sp1_ref, sp2_ref, y_ref, up_ref, a_ref, *, seq_len, final_norm):
    def emit_up(cols, up):
        up_ref[:, cols] = up

    fix = lambda cols, shape: _state_fix(sp1_ref[:, cols], sp2_ref[:, cols], seq_len, shape)
    _ffn_body(x_ref, g_ref, wup_ref, wgate_ref, wconv_ref, wdown_ref, gfin_ref, y_ref,
              a_ref, fix, emit_up, final_norm)


def _ffn(x2d, g_ffn, w_up, w_gate, w_conv, w_down, g_final, *, batch, seq_len, tm, final_norm,
         state=None):
    n, d = x2d.shape
    dff = w_up.shape[-1]
    weights = [_const_spec((1, d)), _const_spec(w_up.shape), _const_spec(w_gate.shape),
               _const_spec(w_conv.shape), _const_spec(w_down.shape), _const_spec((1, d))]
    if state is None:
        nj = seq_len // tm
        row = lambda b, j: (b * nj + j, 0)
        return pl.pallas_call(
            functools.partial(_ffn_carry_kernel, final_norm=final_norm),
            grid=(batch, nj),
            in_specs=[pl.BlockSpec((tm, d), row)] + weights,
            out_specs=[pl.BlockSpec((tm, d), row),
                       pl.BlockSpec((1, SUBLANES, dff), lambda b, j: (b, 0, 0))],
            out_shape=[jax.ShapeDtypeStruct((n, d), F32),
                       jax.ShapeDtypeStruct((batch, SUBLANES, dff), F32)],
            scratch_shapes=[pltpu.VMEM((tm, dff), BF16), pltpu.VMEM((SUBLANES, dff), F32),
                            pltpu.VMEM((SUBLANES, dff), F32)],
            compiler_params=_params(2), name="ffn_prompt",
        )(x2d, g_ffn, w_up, w_gate, w_conv, w_down, g_final)
    sp1, sp2 = state
    row = lambda i: (i, 0)
    return pl.pallas_call(
        functools.partial(_ffn_state_kernel, seq_len=seq_len, final_norm=final_norm),
        grid=(n // tm,),
        in_specs=[pl.BlockSpec((tm, d), row)] + weights + [pl.BlockSpec((tm, dff), row)] * 2,
        out_specs=[pl.BlockSpec((tm, d), row), pl.BlockSpec((tm, dff), row)],
        out_shape=[jax.ShapeDtypeStruct((n, d), F32), jax.ShapeDtypeStruct((n, dff), F32)],
        scratch_shapes=[pltpu.VMEM((tm, dff), BF16)],
        compiler_params=_params(1), name="ffn_sample",
    )(x2d, g_ffn, w_up, w_gate, w_conv, w_down, g_final, sp1, sp2)


def _rope_tables(pos, qk_head_dim):
    rot = qk_head_dim // 4
    half = rot // 2
    inv = 1.0 / (ROPE_THETA ** (jnp.arange(half, dtype=F32) * 2.0 / rot))
    ang = pos.astype(F32)[:, None] * inv[None, :]
    cos, sin = jnp.cos(ang), jnp.sin(ang)
    t = pos.shape[0]
    zeros = lambda w: jnp.zeros((t, w), F32)
    c = jnp.concatenate([cos, cos, jnp.ones((t, qk_head_dim - rot), F32)], axis=1)
    a = jnp.concatenate([-sin, zeros(qk_head_dim - half)], axis=1)
    b = jnp.concatenate([zeros(half), sin, zeros(qk_head_dim - rot)], axis=1)
    rep = LANES // qk_head_dim
    return tuple(jnp.tile(m, (1, rep)) for m in (c, a, b))


def _expand_state(state, seq_len):
    b, _, c = state.shape
    z = jnp.zeros((b, seq_len, c), state.dtype)
    sp1 = z.at[:, 0].set(state[:, 1])
    sp2 = z.at[:, 0].set(state[:, 0]).at[:, 1].set(state[:, 1])
    return sp1.reshape(b * seq_len, c), sp2.reshape(b * seq_len, c)


def kernel(x_prompt, x_sample, cache_attn_k, cache_attn_v, state_short_conv, state_ffn_conv,
           cache_mem_k, cache_mem_v, mem_prompt, g_mix, w_in, lam_q1, lam_k1, lam_q2, lam_k2,
           g_sub, w_sc, w_out, g_mem, g_x, w_xq, w_xk, w_xv, w_xo, g_ffn, w_up, w_gate,
           w_ffconv, w_down, g_final):
    depth = w_in.shape[0]
    bp, seq, d = x_prompt.shape
    bs, t_new, _ = x_sample.shape
    past = cache_attn_k.shape[2]
    heads_a, qk_dim = cache_attn_k.shape[3], cache_attn_k.shape[5]
    v_dim = cache_attn_v.shape[4]
    cw = state_short_conv.shape[-1]
    dff = state_ffn_conv.shape[-1]
    n_mem, heads_x, x_dim = cache_mem_k.shape[2:]
    qc = kc = heads_a * 2 * qk_dim
    vc = heads_a * v_dim
    dims = (qc, kc, vc, cw)
    assert 2 * qk_dim == LANES and v_dim == LANES and qk_dim == CHUNK

    tabs_p = _rope_tables(jnp.arange(seq, dtype=jnp.int32), qk_dim)
    tabs_s = tuple(jnp.tile(m, (bs, 1)) for m in
                   _rope_tables(past + jnp.arange(t_new, dtype=jnp.int32), qk_dim))
    row = lambda v: v.reshape(1, -1)

    hp = x_prompt.reshape(bp * seq, d)
    hs = x_sample.reshape(bs * t_new, d)
    outs_p = [[] for _ in range(6)]
    outs_s = [[] for _ in range(4)]
    for l in range(depth):
        lam_init = _lambda_init(l)
        wi, wo = w_in[l].astype(BF16), w_out[l].astype(BF16)
        wq, wxo_b = w_xq[l].astype(BF16), w_xo[l].astype(BF16)
        wkv = jnp.concatenate([w_xk[l], w_xv[l]], axis=1).astype(BF16)
        wu, wg, wd = w_up[l].astype(BF16), w_gate[l].astype(BF16), w_down[l].astype(BF16)
        lams = tuple(row(v[l]) for v in (lam_q1, lam_k1, lam_q2, lam_k2))
        gsub = row(g_sub[l])

        mk8, mv8, mk, mv = _mem_kv(mem_prompt.reshape(bp * n_mem, d), row(g_mem[l]), wkv, 512,
                                   heads_x)
        q, kt, v4, vb, ysc, sc_tail = _in_proj(hp, row(g_mix[l]), wi, tabs_p, w_sc[l], dims,
                                               batch=bp, seq_len=seq, tm=512)
        o = _attn_prompt(q.reshape(bp, seq, qc), kt, vb.reshape(bp, seq, vc), lams, gsub,
                         tq=256, lam_init=lam_init)
        x2 = _post(hp, o.reshape(bp * seq, vc), ysc, wo, row(g_x[l]), wq,
                   mk.reshape(bp, n_mem, d), mv.reshape(bp, n_mem, d), wxo_b,
                   batch=bp, seq_len=seq, tm=512, heads=heads_x)
        hp, ff_tail = _ffn(x2, row(g_ffn[l]), wu, wg, w_ffconv[l], wd, row(g_final),
                           batch=bp, seq_len=seq, tm=512, final_norm=l == depth - 1)
        outs_p[0].append(jnp.transpose(kt.reshape(bp, heads_a, 2, qk_dim, seq), (0, 4, 1, 2, 3)))
        outs_p[1].append(v4.reshape(bp, seq, heads_a, v_dim))
        outs_p[2].append(sc_tail[:, SUBLANES - 2:])
        outs_p[3].append(ff_tail[:, SUBLANES - 2:])
        outs_p[4].append(_from_mem_layout(mk8, bp, n_mem, heads_x, x_dim))
        outs_p[5].append(_from_mem_layout(mv8, bp, n_mem, heads_x, x_dim))

        n_s = bs * t_new
        qs, ks, vs, yscs, u_s = _in_proj(
            hs, row(g_mix[l]), wi, tabs_s, w_sc[l], dims, batch=bs, seq_len=t_new, tm=n_s,
            state=_expand_state(state_short_conv[l], t_new))
        o_s = _attn_sample(qs.reshape(bs, t_new, qc), ks.reshape(bs, t_new, kc),
                           vs.reshape(bs, t_new, vc),
                           jnp.transpose(cache_attn_k[l].reshape(bs, past, kc), (0, 2, 1)),
                           cache_attn_v[l].reshape(bs, past * heads_a, v_dim), lams, gsub,
                           lam_init=lam_init)
        x2s = _post_rows(hs, o_s.reshape(n_s, vc), yscs, wo, row(g_x[l]), wq,
                         _to_mem_layout(cache_mem_k[l], x_dim // LANES),
                         _to_mem_layout(cache_mem_v[l], x_dim // LANES), wxo_b,
                         batch=bs, seq_len=t_new, heads=heads_x)
        hs, up_s = _ffn(x2s, row(g_ffn[l]), wu, wg, w_ffconv[l], wd, row(g_final),
                        batch=bs, seq_len=t_new, tm=n_s, final_norm=l == depth - 1,
                        state=_expand_state(state_ffn_conv[l], t_new))
        outs_s[0].append(ks.reshape(bs, t_new, heads_a, 2, qk_dim))
        outs_s[1].append(vs.reshape(bs, t_new, heads_a, v_dim))
        outs_s[2].append(u_s.reshape(bs, t_new, cw)[:, t_new - 2:])
        outs_s[3].append(up_s.reshape(bs, t_new, dff)[:, t_new - 2:])

    return (hp.reshape(bp, seq, d), hs.reshape(bs, t_new, d),
            *(jnp.stack(o) for o in outs_p), *(jnp.stack(o) for o in outs_s))
```

```python
import functools
import math

import jax
import jax.numpy as jnp
from jax import lax
from jax.experimental import pallas as pl
from jax.experimental.pallas import tpu as pltpu

EPS = 1e-6
CHUNK = 64
ROPE_THETA = 500000.0
LANES = 128
SUBLANES = 8
VMEM_LIMIT = 56 * 1024 * 1024
MEM_TILE = 512
PROJ_TILE = 1024
POST_TILE = 1024
FFN_TILE = 1024
Q_TILE = 256
BF16 = jnp.bfloat16
F32 = jnp.float32
NEG_INF = float("-inf")


def _lambda_init(layer_idx):
    return 0.8 - 0.6 * math.exp(-0.3 * layer_idx)


def _rms(x, g):
    return x * lax.rsqrt(jnp.mean(x * x, axis=-1, keepdims=True) + EPS) * g


def _dot(a, b):
    return jnp.dot(a, b, preferred_element_type=F32)


def _dot_nt(a, b):
    return lax.dot_general(a, b, (((1,), (1,)), ((), ())), preferred_element_type=F32)


def _params(n_grid):
    return pltpu.CompilerParams(dimension_semantics=("arbitrary",) * n_grid,
                                vmem_limit_bytes=VMEM_LIMIT)


def _const_spec(shape):
    nd = len(shape)
    return pl.BlockSpec(shape, lambda *_: (0,) * nd, pipeline_mode=pl.Buffered(1))


def _conv3(u, w, fix):
    p1 = pltpu.roll(u, 1, axis=0)
    p2 = pltpu.roll(u, 2, axis=0)
    p1, p2 = fix(p1, p2)
    return w[0:1, :] * p2 + w[1:2, :] * p1 + w[2:3, :] * u


def _carry_fix(carry_ref, cols, shape):
    row = lax.broadcasted_iota(jnp.int32, shape, 0)
    c0 = carry_ref[SUBLANES - 2:SUBLANES - 1, cols]
    c1 = carry_ref[SUBLANES - 1:SUBLANES, cols]

    def fix(p1, p2):
        p1 = jnp.where(row == 0, c1, p1)
        p2 = jnp.where(row == 0, c0, jnp.where(row == 1, c1, p2))
        return p1, p2
    return fix


def _state_fix(sp1, sp2, seq_len, shape):
    row = lax.broadcasted_iota(jnp.int32, shape, 0)
    t = lax.rem(row, seq_len)

    def fix(p1, p2):
        return jnp.where(t == 0, sp1, p1), jnp.where(t < 2, sp2, p2)
    return fix


def _store_mem_layout(ref, blk, heads):
    rows, width = blk.shape
    lt = width // heads // LANES
    for h in range(heads):
        for t in range(lt):
            c0 = (h * lt + t) * LANES
            ref[pl.ds(t * heads + h, rows, stride=heads * lt), :] = blk[:, c0:c0 + LANES]


def _load_mem_head(ref, h, heads, hd):
    if ref.shape[-1] != LANES:
        return ref[0, :, h * hd:(h + 1) * hd]
    lt = hd // LANES
    rows = ref.shape[1] // (heads * lt)
    parts = [ref[0, pl.ds(t * heads + h, rows, stride=heads * lt), :] for t in range(lt)]
    return jnp.concatenate(parts, axis=1).astype(BF16)


def _to_mem_layout(x, lt):
    b, n, heads, hd = x.shape
    x = jnp.transpose(x.reshape(b, n, heads, lt, LANES), (0, 1, 3, 2, 4))
    return x.reshape(b, n * heads * lt, LANES)


def _from_mem_layout(x8, b, n, heads, hd):
    lt = hd // LANES
    x = jnp.transpose(x8.reshape(b, n, lt, heads, LANES), (0, 1, 3, 2, 4))
    return x.reshape(b, n, heads, hd)


def _memkv_kernel(m_ref, g_ref, w_ref, k8_ref, v8_ref, kb_ref, vb_ref, *, heads):
    d = kb_ref.shape[-1]
    h = _rms(m_ref[...], g_ref[...]).astype(BF16)
    kv = _dot(h, w_ref[...])
    for part, (o8, ob) in enumerate(((k8_ref, kb_ref), (v8_ref, vb_ref))):
        blk = kv[:, part * d:(part + 1) * d]
        ob[...] = blk.astype(BF16)
        _store_mem_layout(o8, blk, heads)


def _mem_kv(mem2d, g_mem, w_xkv, tm, heads):
    n, d = mem2d.shape
    group = d // LANES
    row = lambda i: (i, 0)
    return pl.pallas_call(
        functools.partial(_memkv_kernel, heads=heads),
        grid=(n // tm,),
        in_specs=[pl.BlockSpec((tm, d), row), _const_spec((1, d)), _const_spec(w_xkv.shape)],
        out_specs=[pl.BlockSpec((tm * group, LANES), row)] * 2 + [pl.BlockSpec((tm, d), row)] * 2,
        out_shape=[jax.ShapeDtypeStruct((n * group, LANES), F32)] * 2
        + [jax.ShapeDtypeStruct((n, d), BF16)] * 2,
        compiler_params=_params(1),
        name="mem_kv",
    )(mem2d, g_mem, w_xkv)


def _inproj_body(x_ref, g_ref, w_ref, cos_ref, sa_ref, sb_ref, wsc_ref,
                 q_ref, put_k, put_v, y_ref, fix, dims):
    qc, kc, vc, cw = dims
    h = _rms(x_ref[...], g_ref[...]).astype(BF16)
    cos, sa, sb = cos_ref[...], sa_ref[...], sb_ref[...]

    def rope(t):
        return (t * cos + pltpu.roll(t, LANES - 8, axis=1) * sa
                + pltpu.roll(t, 8, axis=1) * sb)

    qk = _dot(h, w_ref[:, :qc + kc])
    for c in range(qc // LANES):
        blk = rope(qk[:, c * LANES:(c + 1) * LANES])
        q_ref[:, c * LANES:(c + 1) * LANES] = (blk * 0.125).astype(BF16)
    for c in range(kc // LANES):
        lo = qc + c * LANES
        put_k(c, rope(qk[:, lo:lo + LANES]))
    o = qc + kc
    vb = _dot(h, w_ref[:, o:o + vc + cw])
    put_v(vb[:, :vc])
    bg = vb[:, vc:]
    cx = _dot(h, w_ref[:, o + vc + cw:])
    u = cx[:, :cw] * cx[:, cw:]
    conv = _conv3(u, wsc_ref[...], fix(u.shape))
    y_ref[...] = (bg * conv).astype(BF16)
    return u


def _inproj_carry_kernel(x_ref, g_ref, w_ref, cos_ref, sa_ref, sb_ref, wsc_ref,
                         q_ref, kt_ref, v4_ref, vb_ref, y_ref, tail_ref, carry_ref, *, dims):
    @pl.when(pl.program_id(1) == 0)
    def _():
        carry_ref[...] = jnp.zeros_like(carry_ref)

    def put_k(c, blk):
        kt_ref[0, c * LANES:(c + 1) * LANES, :] = blk.T

    def put_v(v):
        vb_ref[...] = v.astype(BF16)
        heads = v.shape[1] // LANES
        for hd in range(heads):
            v4_ref[pl.ds(hd, v.shape[0], stride=heads), :] = v[:, hd * LANES:(hd + 1) * LANES]

    fix = lambda shape: _carry_fix(carry_ref, slice(None), shape)
    u = _inproj_body(x_ref, g_ref, w_ref, cos_ref, sa_ref, sb_ref, wsc_ref,
                     q_ref, put_k, put_v, y_ref, fix, dims)
    last = u[u.shape[0] - SUBLANES:, :]
    carry_ref[...] = last
    tail_ref[0] = last


def _inproj_state_kernel(x_ref, g_ref, w_ref, cos_ref, sa_ref, sb_ref, wsc_ref,
                         sp1_ref, sp2_ref, q_ref, k_ref, v_ref, y_ref, u_ref,
                         *, dims, seq_len):
    def put_k(c, blk):
        k_ref[:, c * LANES:(c + 1) * LANES] = blk

    def put_v(v):
        v_ref[...] = v

    fix = lambda shape: _state_fix(sp1_ref[...], sp2_ref[...], seq_len, shape)
    u_ref[...] = _inproj_body(x_ref, g_ref, w_ref, cos_ref, sa_ref, sb_ref, wsc_ref,
                              q_ref, put_k, put_v, y_ref, fix, dims)


def _in_proj(x2d, g, w, tabs, w_sc, dims, *, batch, seq_len, tm, state=None):
    n, d = x2d.shape
    qc, kc, vc, cw = dims
    common_in = [None, _const_spec((1, d)), _const_spec(w.shape), None, None, None,
                 _const_spec(w_sc.shape)]
    if state is None:
        nj = seq_len // tm
        heads = vc // LANES
        row = lambda b, j: (b * nj + j, 0)
        tab = pl.BlockSpec((tm, LANES), lambda b, j: (j, 0))
        in_specs = list(common_in)
        in_specs[0] = pl.BlockSpec((tm, d), row)
        in_specs[3:6] = [tab, tab, tab]
        out_specs = [pl.BlockSpec((tm, qc), row),
                     pl.BlockSpec((1, kc, tm), lambda b, j: (b, 0, j)),
                     pl.BlockSpec((tm * heads, LANES), row),
                     pl.BlockSpec((tm, vc), row), pl.BlockSpec((tm, cw), row),
                     pl.BlockSpec((1, SUBLANES, cw), lambda b, j: (b, 0, 0))]
        out_shape = [jax.ShapeDtypeStruct((n, qc), BF16),
                     jax.ShapeDtypeStruct((batch, kc, seq_len), F32),
                     jax.ShapeDtypeStruct((n * heads, LANES), F32),
                     jax.ShapeDtypeStruct((n, vc), BF16), jax.ShapeDtypeStruct((n, cw), BF16),
                     jax.ShapeDtypeStruct((batch, SUBLANES, cw), F32)]
        return pl.pallas_call(
            functools.partial(_inproj_carry_kernel, dims=dims),
            grid=(batch, nj), in_specs=in_specs, out_specs=out_specs, out_shape=out_shape,
            scratch_shapes=[pltpu.VMEM((SUBLANES, cw), F32)],
            compiler_params=_params(2), name="in_proj_prompt",
        )(x2d, g, w, *tabs, w_sc)
    sp1, sp2 = state
    row = lambda i: (i, 0)
    tab = pl.BlockSpec((tm, LANES), row)
    in_specs = list(common_in)
    in_specs[0] = pl.BlockSpec((tm, d), row)
    in_specs[3:6] = [tab, tab, tab]
    in_specs += [pl.BlockSpec((tm, cw), row)] * 2
    out_specs = [pl.BlockSpec((tm, c), row) for c in (qc, kc, vc, cw, cw)]
    out_shape = [jax.ShapeDtypeStruct((n, qc), BF16), jax.ShapeDtypeStruct((n, kc), F32),
                 jax.ShapeDtypeStruct((n, vc), F32), jax.ShapeDtypeStruct((n, cw), BF16),
                 jax.ShapeDtypeStruct((n, cw), F32)]
    return pl.pallas_call(
        functools.partial(_inproj_state_kernel, dims=dims, seq_len=seq_len),
        grid=(n // tm,), in_specs=in_specs, out_specs=out_specs, out_shape=out_shape,
        compiler_params=_params(1), name="in_proj_sample",
    )(x2d, g, w, *tabs, w_sc, sp1, sp2)


def _diff_lambda(lq1_ref, lk1_ref, lq2_ref, lk2_ref, lam_init):
    a = jnp.sum(lq1_ref[...] * lk1_ref[...], axis=-1, keepdims=True)
    b = jnp.sum(lq2_ref[...] * lk2_ref[...], axis=-1, keepdims=True)
    return jnp.exp(a) - jnp.exp(b) + lam_init


def _stack_maps(qt):
    lane = lax.broadcasted_iota(jnp.int32, qt.shape, 1)
    first = lane < (LANES // 2)
    zero = jnp.zeros_like(qt)
    return jnp.concatenate([jnp.where(first, qt, zero), jnp.where(first, zero, qt)], axis=0)


def _finish_head(acc, l, lam, g, lam_init, tq):
    o = acc[:tq] / l[:tq] - lam * (acc[tq:] / l[tq:])
    return _rms(o, g) * (1.0 - lam_init)


def _attn_prompt_kernel(q_ref, k_ref, v_ref, lq1_ref, lk1_ref, lq2_ref, lk2_ref, g_ref,
                        o_ref, kb_ref, vt_ref, *, tq, lam_init):
    seq = q_ref.shape[1]
    hd = v_ref.shape[2]
    kb_ref[...] = k_ref[0].T.astype(BF16)
    vt_ref[:hd, :] = v_ref[0].astype(F32).T.astype(BF16)
    vt_ref[hd:, :] = jnp.ones((vt_ref.shape[0] - hd, seq), BF16)
    lam = _diff_lambda(lq1_ref, lk1_ref, lq2_ref, lk2_ref, lam_init)
    g = g_ref[...]
    r = lax.broadcasted_iota(jnp.int32, (tq, 2 * tq), 0)
    c = lax.broadcasted_iota(jnp.int32, (tq, 2 * tq), 1)
    shift = CHUNK.bit_length() - 1
    diag_bias = jnp.where((r >> shift) <= ((c & (tq - 1)) >> shift), 0.0, NEG_INF)

    nq = seq // tq
    order = list(range(0, nq, 2)) + list(range(nq - 1 - (nq % 2), 0, -2))
    def scores(i):
        lo = i * tq
        qm = _stack_maps(q_ref[0, lo:lo + tq, :])
        s_d = _dot_nt(kb_ref[lo:lo + tq, :], qm) + diag_bias
        m = jnp.max(s_d, axis=0, keepdims=True)
        s_o = None
        if i:
            s_o = _dot_nt(kb_ref[:lo, :], qm)
            m = jnp.maximum(m, jnp.max(s_o, axis=0, keepdims=True))
        return s_d, s_o, m

    def finish(i, s_d, s_o, m):
        lo = i * tq
        acc = _dot(vt_ref[:, lo:lo + tq], jnp.exp(s_d - m).astype(BF16))
        if i:
            acc = acc + _dot(vt_ref[:, :lo], jnp.exp(s_o - m).astype(BF16))
        num, l = acc[:hd, :], acc[hd:hd + 1, :]
        ot = num[:, :tq] / l[:, :tq] - lam * (num[:, tq:] / l[:, tq:])
        ot = ot * lax.rsqrt(jnp.mean(ot * ot, axis=0, keepdims=True) + EPS) * g
        o_ref[0, lo:lo + tq, :] = (ot * (1.0 - lam_init)).T.astype(BF16)

    ahead = 2
    pending = [scores(i) for i in order[:ahead]]
    for n, i in enumerate(order):
        if n + ahead < nq:
            pending.append(scores(order[n + ahead]))
        finish(i, *pending.pop(0))


def _attn_prompt(q3, kt3, v3, lams, g_sub, *, tq, lam_init):
    b, seq, width = q3.shape
    heads = width // LANES
    blk = lambda: pl.BlockSpec((1, seq, LANES), lambda i, h: (i, 0, h))
    lam_spec = _const_spec(lams[0].shape)
    g_col = g_sub.reshape(-1, 1)
    ones_rows = 2 * SUBLANES
    return pl.pallas_call(
        functools.partial(_attn_prompt_kernel, tq=tq, lam_init=lam_init),
        grid=(b, heads),
        in_specs=[blk(), pl.BlockSpec((1, LANES, seq), lambda i, h: (i, h, 0)), blk(),
                  lam_spec, lam_spec, lam_spec, lam_spec, _const_spec(g_col.shape)],
        out_specs=blk(),
        out_shape=jax.ShapeDtypeStruct((b, seq, width), BF16),
        scratch_shapes=[pltpu.VMEM((seq, LANES), BF16),
                        pltpu.VMEM((LANES + ones_rows, seq), BF16)],
        compiler_params=_params(2), name="attn_prompt",
    )(q3, kt3, v3, *lams, g_col)


def _attn_sample_kernel(q_ref, kn_ref, vn_ref, kp_ref, vp_ref, lq1_ref, lk1_ref, lq2_ref,
                        lk2_ref, g_ref, o_ref, *, lam_init):
    t_new = q_ref.shape[1]
    heads = q_ref.shape[2] // LANES
    past = kp_ref.shape[2]
    lam = _diff_lambda(lq1_ref, lk1_ref, lq2_ref, lk2_ref, lam_init)
    g = g_ref[...]
    pad = jnp.zeros((LANES - t_new, LANES), BF16)
    lane = lax.broadcasted_iota(jnp.int32, (2 * t_new, LANES), 1)
    for h in range(heads):
        cols = slice(h * LANES, (h + 1) * LANES)
        qm = _stack_maps(q_ref[0, :, cols])
        kn = jnp.concatenate([kn_ref[0, :, cols].astype(BF16), pad], axis=0)
        vn = jnp.concatenate([vn_ref[0, :, cols].astype(BF16), pad], axis=0)
        s_p = _dot(qm, kp_ref[0, cols, :].astype(BF16))
        vp = vp_ref[0, pl.ds(h, past, stride=heads), :].astype(BF16)
        s_n = jnp.where(lane < t_new, _dot_nt(qm, kn), NEG_INF)
        m = jnp.maximum(jnp.max(s_p, axis=-1, keepdims=True),
                        jnp.max(s_n, axis=-1, keepdims=True))
        e_p = jnp.exp(s_p - m)
        e_n = jnp.exp(s_n - m)
        l = jnp.sum(e_p, axis=-1, keepdims=True) + jnp.sum(e_n, axis=-1, keepdims=True)
        acc = _dot(e_p.astype(BF16), vp) + _dot(e_n.astype(BF16), vn)
        o_ref[0, :, cols] = _finish_head(acc, l, lam, g, lam_init, t_new).astype(BF16)


def _attn_sample(q3, kn3, vn3, kpt3, vp4, lams, g_sub, *, lam_init):
    b, t_new, width = q3.shape
    new = lambda: pl.BlockSpec((1, t_new, width), lambda i: (i, 0, 0))
    whole = lambda a: pl.BlockSpec((1,) + a.shape[1:], lambda i: (i, 0, 0))
    lam_spec = _const_spec(lams[0].shape)
    return pl.pallas_call(
        functools.partial(_attn_sample_kernel, lam_init=lam_init),
        grid=(b,),
        in_specs=[new(), new(), new(), whole(kpt3), whole(vp4), lam_spec, lam_spec, lam_spec,
                  lam_spec, _const_spec(g_sub.shape)],
        out_specs=new(),
        out_shape=jax.ShapeDtypeStruct((b, t_new, width), BF16),
        compiler_params=_params(1), name="attn_sample",
    )(q3, kn3, vn3, kpt3, vp4, *lams, g_sub)


def _mix_and_query(x_ref, o_ref, y_ref, wout_ref, gx_ref, wxq_ref, heads):
    aw = o_ref.shape[-1]
    x1 = x_ref[...] + _dot(o_ref[...], wout_ref[:aw, :]) + _dot(y_ref[...], wout_ref[aw:, :])
    hd = x1.shape[-1] // heads
    hq = (_dot(_rms(x1, gx_ref[...]).astype(BF16), wxq_ref[...]) * (hd ** -0.5)).astype(BF16)
    return x1, hq


def _cross_attend(hq, mk_ref, mv_ref, heads, put):
    hd = hq.shape[-1] // heads
    for h in range(heads):
        cols = slice(h * hd, (h + 1) * hd)
        s = _dot_nt(hq[:, cols], _load_mem_head(mk_ref, h, heads, hd))
        e = jnp.exp(s - jnp.max(s, axis=-1, keepdims=True))
        l = jnp.sum(e, axis=-1, keepdims=True)
        xo = _dot(e.astype(BF16), _load_mem_head(mv_ref, h, heads, hd)) / l
        put(cols, xo.astype(BF16))


def _post_kernel(x_ref, o_ref, y_ref, wout_ref, gx_ref, wxq_ref, mk_ref, mv_ref, wxo_ref,
                 out_ref, xo_ref, *, heads):
    x1, hq = _mix_and_query(x_ref, o_ref, y_ref, wout_ref, gx_ref, wxq_ref, heads)

    def put(cols, xo):
        xo_ref[:, cols] = xo

    _cross_attend(hq, mk_ref, mv_ref, heads, put)
    out_ref[...] = x1 + _dot(xo_ref[...], wxo_ref[...])


def _post_rows_kernel(x_ref, o_ref, y_ref, wout_ref, gx_ref, wxq_ref, mk_ref, mv_ref, wxo_ref,
                      out_ref, x1_ref, hq_ref, xo_ref, *, heads, seq_len):
    b = pl.program_id(0)

    @pl.when(b == 0)
    def _():
        x1_ref[...], hq_ref[...] = _mix_and_query(x_ref, o_ref, y_ref, wout_ref, gx_ref,
                                                  wxq_ref, heads)

    rows = pl.ds(pl.multiple_of(b * seq_len, seq_len), seq_len)

    def put(cols, xo):
        xo_ref[rows, cols] = xo

    _cross_attend(hq_ref[rows, :], mk_ref, mv_ref, heads, put)

    @pl.when(b == pl.num_programs(0) - 1)
    def _():
        out_ref[...] = x1_ref[...] + _dot(xo_ref[...], wxo_ref[...])


def _post_rows(x2d, o2d, y2d, w_out, g_x, w_xq, mk3, mv3, w_xo, *, batch, seq_len, heads):
    n, d = x2d.shape
    mem = pl.BlockSpec((1,) + mk3.shape[1:], lambda b: (b, 0, 0))
    return pl.pallas_call(
        functools.partial(_post_rows_kernel, heads=heads, seq_len=seq_len),
        grid=(batch,),
        in_specs=[_const_spec(x2d.shape), _const_spec(o2d.shape), _const_spec(y2d.shape),
                  _const_spec(w_out.shape), _const_spec((1, d)), _const_spec(w_xq.shape),
                  mem, mem, _const_spec(w_xo.shape)],
        out_specs=pl.BlockSpec((n, d), lambda b: (0, 0)),
        out_shape=jax.ShapeDtypeStruct((n, d), F32),
        scratch_shapes=[pltpu.VMEM((n, d), F32), pltpu.VMEM((n, d), BF16),
                        pltpu.VMEM((n, d), BF16)],
        compiler_params=_params(1), name="post_sample",
    )(x2d, o2d, y2d, w_out, g_x, w_xq, mk3, mv3, w_xo)


def _post(x2d, o2d, y2d, w_out, g_x, w_xq, mk3, mv3, w_xo, *, batch, seq_len, tm, heads):
    n, d = x2d.shape
    aw, cw = o2d.shape[-1], y2d.shape[-1]
    nj = seq_len // tm
    row = lambda b, j: (b * nj + j, 0)
    mem = pl.BlockSpec((1,) + mk3.shape[1:], lambda b, j: (b, 0, 0))
    return pl.pallas_call(
        functools.partial(_post_kernel, heads=heads),
        grid=(batch, nj),
        in_specs=[pl.BlockSpec((tm, d), row), pl.BlockSpec((tm, aw), row),
                  pl.BlockSpec((tm, cw), row), _const_spec(w_out.shape), _const_spec((1, d)),
                  _const_spec(w_xq.shape), mem, mem, _const_spec(w_xo.shape)],
        out_specs=pl.BlockSpec((tm, d), row),
        out_shape=jax.ShapeDtypeStruct((n, d), F32),
        scratch_shapes=[pltpu.VMEM((tm, d), BF16)],
        compiler_params=_params(2), name="post",
    )(x2d, o2d, y2d, w_out, g_x, w_xq, mk3, mv3, w_xo)


FF_CHUNK = 256


def _ffn_body(x_ref, g_ref, wup_ref, wgate_ref, wconv_ref, wdown_ref, gfin_ref, y_ref,
              a_ref, fix, emit_up, final_norm):
    x = x_ref[...]
    hf = _rms(x, g_ref[...]).astype(BF16)
    dff = wup_ref.shape[-1]
    for c in range(dff // FF_CHUNK):
        cols = slice(c * FF_CHUNK, (c + 1) * FF_CHUNK)
        up = _dot(hf, wup_ref[:, cols])
        emit_up(cols, up)
        uc = _conv3(up, wconv_ref[:, cols], fix(cols, up.shape))
        gate = _dot(hf, wgate_ref[:, cols])
        a_ref[:, cols] = (uc * jax.nn.sigmoid(uc) * gate).astype(BF16)
    x3 = x + _dot(a_ref[...], wdown_ref[...])
    y_ref[...] = _rms(x3, gfin_ref[...]) if final_norm else x3


def _ffn_carry_kernel(x_ref, g_ref, wup_ref, wgate_ref, wconv_ref, wdown_ref, gfin_ref,
                      y_ref, tail_ref, a_ref, carry_ref, new_ref, *, final_norm):
    @pl.when(pl.program_id(1) == 0)
    def _():
        carry_ref[...] = jnp.zeros_like(carry_ref)

    def emit_up(cols, up):
        new_ref[:, cols] = up[up.shape[0] - SUBLANES:, :]

    fix = lambda cols, shape: _carry_fix(carry_ref, cols, shape)
    _ffn_body(x_ref, g_ref, wup_ref, wgate_ref, wconv_ref, wdown_ref, gfin_ref, y_ref,
              a_ref, fix, emit_up, final_norm)
    carry_ref[...] = new_ref[...]
    tail_ref[0] = new_ref[...]


def _ffn_state_kernel(x_ref, g_ref, wup_ref, wgate_ref, wconv_ref, wdown_ref, gfin_ref,
                      sp1_ref, sp2_ref, y_ref, up_ref, a_ref, *, seq_len, final_norm):
    def emit_up(cols, up):
        up_ref[:, cols] = up

    fix = lambda cols, shape: _state_fix(sp1_ref[:, cols], sp2_ref[:, cols], seq_len, shape)
    _ffn_body(x_ref, g_ref, wup_ref, wgate_ref, wconv_ref, wdown_ref, gfin_ref, y_ref,
              a_ref, fix, emit_up, final_norm)


def _ffn(x2d, g_ffn, w_up, w_gate, w_conv, w_down, g_final, *, batch, seq_len, tm, final_norm,
         state=None):
    n, d = x2d.shape
    dff = w_up.shape[-1]
    weights = [_const_spec((1, d)), _const_spec(w_up.shape), _const_spec(w_gate.shape),
               _const_spec(w_conv.shape), _const_spec(w_down.shape), _const_spec((1, d))]
    if state is None:
        nj = seq_len // tm
        row = lambda b, j: (b * nj + j, 0)
        return pl.pallas_call(
            functools.partial(_ffn_carry_kernel, final_norm=final_norm),
            grid=(batch, nj),
            in_specs=[pl.BlockSpec((tm, d), row)] + weights,
            out_specs=[pl.BlockSpec((tm, d), row),
                       pl.BlockSpec((1, SUBLANES, dff), lambda b, j: (b, 0, 0))],
            out_shape=[jax.ShapeDtypeStruct((n, d), F32),
                       jax.ShapeDtypeStruct((batch, SUBLANES, dff), F32)],
            scratch_shapes=[pltpu.VMEM((tm, dff), BF16), pltpu.VMEM((SUBLANES, dff), F32),
                            pltpu.VMEM((SUBLANES, dff), F32)],
            compiler_params=_params(2), name="ffn_prompt",
        )(x2d, g_ffn, w_up, w_gate, w_conv, w_down, g_final)
    sp1, sp2 = state
    row = lambda i: (i, 0)
    return pl.pallas_call(
        functools.partial(_ffn_state_kernel, seq_len=seq_len, final_norm=final_norm),
        grid=(n // tm,),
        in_specs=[pl.BlockSpec((tm, d), row)] + weights + [pl.BlockSpec((tm, dff), row)] * 2,
        out_specs=[pl.BlockSpec((tm, d), row), pl.BlockSpec((tm, dff), row)],
        out_shape=[jax.ShapeDtypeStruct((n, d), F32), jax.ShapeDtypeStruct((n, dff), F32)],
        scratch_shapes=[pltpu.VMEM((tm, dff), BF16)],
        compiler_params=_params(1), name="ffn_sample",
    )(x2d, g_ffn, w_up, w_gate, w_conv, w_down, g_final, sp1, sp2)


def _rope_tables(pos, qk_head_dim):
    rot = qk_head_dim // 4
    half = rot // 2
    inv = 1.0 / (ROPE_THETA ** (jnp.arange(half, dtype=F32) * 2.0 / rot))
    ang = pos.astype(F32)[:, None] * inv[None, :]
    cos, sin = jnp.cos(ang), jnp.sin(ang)
    t = pos.shape[0]
    zeros = lambda w: jnp.zeros((t, w), F32)
    c = jnp.concatenate([cos, cos, jnp.ones((t, qk_head_dim - rot), F32)], axis=1)
    a = jnp.concatenate([-sin, zeros(qk_head_dim - half)], axis=1)
    b = jnp.concatenate([zeros(half), sin, zeros(qk_head_dim - rot)], axis=1)
    rep = LANES // qk_head_dim
    return tuple(jnp.tile(m, (1, rep)) for m in (c, a, b))


def _expand_state(state, seq_len):
    b, _, c = state.shape
    z = jnp.zeros((b, seq_len, c), state.dtype)
    sp1 = z.at[:, 0].set(state[:, 1])
    sp2 = z.at[:, 0].set(state[:, 0]).at[:, 1].set(state[:, 1])
    return sp1.reshape(b * seq_len, c), sp2.reshape(b * seq_len, c)


def kernel(x_prompt, x_sample, cache_attn_k, cache_attn_v, state_short_conv, state_ffn_conv,
           cache_mem_k, cache_mem_v, mem_prompt, g_mix, w_in, lam_q1, lam_k1, lam_q2, lam_k2,
           g_sub, w_sc, w_out, g_mem, g_x, w_xq, w_xk, w_xv, w_xo, g_ffn, w_up, w_gate,
           w_ffconv, w_down, g_final):
    depth = w_in.shape[0]
    bp, seq, d = x_prompt.shape
    bs, t_new, _ = x_sample.shape
    past = cache_attn_k.shape[2]
    heads_a, qk_dim = cache_attn_k.shape[3], cache_attn_k.shape[5]
    v_dim = cache_attn_v.shape[4]
    cw = state_short_conv.shape[-1]
    dff = state_ffn_conv.shape[-1]
    n_mem, heads_x, x_dim = cache_mem_k.shape[2:]
    qc = kc = heads_a * 2 * qk_dim
    vc = heads_a * v_dim
    dims = (qc, kc, vc, cw)
    assert 2 * qk_dim == LANES and v_dim == LANES and qk_dim == CHUNK

    tabs_p = _rope_tables(jnp.arange(seq, dtype=jnp.int32), qk_dim)
    tabs_s = tuple(jnp.tile(m, (bs, 1)) for m in
                   _rope_tables(past + jnp.arange(t_new, dtype=jnp.int32), qk_dim))
    row = lambda v: v.reshape(1, -1)

    hp = x_prompt.reshape(bp * seq, d)
    hs = x_sample.reshape(bs * t_new, d)
    outs_p = [[] for _ in range(6)]
    outs_s = [[] for _ in range(4)]
    for l in range(depth):
        lam_init = _lambda_init(l)
        wi, wo = w_in[l].astype(BF16), w_out[l].astype(BF16)
        wq, wxo_b = w_xq[l].astype(BF16), w_xo[l].astype(BF16)
        wkv = jnp.concatenate([w_xk[l], w_xv[l]], axis=1).astype(BF16)
        wu, wg, wd = w_up[l].astype(BF16), w_gate[l].astype(BF16), w_down[l].astype(BF16)
        lams = tuple(row(v[l]) for v in (lam_q1, lam_k1, lam_q2, lam_k2))
        gsub = row(g_sub[l])

        mk8, mv8, mk, mv = _mem_kv(mem_prompt.reshape(bp * n_mem, d), row(g_mem[l]), wkv,
                                   MEM_TILE, heads_x)
        q, kt, v4, vb, ysc, sc_tail = _in_proj(hp, row(g_mix[l]), wi, tabs_p, w_sc[l], dims,
                                               batch=bp, seq_len=seq, tm=PROJ_TILE)
        o = _attn_prompt(q.reshape(bp, seq, qc), kt, vb.reshape(bp, seq, vc), lams, gsub,
                         tq=Q_TILE, lam_init=lam_init)
        x2 = _post(hp, o.reshape(bp * seq, vc), ysc, wo, row(g_x[l]), wq,
                   mk.reshape(bp, n_mem, d), mv.reshape(bp, n_mem, d), wxo_b,
                   batch=bp, seq_len=seq, tm=POST_TILE, heads=heads_x)
        hp, ff_tail = _ffn(x2, row(g_ffn[l]), wu, wg, w_ffconv[l], wd, row(g_final),
                           batch=bp, seq_len=seq, tm=FFN_TILE, final_norm=l == depth - 1)
        outs_p[0].append(jnp.transpose(kt.reshape(bp, heads_a, 2, qk_dim, seq), (0, 4, 1, 2, 3)))
        outs_p[1].append(v4.reshape(bp, seq, heads_a, v_dim))
        outs_p[2].append(sc_tail[:, SUBLANES - 2:])
        outs_p[3].append(ff_tail[:, SUBLANES - 2:])
        outs_p[4].append(_from_mem_layout(mk8, bp, n_mem, heads_x, x_dim))
        outs_p[5].append(_from_mem_layout(mv8, bp, n_mem, heads_x, x_dim))

        n_s = bs * t_new
        qs, ks, vs, yscs, u_s = _in_proj(
            hs, row(g_mix[l]), wi, tabs_s, w_sc[l], dims, batch=bs, seq_len=t_new, tm=n_s,
            state=_expand_state(state_short_conv[l], t_new))
        o_s = _attn_sample(qs.reshape(bs, t_new, qc), ks.reshape(bs, t_new, kc),
                           vs.reshape(bs, t_new, vc),
                           jnp.transpose(cache_attn_k[l].reshape(bs, past, kc), (0, 2, 1)),
                           cache_attn_v[l].reshape(bs, past * heads_a, v_dim), lams, gsub,
                           lam_init=lam_init)
        x2s = _post_rows(hs, o_s.reshape(n_s, vc), yscs, wo, row(g_x[l]), wq,
                         _to_mem_layout(cache_mem_k[l], x_dim // LANES),
                         _to_mem_layout(cache_mem_v[l], x_dim // LANES), wxo_b,
                         batch=bs, seq_len=t_new, heads=heads_x)
        hs, up_s = _ffn(x2s, row(g_ffn[l]), wu, wg, w_ffconv[l], wd, row(g_final),
                        batch=bs, seq_len=t_new, tm=n_s, final_norm=l == depth - 1,
                        state=_expand_state(state_ffn_conv[l], t_new))
        outs_s[0].append(ks.reshape(bs, t_new, heads_a, 2, qk_dim))
        outs_s[1].append(vs.reshape(bs, t_new, heads_a, v_dim))
        outs_s[2].append(u_s.reshape(bs, t_new, cw)[:, t_new - 2:])
        outs_s[3].append(up_s.reshape(bs, t_new, dff)[:, t_new - 2:])

    return (hp.reshape(bp, seq, d), hs.reshape(bs, t_new, d),
            *(jnp.stack(o) for o in outs_p), *(jnp.stack(o) for o in outs_s))
```

```python
import functools
import math

import jax
import jax.numpy as jnp
from jax import lax
from jax.experimental import pallas as pl
from jax.experimental.pallas import tpu as pltpu

EPS = 1e-6
CHUNK = 64
ROPE_THETA = 500000.0
LANES = 128
SUBLANES = 8
VMEM_LIMIT = 56 * 1024 * 1024
MEM_TILE = 512
PROJ_TILE = 1024
POST_TILE = 1024
FFN_TILE = 1024
Q_TILE = 256
SAMPLE_ATTN_SEQS = 2
SAMPLE_POST_SEQS = 4
BF16 = jnp.bfloat16
F32 = jnp.float32
NEG_INF = float("-inf")


def _lambda_init(layer_idx):
    return 0.8 - 0.6 * math.exp(-0.3 * layer_idx)


def _rms(x, g):
    return x * lax.rsqrt(jnp.mean(x * x, axis=-1, keepdims=True) + EPS) * g


def _dot(a, b):
    return jnp.dot(a, b, preferred_element_type=F32)


def _dot_nt(a, b):
    return lax.dot_general(a, b, (((1,), (1,)), ((), ())), preferred_element_type=F32)


def _params(n_grid):
    return pltpu.CompilerParams(dimension_semantics=("arbitrary",) * n_grid,
                                vmem_limit_bytes=VMEM_LIMIT)


def _const_spec(shape):
    nd = len(shape)
    return pl.BlockSpec(shape, lambda *_: (0,) * nd, pipeline_mode=pl.Buffered(1))


def _conv3(u, w, fix):
    p1 = pltpu.roll(u, 1, axis=0)
    p2 = pltpu.roll(u, 2, axis=0)
    p1, p2 = fix(p1, p2)
    return w[0:1, :] * p2 + w[1:2, :] * p1 + w[2:3, :] * u


def _carry_fix(carry_ref, cols, shape):
    row = lax.broadcasted_iota(jnp.int32, shape, 0)
    c0 = carry_ref[SUBLANES - 2:SUBLANES - 1, cols]
    c1 = carry_ref[SUBLANES - 1:SUBLANES, cols]

    def fix(p1, p2):
        p1 = jnp.where(row == 0, c1, p1)
        p2 = jnp.where(row == 0, c0, jnp.where(row == 1, c1, p2))
        return p1, p2
    return fix


def _state_fix(sp1, sp2, seq_len, shape):
    row = lax.broadcasted_iota(jnp.int32, shape, 0)
    t = lax.rem(row, seq_len)

    def fix(p1, p2):
        return jnp.where(t == 0, sp1, p1), jnp.where(t < 2, sp2, p2)
    return fix


def _store_mem_layout(ref, blk, heads):
    rows, width = blk.shape
    lt = width // heads // LANES
    for h in range(heads):
        for t in range(lt):
            c0 = (h * lt + t) * LANES
            ref[pl.ds(t * heads + h, rows, stride=heads * lt), :] = blk[:, c0:c0 + LANES]


def _load_mem_head(ref, bi, h, heads, hd):
    if ref.shape[-1] != LANES:
        return ref[bi, :, h * hd:(h + 1) * hd]
    lt = hd // LANES
    rows = ref.shape[1] // (heads * lt)
    parts = [ref[bi, pl.ds(t * heads + h, rows, stride=heads * lt), :] for t in range(lt)]
    return jnp.concatenate(parts, axis=1).astype(BF16)


def _to_mem_layout(x, lt):
    b, n, heads, hd = x.shape
    x = jnp.transpose(x.reshape(b, n, heads, lt, LANES), (0, 1, 3, 2, 4))
    return x.reshape(b, n * heads * lt, LANES)


def _from_mem_layout(x8, b, n, heads, hd):
    lt = hd // LANES
    x = jnp.transpose(x8.reshape(b, n, lt, heads, LANES), (0, 1, 3, 2, 4))
    return x.reshape(b, n, heads, hd)


def _memkv_kernel(m_ref, g_ref, w_ref, k8_ref, v8_ref, kb_ref, vb_ref, *, heads):
    d = kb_ref.shape[-1]
    h = _rms(m_ref[...], g_ref[...]).astype(BF16)
    kv = _dot(h, w_ref[...])
    for part, (o8, ob) in enumerate(((k8_ref, kb_ref), (v8_ref, vb_ref))):
        blk = kv[:, part * d:(part + 1) * d]
        ob[...] = blk.astype(BF16)
        _store_mem_layout(o8, blk, heads)


def _mem_kv(mem2d, g_mem, w_xkv, tm, heads):
    n, d = mem2d.shape
    group = d // LANES
    row = lambda i: (i, 0)
    return pl.pallas_call(
        functools.partial(_memkv_kernel, heads=heads),
        grid=(n // tm,),
        in_specs=[pl.BlockSpec((tm, d), row), _const_spec((1, d)), _const_spec(w_xkv.shape)],
        out_specs=[pl.BlockSpec((tm * group, LANES), row)] * 2 + [pl.BlockSpec((tm, d), row)] * 2,
        out_shape=[jax.ShapeDtypeStruct((n * group, LANES), F32)] * 2
        + [jax.ShapeDtypeStruct((n, d), BF16)] * 2,
        compiler_params=_params(1),
        name="mem_kv",
    )(mem2d, g_mem, w_xkv)


def _inproj_body(x_ref, g_ref, w_ref, cos_ref, sa_ref, sb_ref, wsc_ref,
                 q_ref, put_k, put_v, y_ref, fix, dims):
    qc, kc, vc, cw, q_scale = dims
    h = _rms(x_ref[...], g_ref[...]).astype(BF16)
    cos, sa, sb = cos_ref[...], sa_ref[...], sb_ref[...]

    def rope(t):
        return (t * cos + pltpu.roll(t, LANES - 8, axis=1) * sa
                + pltpu.roll(t, 8, axis=1) * sb)

    qk = _dot(h, w_ref[:, :qc + kc])
    for c in range(qc // LANES):
        blk = rope(qk[:, c * LANES:(c + 1) * LANES])
        q_ref[:, c * LANES:(c + 1) * LANES] = (blk * q_scale).astype(BF16)
    for c in range(kc // LANES):
        lo = qc + c * LANES
        put_k(c, rope(qk[:, lo:lo + LANES]))
    o = qc + kc
    vb = _dot(h, w_ref[:, o:o + vc + cw])
    put_v(vb[:, :vc])
    bg = vb[:, vc:]
    cx = _dot(h, w_ref[:, o + vc + cw:])
    u = cx[:, :cw] * cx[:, cw:]
    conv = _conv3(u, wsc_ref[...], fix(u.shape))
    y_ref[...] = (bg * conv).astype(BF16)
    return u


def _inproj_carry_kernel(x_ref, g_ref, w_ref, cos_ref, sa_ref, sb_ref, wsc_ref,
                         q_ref, kt_ref, v4_ref, vb_ref, y_ref, tail_ref, carry_ref, *, dims):
    @pl.when(pl.program_id(1) == 0)
    def _():
        carry_ref[...] = jnp.zeros_like(carry_ref)

    def put_k(c, blk):
        kt_ref[0, c * LANES:(c + 1) * LANES, :] = blk.T

    def put_v(v):
        vb_ref[...] = v.astype(BF16)
        heads = v.shape[1] // LANES
        for hd in range(heads):
            v4_ref[pl.ds(hd, v.shape[0], stride=heads), :] = v[:, hd * LANES:(hd + 1) * LANES]

    fix = lambda shape: _carry_fix(carry_ref, slice(None), shape)
    u = _inproj_body(x_ref, g_ref, w_ref, cos_ref, sa_ref, sb_ref, wsc_ref,
                     q_ref, put_k, put_v, y_ref, fix, dims)
    last = u[u.shape[0] - SUBLANES:, :]
    carry_ref[...] = last
    tail_ref[0] = last


def _inproj_state_kernel(x_ref, g_ref, w_ref, cos_ref, sa_ref, sb_ref, wsc_ref,
                         sp1_ref, sp2_ref, q_ref, k_ref, v_ref, y_ref, u_ref,
                         *, dims, seq_len):
    def put_k(c, blk):
        k_ref[:, c * LANES:(c + 1) * LANES] = blk

    def put_v(v):
        v_ref[...] = v

    fix = lambda shape: _state_fix(sp1_ref[...], sp2_ref[...], seq_len, shape)
    u_ref[...] = _inproj_body(x_ref, g_ref, w_ref, cos_ref, sa_ref, sb_ref, wsc_ref,
                              q_ref, put_k, put_v, y_ref, fix, dims)


def _in_proj(x2d, g, w, tabs, w_sc, dims, *, batch, seq_len, tm, state=None):
    n, d = x2d.shape
    qc, kc, vc, cw, _ = dims
    common_in = [None, _const_spec((1, d)), _const_spec(w.shape), None, None, None,
                 _const_spec(w_sc.shape)]
    if state is None:
        nj = seq_len // tm
        heads = vc // LANES
        row = lambda b, j: (b * nj + j, 0)
        tab = pl.BlockSpec((tm, LANES), lambda b, j: (j, 0))
        in_specs = list(common_in)
        in_specs[0] = pl.BlockSpec((tm, d), row)
        in_specs[3:6] = [tab, tab, tab]
        out_specs = [pl.BlockSpec((tm, qc), row),
                     pl.BlockSpec((1, kc, tm), lambda b, j: (b, 0, j)),
                     pl.BlockSpec((tm * heads, LANES), row),
                     pl.BlockSpec((tm, vc), row), pl.BlockSpec((tm, cw), row),
                     pl.BlockSpec((1, SUBLANES, cw), lambda b, j: (b, 0, 0))]
        out_shape = [jax.ShapeDtypeStruct((n, qc), BF16),
                     jax.ShapeDtypeStruct((batch, kc, seq_len), F32),
                     jax.ShapeDtypeStruct((n * heads, LANES), F32),
                     jax.ShapeDtypeStruct((n, vc), BF16), jax.ShapeDtypeStruct((n, cw), BF16),
                     jax.ShapeDtypeStruct((batch, SUBLANES, cw), F32)]
        return pl.pallas_call(
            functools.partial(_inproj_carry_kernel, dims=dims),
            grid=(batch, nj), in_specs=in_specs, out_specs=out_specs, out_shape=out_shape,
            scratch_shapes=[pltpu.VMEM((SUBLANES, cw), F32)],
            compiler_params=_params(2), name="in_proj_prompt",
        )(x2d, g, w, *tabs, w_sc)
    sp1, sp2 = state
    row = lambda i: (i, 0)
    tab = pl.BlockSpec((tm, LANES), row)
    in_specs = list(common_in)
    in_specs[0] = pl.BlockSpec((tm, d), row)
    in_specs[3:6] = [tab, tab, tab]
    in_specs += [pl.BlockSpec((tm, cw), row)] * 2
    out_specs = [pl.BlockSpec((tm, c), row) for c in (qc, kc, vc, cw, cw)]
    out_shape = [jax.ShapeDtypeStruct((n, qc), BF16), jax.ShapeDtypeStruct((n, kc), F32),
                 jax.ShapeDtypeStruct((n, vc), F32), jax.ShapeDtypeStruct((n, cw), BF16),
                 jax.ShapeDtypeStruct((n, cw), F32)]
    return pl.pallas_call(
        functools.partial(_inproj_state_kernel, dims=dims, seq_len=seq_len),
        grid=(n // tm,), in_specs=in_specs, out_specs=out_specs, out_shape=out_shape,
        compiler_params=_params(1), name="in_proj_sample",
    )(x2d, g, w, *tabs, w_sc, sp1, sp2)


def _diff_lambda(lq1_ref, lk1_ref, lq2_ref, lk2_ref, lam_init):
    a = jnp.sum(lq1_ref[...] * lk1_ref[...], axis=-1, keepdims=True)
    b = jnp.sum(lq2_ref[...] * lk2_ref[...], axis=-1, keepdims=True)
    return jnp.exp(a) - jnp.exp(b) + lam_init


def _stack_maps(qt):
    lane = lax.broadcasted_iota(jnp.int32, qt.shape, 1)
    first = lane < (LANES // 2)
    zero = jnp.zeros_like(qt)
    return jnp.concatenate([jnp.where(first, qt, zero), jnp.where(first, zero, qt)], axis=0)


def _finish_head(acc, l, lam, g, lam_init, tq):
    o = acc[:tq] / l[:tq] - lam * (acc[tq:] / l[tq:])
    return _rms(o, g) * (1.0 - lam_init)


def _attn_prompt_kernel(q_ref, k_ref, v_ref, lq1_ref, lk1_ref, lq2_ref, lk2_ref, g_ref,
                        o_ref, kb_ref, vt_ref, *, tq, lam_init):
    seq = q_ref.shape[1]
    hd = v_ref.shape[2]
    kb_ref[...] = k_ref[0].T.astype(BF16)
    vt_ref[:hd, :] = v_ref[0].astype(F32).T.astype(BF16)
    vt_ref[hd:, :] = jnp.ones((vt_ref.shape[0] - hd, seq), BF16)
    lam = _diff_lambda(lq1_ref, lk1_ref, lq2_ref, lk2_ref, lam_init)
    g = g_ref[...]
    r = lax.broadcasted_iota(jnp.int32, (tq, 2 * tq), 0)
    c = lax.broadcasted_iota(jnp.int32, (tq, 2 * tq), 1)
    shift = CHUNK.bit_length() - 1
    diag_bias = jnp.where((r >> shift) <= ((c & (tq - 1)) >> shift), 0.0, NEG_INF)

    nq = seq // tq
    order = list(range(0, nq, 2)) + list(range(nq - 1 - (nq % 2), 0, -2))
    def scores(i):
        lo = i * tq
        qm = _stack_maps(q_ref[0, lo:lo + tq, :])
        s_d = _dot_nt(kb_ref[lo:lo + tq, :], qm) + diag_bias
        m = jnp.max(s_d, axis=0, keepdims=True)
        s_o = None
        if i:
            s_o = _dot_nt(kb_ref[:lo, :], qm)
            m = jnp.maximum(m, jnp.max(s_o, axis=0, keepdims=True))
        return s_d, s_o, m

    def finish(i, s_d, s_o, m):
        lo = i * tq
        acc = _dot(vt_ref[:, lo:lo + tq], jnp.exp2(s_d - m).astype(BF16))
        if i:
            acc = acc + _dot(vt_ref[:, :lo], jnp.exp2(s_o - m).astype(BF16))
        num, l = acc[:hd, :], acc[hd:hd + 1, :]
        ot = num[:, :tq] / l[:, :tq] - lam * (num[:, tq:] / l[:, tq:])
        ot = ot * lax.rsqrt(jnp.mean(ot * ot, axis=0, keepdims=True) + EPS) * g
        o_ref[0, lo:lo + tq, :] = (ot * (1.0 - lam_init)).T.astype(BF16)

    ahead = 2
    pending = [scores(i) for i in order[:ahead]]
    for n, i in enumerate(order):
        if n + ahead < nq:
            pending.append(scores(order[n + ahead]))
        finish(i, *pending.pop(0))


def _attn_prompt(q3, kt3, v3, lams, g_sub, *, tq, lam_init):
    b, seq, width = q3.shape
    heads = width // LANES
    blk = lambda: pl.BlockSpec((1, seq, LANES), lambda i, h: (i, 0, h))
    lam_spec = _const_spec(lams[0].shape)
    g_col = g_sub.reshape(-1, 1)
    ones_rows = 2 * SUBLANES
    return pl.pallas_call(
        functools.partial(_attn_prompt_kernel, tq=tq, lam_init=lam_init),
        grid=(b, heads),
        in_specs=[blk(), pl.BlockSpec((1, LANES, seq), lambda i, h: (i, h, 0)), blk(),
                  lam_spec, lam_spec, lam_spec, lam_spec, _const_spec(g_col.shape)],
        out_specs=blk(),
        out_shape=jax.ShapeDtypeStruct((b, seq, width), BF16),
        scratch_shapes=[pltpu.VMEM((seq, LANES), BF16),
                        pltpu.VMEM((LANES + ones_rows, seq), BF16)],
        compiler_params=_params(2), name="attn_prompt",
    )(q3, kt3, v3, *lams, g_col)


def _attn_sample_kernel(q_ref, kn_ref, vn_ref, kp_ref, vp_ref, lq1_ref, lk1_ref, lq2_ref,
                        lk2_ref, g_ref, o_ref, *, lam_init):
    t_new = q_ref.shape[1]
    heads = q_ref.shape[2] // LANES
    past = kp_ref.shape[2]
    lam = _diff_lambda(lq1_ref, lk1_ref, lq2_ref, lk2_ref, lam_init)
    g = g_ref[...]
    pad = jnp.zeros((LANES - t_new, LANES), BF16)
    lane = lax.broadcasted_iota(jnp.int32, (2 * t_new, LANES), 1)
    for bi, h in [(bi, h) for bi in range(q_ref.shape[0]) for h in range(heads)]:
        cols = slice(h * LANES, (h + 1) * LANES)
        qm = _stack_maps(q_ref[bi, :, cols])
        kn = jnp.concatenate([kn_ref[bi, :, cols].astype(BF16), pad], axis=0)
        vn = jnp.concatenate([vn_ref[bi, :, cols].astype(BF16), pad], axis=0)
        s_p = _dot(qm, kp_ref[bi, cols, :].astype(BF16))
        vp = vp_ref[bi, pl.ds(h, past, stride=heads), :].astype(BF16)
        s_n = jnp.where(lane < t_new, _dot_nt(qm, kn), NEG_INF)
        m = jnp.maximum(jnp.max(s_p, axis=-1, keepdims=True),
                        jnp.max(s_n, axis=-1, keepdims=True))
        e_p = jnp.exp2(s_p - m)
        e_n = jnp.exp2(s_n - m)
        l = jnp.sum(e_p, axis=-1, keepdims=True) + jnp.sum(e_n, axis=-1, keepdims=True)
        acc = _dot(e_p.astype(BF16), vp) + _dot(e_n.astype(BF16), vn)
        o_ref[bi, :, cols] = _finish_head(acc, l, lam, g, lam_init, t_new).astype(BF16)


def _attn_sample(q3, kn3, vn3, kpt3, vp4, lams, g_sub, *, lam_init, nb):
    b, t_new, width = q3.shape
    new = lambda: pl.BlockSpec((nb, t_new, width), lambda i: (i, 0, 0))
    whole = lambda a: pl.BlockSpec((nb,) + a.shape[1:], lambda i: (i, 0, 0))
    lam_spec = _const_spec(lams[0].shape)
    return pl.pallas_call(
        functools.partial(_attn_sample_kernel, lam_init=lam_init),
        grid=(b // nb,),
        in_specs=[new(), new(), new(), whole(kpt3), whole(vp4), lam_spec, lam_spec, lam_spec,
                  lam_spec, _const_spec(g_sub.shape)],
        out_specs=new(),
        out_shape=jax.ShapeDtypeStruct((b, t_new, width), BF16),
        compiler_params=_params(1), name="attn_sample",
    )(q3, kn3, vn3, kpt3, vp4, *lams, g_sub)


def _mix_and_query(x_ref, o_ref, y_ref, wout_ref, gx_ref, wxq_ref, heads):
    aw = o_ref.shape[-1]
    x1 = x_ref[...] + _dot(o_ref[...], wout_ref[:aw, :]) + _dot(y_ref[...], wout_ref[aw:, :])
    hd = x1.shape[-1] // heads
    scale = hd ** -0.5 * math.log2(math.e)
    hq = (_dot(_rms(x1, gx_ref[...]).astype(BF16), wxq_ref[...]) * scale).astype(BF16)
    return x1, hq


def _cross_attend(hq, mk_ref, mv_ref, bi, heads, put):
    hd = hq.shape[-1] // heads
    for h in range(heads):
        cols = slice(h * hd, (h + 1) * hd)
        s = _dot_nt(hq[:, cols], _load_mem_head(mk_ref, bi, h, heads, hd))
        e = jnp.exp2(s - jnp.max(s, axis=-1, keepdims=True))
        l = jnp.sum(e, axis=-1, keepdims=True)
        xo = _dot(e.astype(BF16), _load_mem_head(mv_ref, bi, h, heads, hd)) / l
        put(cols, xo.astype(BF16))


def _post_kernel(x_ref, o_ref, y_ref, wout_ref, gx_ref, wxq_ref, mk_ref, mv_ref, wxo_ref,
                 out_ref, xo_ref, *, heads):
    x1, hq = _mix_and_query(x_ref, o_ref, y_ref, wout_ref, gx_ref, wxq_ref, heads)

    def put(cols, xo):
        xo_ref[:, cols] = xo

    _cross_attend(hq, mk_ref, mv_ref, 0, heads, put)
    out_ref[...] = x1 + _dot(xo_ref[...], wxo_ref[...])


def _post_rows_kernel(x_ref, o_ref, y_ref, wout_ref, gx_ref, wxq_ref, mk_ref, mv_ref, wxo_ref,
                      out_ref, x1_ref, hq_ref, xo_ref, *, heads, seq_len):
    b = pl.program_id(0)
    nb = mk_ref.shape[0]

    @pl.when(b == 0)
    def _():
        x1_ref[...], hq_ref[...] = _mix_and_query(x_ref, o_ref, y_ref, wout_ref, gx_ref,
                                                  wxq_ref, heads)

    for bi in range(nb):
        rows = pl.ds(pl.multiple_of((b * nb + bi) * seq_len, seq_len), seq_len)

        def put(cols, xo, rows=rows):
            xo_ref[rows, cols] = xo

        _cross_attend(hq_ref[rows, :], mk_ref, mv_ref, bi, heads, put)

    @pl.when(b == pl.num_programs(0) - 1)
    def _():
        out_ref[...] = x1_ref[...] + _dot(xo_ref[...], wxo_ref[...])


def _post_rows(x2d, o2d, y2d, w_out, g_x, w_xq, mk3, mv3, w_xo, *, batch, seq_len, heads, nb):
    n, d = x2d.shape
    mem = pl.BlockSpec((nb,) + mk3.shape[1:], lambda b: (b, 0, 0))
    return pl.pallas_call(
        functools.partial(_post_rows_kernel, heads=heads, seq_len=seq_len),
        grid=(batch // nb,),
        in_specs=[_const_spec(x2d.shape), _const_spec(o2d.shape), _const_spec(y2d.shape),
                  _const_spec(w_out.shape), _const_spec((1, d)), _const_spec(w_xq.shape),
                  mem, mem, _const_spec(w_xo.shape)],
        out_specs=pl.BlockSpec((n, d), lambda b: (0, 0)),
        out_shape=jax.ShapeDtypeStruct((n, d), F32),
        scratch_shapes=[pltpu.VMEM((n, d), F32), pltpu.VMEM((n, d), BF16),
                        pltpu.VMEM((n, d), BF16)],
        compiler_params=_params(1), name="post_sample",
    )(x2d, o2d, y2d, w_out, g_x, w_xq, mk3, mv3, w_xo)


def _post(x2d, o2d, y2d, w_out, g_x, w_xq, mk3, mv3, w_xo, *, batch, seq_len, tm, heads):
    n, d = x2d.shape
    aw, cw = o2d.shape[-1], y2d.shape[-1]
    nj = seq_len // tm
    row = lambda b, j: (b * nj + j, 0)
    mem = pl.BlockSpec((1,) + mk3.shape[1:], lambda b, j: (b, 0, 0))
    return pl.pallas_call(
        functools.partial(_post_kernel, heads=heads),
        grid=(batch, nj),
        in_specs=[pl.BlockSpec((tm, d), row), pl.BlockSpec((tm, aw), row),
                  pl.BlockSpec((tm, cw), row), _const_spec(w_out.shape), _const_spec((1, d)),
                  _const_spec(w_xq.shape), mem, mem, _const_spec(w_xo.shape)],
        out_specs=pl.BlockSpec((tm, d), row),
        out_shape=jax.ShapeDtypeStruct((n, d), F32),
        scratch_shapes=[pltpu.VMEM((tm, d), BF16)],
        compiler_params=_params(2), name="post",
    )(x2d, o2d, y2d, w_out, g_x, w_xq, mk3, mv3, w_xo)


FF_CHUNK = 256


def _ffn_body(x_ref, g_ref, wup_ref, wgate_ref, wconv_ref, wdown_ref, gfin_ref, y_ref,
              a_ref, fix, emit_up, final_norm):
    x = x_ref[...]
    hf = _rms(x, g_ref[...]).astype(BF16)
    dff = wup_ref.shape[-1]
    for c in range(dff // FF_CHUNK):
        cols = slice(c * FF_CHUNK, (c + 1) * FF_CHUNK)
        up = _dot(hf, wup_ref[:, cols])
        emit_up(cols, up)
        uc = _conv3(up, wconv_ref[:, cols], fix(cols, up.shape))
        gate = _dot(hf, wgate_ref[:, cols])
        a_ref[:, cols] = (uc * jax.nn.sigmoid(uc) * gate).astype(BF16)
    x3 = x + _dot(a_ref[...], wdown_ref[...])
    y_ref[...] = _rms(x3, gfin_ref[...]) if final_norm else x3


def _ffn_carry_kernel(x_ref, g_ref, wup_ref, wgate_ref, wconv_ref, wdown_ref, gfin_ref,
                      y_ref, tail_ref, a_ref, carry_ref, new_ref, *, final_norm):
    @pl.when(pl.program_id(1) == 0)
    def _():
        carry_ref[...] = jnp.zeros_like(carry_ref)

    def emit_up(cols, up):
        new_ref[:, cols] = up[up.shape[0] - SUBLANES:, :]

    fix = lambda cols, shape: _carry_fix(carry_ref, cols, shape)
    _ffn_body(x_ref, g_ref, wup_ref, wgate_ref, wconv_ref, wdown_ref, gfin_ref, y_ref,
              a_ref, fix, emit_up, final_norm)
    carry_ref[...] = new_ref[...]
    tail_ref[0] = new_ref[...]


def _ffn_state_kernel(x_ref, g_ref, wup_ref, wgate_ref, wconv_ref, wdown_ref, gfin_ref,
                      sp1_ref, sp2_ref, y_ref, up_ref, a_ref, *, seq_len, final_norm):
    def emit_up(cols, up):
        up_ref[:, cols] = up

    fix = lambda cols, shape: _state_fix(sp1_ref[:, cols], sp2_ref[:, cols], seq_len, shape)
    _ffn_body(x_ref, g_ref, wup_ref, wgate_ref, wconv_ref, wdown_ref, gfin_ref, y_ref,
              a_ref, fix, emit_up, final_norm)


def _ffn(x2d, g_ffn, w_up, w_gate, w_conv, w_down, g_final, *, batch, seq_len, tm, final_norm,
         state=None):
    n, d = x2d.shape
    dff = w_up.shape[-1]
    weights = [_const_spec((1, d)), _const_spec(w_up.shape), _const_spec(w_gate.shape),
               _const_spec(w_conv.shape), _const_spec(w_down.shape), _const_spec((1, d))]
    if state is None:
        nj = seq_len // tm
        row = lambda b, j: (b * nj + j, 0)
        return pl.pallas_call(
            functools.partial(_ffn_carry_kernel, final_norm=final_norm),
            grid=(batch, nj),
            in_specs=[pl.BlockSpec((tm, d), row)] + weights,
            out_specs=[pl.BlockSpec((tm, d), row),
                       pl.BlockSpec((1, SUBLANES, dff), lambda b, j: (b, 0, 0))],
            out_shape=[jax.ShapeDtypeStruct((n, d), F32),
                       jax.ShapeDtypeStruct((batch, SUBLANES, dff), F32)],
            scratch_shapes=[pltpu.VMEM((tm, dff), BF16), pltpu.VMEM((SUBLANES, dff), F32),
                            pltpu.VMEM((SUBLANES, dff), F32)],
            compiler_params=_params(2), name="ffn_prompt",
        )(x2d, g_ffn, w_up, w_gate, w_conv, w_down, g_final)
    sp1, sp2 = state
    row = lambda i: (i, 0)
    return pl.pallas_call(
        functools.partial(_ffn_state_kernel, seq_len=seq_len, final_norm=final_norm),
        grid=(n // tm,),
        in_specs=[pl.BlockSpec((tm, d), row)] + weights + [pl.BlockSpec((tm, dff), row)] * 2,
        out_specs=[pl.BlockSpec((tm, d), row), pl.BlockSpec((tm, dff), row)],
        out_shape=[jax.ShapeDtypeStruct((n, d), F32), jax.ShapeDtypeStruct((n, dff), F32)],
        scratch_shapes=[pltpu.VMEM((tm, dff), BF16)],
        compiler_params=_params(1), name="ffn_sample",
    )(x2d, g_ffn, w_up, w_gate, w_conv, w_down, g_final, sp1, sp2)


def _rope_tables(pos, qk_head_dim):
    rot = qk_head_dim // 4
    half = rot // 2
    inv = 1.0 / (ROPE_THETA ** (jnp.arange(half, dtype=F32) * 2.0 / rot))
    ang = pos.astype(F32)[:, None] * inv[None, :]
    cos, sin = jnp.cos(ang), jnp.sin(ang)
    t = pos.shape[0]
    zeros = lambda w: jnp.zeros((t, w), F32)
    c = jnp.concatenate([cos, cos, jnp.ones((t, qk_head_dim - rot), F32)], axis=1)
    a = jnp.concatenate([-sin, zeros(qk_head_dim - half)], axis=1)
    b = jnp.concatenate([zeros(half), sin, zeros(qk_head_dim - rot)], axis=1)
    rep = LANES // qk_head_dim
    return tuple(jnp.tile(m, (1, rep)) for m in (c, a, b))


def _expand_state(state, seq_len):
    b, _, c = state.shape
    z = jnp.zeros((b, seq_len, c), state.dtype)
    sp1 = z.at[:, 0].set(state[:, 1])
    sp2 = z.at[:, 0].set(state[:, 0]).at[:, 1].set(state[:, 1])
    return sp1.reshape(b * seq_len, c), sp2.reshape(b * seq_len, c)


def kernel(x_prompt, x_sample, cache_attn_k, cache_attn_v, state_short_conv, state_ffn_conv,
           cache_mem_k, cache_mem_v, mem_prompt, g_mix, w_in, lam_q1, lam_k1, lam_q2, lam_k2,
           g_sub, w_sc, w_out, g_mem, g_x, w_xq, w_xk, w_xv, w_xo, g_ffn, w_up, w_gate,
           w_ffconv, w_down, g_final):
    depth = w_in.shape[0]
    bp, seq, d = x_prompt.shape
    bs, t_new, _ = x_sample.shape
    past = cache_attn_k.shape[2]
    heads_a, qk_dim = cache_attn_k.shape[3], cache_attn_k.shape[5]
    v_dim = cache_attn_v.shape[4]
    cw = state_short_conv.shape[-1]
    dff = state_ffn_conv.shape[-1]
    n_mem, heads_x, x_dim = cache_mem_k.shape[2:]
    qc = kc = heads_a * 2 * qk_dim
    vc = heads_a * v_dim
    dims = (qc, kc, vc, cw, qk_dim ** -0.5 * math.log2(math.e))
    assert 2 * qk_dim == LANES and v_dim == LANES and qk_dim == CHUNK

    tabs_p = _rope_tables(jnp.arange(seq, dtype=jnp.int32), qk_dim)
    tabs_s = tuple(jnp.tile(m, (bs, 1)) for m in
                   _rope_tables(past + jnp.arange(t_new, dtype=jnp.int32), qk_dim))
    row = lambda v: v.reshape(1, -1)

    hp = x_prompt.reshape(bp * seq, d)
    hs = x_sample.reshape(bs * t_new, d)
    outs_p = [[] for _ in range(6)]
    outs_s = [[] for _ in range(4)]
    for l in range(depth):
        lam_init = _lambda_init(l)
        wi, wo = w_in[l].astype(BF16), w_out[l].astype(BF16)
        wq, wxo_b = w_xq[l].astype(BF16), w_xo[l].astype(BF16)
        wkv = jnp.concatenate([w_xk[l], w_xv[l]], axis=1).astype(BF16)
        wu, wg, wd = w_up[l].astype(BF16), w_gate[l].astype(BF16), w_down[l].astype(BF16)
        lams = tuple(row(v[l]) for v in (lam_q1, lam_k1, lam_q2, lam_k2))
        gsub = row(g_sub[l])

        mk8, mv8, mk, mv = _mem_kv(mem_prompt.reshape(bp * n_mem, d), row(g_mem[l]), wkv,
                                   MEM_TILE, heads_x)
        q, kt, v4, vb, ysc, sc_tail = _in_proj(hp, row(g_mix[l]), wi, tabs_p, w_sc[l], dims,
                                               batch=bp, seq_len=seq, tm=PROJ_TILE)
        o = _attn_prompt(q.reshape(bp, seq, qc), kt, vb.reshape(bp, seq, vc), lams, gsub,
                         tq=Q_TILE, lam_init=lam_init)
        x2 = _post(hp, o.reshape(bp * seq, vc), ysc, wo, row(g_x[l]), wq,
                   mk.reshape(bp, n_mem, d), mv.reshape(bp, n_mem, d), wxo_b,
                   batch=bp, seq_len=seq, tm=POST_TILE, heads=heads_x)
        hp, ff_tail = _ffn(x2, row(g_ffn[l]), wu, wg, w_ffconv[l], wd, row(g_final),
                           batch=bp, seq_len=seq, tm=FFN_TILE, final_norm=l == depth - 1)
        outs_p[0].append(jnp.transpose(kt.reshape(bp, heads_a, 2, qk_dim, seq), (0, 4, 1, 2, 3)))
        outs_p[1].append(v4.reshape(bp, seq, heads_a, v_dim))
        outs_p[2].append(sc_tail[:, SUBLANES - 2:])
        outs_p[3].append(ff_tail[:, SUBLANES - 2:])
        outs_p[4].append(_from_mem_layout(mk8, bp, n_mem, heads_x, x_dim))
        outs_p[5].append(_from_mem_layout(mv8, bp, n_mem, heads_x, x_dim))

        n_s = bs * t_new
        qs, ks, vs, yscs, u_s = _in_proj(
            hs, row(g_mix[l]), wi, tabs_s, w_sc[l], dims, batch=bs, seq_len=t_new, tm=n_s,
            state=_expand_state(state_short_conv[l], t_new))
        o_s = _attn_sample(qs.reshape(bs, t_new, qc), ks.reshape(bs, t_new, kc),
                           vs.reshape(bs, t_new, vc),
                           jnp.transpose(cache_attn_k[l].reshape(bs, past, kc), (0, 2, 1)),
                           cache_attn_v[l].reshape(bs, past * heads_a, v_dim), lams, gsub,
                           lam_init=lam_init, nb=SAMPLE_ATTN_SEQS)
        x2s = _post_rows(hs, o_s.reshape(n_s, vc), yscs, wo, row(g_x[l]), wq,
                         _to_mem_layout(cache_mem_k[l], x_dim // LANES),
                         _to_mem_layout(cache_mem_v[l], x_dim // LANES), wxo_b,
                         batch=bs, seq_len=t_new, heads=heads_x, nb=SAMPLE_POST_SEQS)
        hs, up_s = _ffn(x2s, row(g_ffn[l]), wu, wg, w_ffconv[l], wd, row(g_final),
                        batch=bs, seq_len=t_new, tm=n_s, final_norm=l == depth - 1,
                        state=_expand_state(state_ffn_conv[l], t_new))
        outs_s[0].append(ks.reshape(bs, t_new, heads_a, 2, qk_dim))
        outs_s[1].append(vs.reshape(bs, t_new, heads_a, v_dim))
        outs_s[2].append(u_s.reshape(bs, t_new, cw)[:, t_new - 2:])
        outs_s[3].append(up_s.reshape(bs, t_new, dff)[:, t_new - 2:])

    return (hp.reshape(bp, seq, d), hs.reshape(bs, t_new, d),
            *(jnp.stack(o) for o in outs_p), *(jnp.stack(o) for o in outs_s))
```

```python
import functools
import math

import jax
import jax.numpy as jnp
from jax import lax
from jax.experimental import pallas as pl
from jax.experimental.pallas import tpu as pltpu

EPS = 1e-6
CHUNK = 64
ROPE_THETA = 500000.0
LANES = 128
SUBLANES = 8
VMEM_LIMIT = 56 * 1024 * 1024
MEM_TILE = 512
PROJ_TILE = 1024
POST_TILE = 1024
FFN_TILE = 1024
Q_TILE = 256
SAMPLE_ATTN_SEQS = 2
SAMPLE_POST_SEQS = 4
BF16 = jnp.bfloat16
F32 = jnp.float32
NEG_INF = float("-inf")


def _lambda_init(layer_idx):
    return 0.8 - 0.6 * math.exp(-0.3 * layer_idx)


def _rms(x, g):
    return x * lax.rsqrt(jnp.mean(x * x, axis=-1, keepdims=True) + EPS) * g


def _dot(a, b):
    return jnp.dot(a, b, preferred_element_type=F32)


def _dot_nt(a, b):
    return lax.dot_general(a, b, (((1,), (1,)), ((), ())), preferred_element_type=F32)


def _params(n_grid):
    return pltpu.CompilerParams(dimension_semantics=("arbitrary",) * n_grid,
                                vmem_limit_bytes=VMEM_LIMIT)


def _const_spec(shape):
    nd = len(shape)
    return pl.BlockSpec(shape, lambda *_: (0,) * nd, pipeline_mode=pl.Buffered(1))


def _conv3(u, w, fix):
    p1 = pltpu.roll(u, 1, axis=0)
    p2 = pltpu.roll(u, 2, axis=0)
    p1, p2 = fix(p1, p2)
    return w[0:1, :] * p2 + w[1:2, :] * p1 + w[2:3, :] * u


def _carry_fix(carry_ref, cols, shape):
    row = lax.broadcasted_iota(jnp.int32, shape, 0)
    c0 = carry_ref[SUBLANES - 2:SUBLANES - 1, cols]
    c1 = carry_ref[SUBLANES - 1:SUBLANES, cols]

    def fix(p1, p2):
        p1 = jnp.where(row == 0, c1, p1)
        p2 = jnp.where(row == 0, c0, jnp.where(row == 1, c1, p2))
        return p1, p2
    return fix


def _state_fix(sp1, sp2, seq_len, shape):
    row = lax.broadcasted_iota(jnp.int32, shape, 0)
    t = lax.rem(row, seq_len)

    def fix(p1, p2):
        return jnp.where(t == 0, sp1, p1), jnp.where(t < 2, sp2, p2)
    return fix


def _store_mem_layout(ref, blk, heads):
    rows, width = blk.shape
    lt = width // heads // LANES
    for h in range(heads):
        for t in range(lt):
            c0 = (h * lt + t) * LANES
            ref[pl.ds(t * heads + h, rows, stride=heads * lt), :] = blk[:, c0:c0 + LANES]


def _load_mem_head(ref, bi, h, heads, hd):
    if ref.shape[-1] != LANES:
        return ref[bi, :, h * hd:(h + 1) * hd]
    lt = hd // LANES
    rows = ref.shape[1] // (heads * lt)
    parts = [ref[bi, pl.ds(t * heads + h, rows, stride=heads * lt), :] for t in range(lt)]
    return jnp.concatenate(parts, axis=1).astype(BF16)


def _to_mem_layout(x, lt):
    b, n, heads, hd = x.shape
    x = jnp.transpose(x.reshape(b, n, heads, lt, LANES), (0, 1, 3, 2, 4))
    return x.reshape(b, n * heads * lt, LANES)


def _from_mem_layout(x8, b, n, heads, hd):
    lt = hd // LANES
    x = jnp.transpose(x8.reshape(b, n, lt, heads, LANES), (0, 1, 3, 2, 4))
    return x.reshape(b, n, heads, hd)


def _memkv_kernel(m_ref, g_ref, w_ref, k8_ref, v8_ref, kb_ref, vb_ref, *, heads):
    d = kb_ref.shape[-1]
    h = _rms(m_ref[...], g_ref[...]).astype(BF16)
    kv = _dot(h, w_ref[...])
    for part, (o8, ob) in enumerate(((k8_ref, kb_ref), (v8_ref, vb_ref))):
        blk = kv[:, part * d:(part + 1) * d]
        ob[...] = blk.astype(BF16)
        _store_mem_layout(o8, blk, heads)


def _mem_kv(mem2d, g_mem, w_xkv, tm, heads):
    n, d = mem2d.shape
    group = d // LANES
    row = lambda i: (i, 0)
    return pl.pallas_call(
        functools.partial(_memkv_kernel, heads=heads),
        grid=(n // tm,),
        in_specs=[pl.BlockSpec((tm, d), row), _const_spec((1, d)), _const_spec(w_xkv.shape)],
        out_specs=[pl.BlockSpec((tm * group, LANES), row)] * 2 + [pl.BlockSpec((tm, d), row)] * 2,
        out_shape=[jax.ShapeDtypeStruct((n * group, LANES), F32)] * 2
        + [jax.ShapeDtypeStruct((n, d), BF16)] * 2,
        compiler_params=_params(1),
        name="mem_kv",
    )(mem2d, g_mem, w_xkv)


def _inproj_body(x_ref, g_ref, w_ref, cos_ref, sa_ref, sb_ref, wsc_ref,
                 put_q, put_k, put_v, y_ref, fix, dims):
    qc, kc, vc, cw, q_scale = dims
    h = _rms(x_ref[...], g_ref[...]).astype(BF16)
    cos, sa, sb = cos_ref[...], sa_ref[...], sb_ref[...]

    def rope(t):
        return (t * cos + pltpu.roll(t, LANES - 8, axis=1) * sa
                + pltpu.roll(t, 8, axis=1) * sb)

    o = qc + kc
    cx = _dot(h, w_ref[:, o + vc + cw:])
    u = cx[:, :cw] * cx[:, cw:]
    conv = _conv3(u, wsc_ref[...], fix(u.shape))
    qk = _dot(h, w_ref[:, :qc + kc])
    for c in range(qc // LANES):
        blk = rope(qk[:, c * LANES:(c + 1) * LANES])
        put_q(c, (blk * q_scale).astype(BF16))
    for c in range(kc // LANES):
        lo = qc + c * LANES
        put_k(c, rope(qk[:, lo:lo + LANES]))
    put_v(_dot(h, w_ref[:, o:o + vc]))
    bg = _dot(h, w_ref[:, o + vc:o + vc + cw])
    y_ref[...] = (bg * conv).astype(BF16)
    return u


def _inproj_carry_kernel(x_ref, g_ref, w_ref, cos_ref, sa_ref, sb_ref, wsc_ref,
                         q_ref, kt_ref, v4_ref, vb_ref, y_ref, tail_ref, carry_ref, *, dims):
    @pl.when(pl.program_id(1) == 0)
    def _():
        carry_ref[...] = jnp.zeros_like(carry_ref)

    def put_q(c, blk):
        q_ref[c] = blk

    def put_k(c, blk):
        kt_ref[c * LANES:(c + 1) * LANES, :] = blk.T

    def put_v(v):
        heads = v.shape[1] // LANES
        for hd in range(heads):
            vh = v[:, hd * LANES:(hd + 1) * LANES]
            vb_ref[hd] = vh.astype(BF16)
            v4_ref[pl.ds(hd, v.shape[0], stride=heads), :] = vh

    fix = lambda shape: _carry_fix(carry_ref, slice(None), shape)
    u = _inproj_body(x_ref, g_ref, w_ref, cos_ref, sa_ref, sb_ref, wsc_ref,
                     put_q, put_k, put_v, y_ref, fix, dims)
    last = u[u.shape[0] - SUBLANES:, :]
    carry_ref[...] = last
    tail_ref[0] = last


def _inproj_state_kernel(x_ref, g_ref, w_ref, cos_ref, sa_ref, sb_ref, wsc_ref,
                         sp1_ref, sp2_ref, q_ref, k_ref, v_ref, y_ref, u_ref,
                         *, dims, seq_len):
    def put_q(c, blk):
        q_ref[:, c * LANES:(c + 1) * LANES] = blk

    def put_k(c, blk):
        k_ref[:, c * LANES:(c + 1) * LANES] = blk

    def put_v(v):
        v_ref[...] = v

    fix = lambda shape: _state_fix(sp1_ref[...], sp2_ref[...], seq_len, shape)
    u_ref[...] = _inproj_body(x_ref, g_ref, w_ref, cos_ref, sa_ref, sb_ref, wsc_ref,
                              put_q, put_k, put_v, y_ref, fix, dims)


def _in_proj(x2d, g, w, tabs, w_sc, dims, *, batch, seq_len, tm, state=None):
    n, d = x2d.shape
    qc, kc, vc, cw, _ = dims
    common_in = [None, _const_spec((1, d)), _const_spec(w.shape), None, None, None,
                 _const_spec(w_sc.shape)]
    if state is None:
        nj = seq_len // tm
        heads = vc // LANES
        row = lambda b, j: (b * nj + j, 0)
        tab = pl.BlockSpec((tm, LANES), lambda b, j: (j, 0))
        in_specs = list(common_in)
        in_specs[0] = pl.BlockSpec((tm, d), row)
        in_specs[3:6] = [tab, tab, tab]
        per_head = pl.BlockSpec((None, heads, tm, LANES), lambda b, j: (b, 0, j, 0))
        out_specs = [per_head,
                     pl.BlockSpec((None, kc, tm), lambda b, j: (b, 0, j)),
                     pl.BlockSpec((tm * heads, LANES), row),
                     per_head, pl.BlockSpec((tm, cw), row),
                     pl.BlockSpec((1, SUBLANES, cw), lambda b, j: (b, 0, 0))]
        out_shape = [jax.ShapeDtypeStruct((batch, heads, seq_len, LANES), BF16),
                     jax.ShapeDtypeStruct((batch, kc, seq_len), F32),
                     jax.ShapeDtypeStruct((n * heads, LANES), F32),
                     jax.ShapeDtypeStruct((batch, heads, seq_len, LANES), BF16),
                     jax.ShapeDtypeStruct((n, cw), BF16),
                     jax.ShapeDtypeStruct((batch, SUBLANES, cw), F32)]
        return pl.pallas_call(
            functools.partial(_inproj_carry_kernel, dims=dims),
            grid=(batch, nj), in_specs=in_specs, out_specs=out_specs, out_shape=out_shape,
            scratch_shapes=[pltpu.VMEM((SUBLANES, cw), F32)],
            compiler_params=_params(2), name="in_proj_prompt",
        )(x2d, g, w, *tabs, w_sc)
    sp1, sp2 = state
    row = lambda i: (i, 0)
    tab = pl.BlockSpec((tm, LANES), row)
    in_specs = list(common_in)
    in_specs[0] = pl.BlockSpec((tm, d), row)
    in_specs[3:6] = [tab, tab, tab]
    in_specs += [pl.BlockSpec((tm, cw), row)] * 2
    out_specs = [pl.BlockSpec((tm, c), row) for c in (qc, kc, vc, cw, cw)]
    out_shape = [jax.ShapeDtypeStruct((n, qc), BF16), jax.ShapeDtypeStruct((n, kc), F32),
                 jax.ShapeDtypeStruct((n, vc), F32), jax.ShapeDtypeStruct((n, cw), BF16),
                 jax.ShapeDtypeStruct((n, cw), F32)]
    return pl.pallas_call(
        functools.partial(_inproj_state_kernel, dims=dims, seq_len=seq_len),
        grid=(n // tm,), in_specs=in_specs, out_specs=out_specs, out_shape=out_shape,
        compiler_params=_params(1), name="in_proj_sample",
    )(x2d, g, w, *tabs, w_sc, sp1, sp2)


def _diff_lambda(lq1_ref, lk1_ref, lq2_ref, lk2_ref, lam_init):
    a = jnp.sum(lq1_ref[...] * lk1_ref[...], axis=-1, keepdims=True)
    b = jnp.sum(lq2_ref[...] * lk2_ref[...], axis=-1, keepdims=True)
    return jnp.exp(a) - jnp.exp(b) + lam_init


def _stack_maps(qt):
    lane = lax.broadcasted_iota(jnp.int32, qt.shape, 1)
    first = lane < (LANES // 2)
    zero = jnp.zeros_like(qt)
    return jnp.concatenate([jnp.where(first, qt, zero), jnp.where(first, zero, qt)], axis=0)


def _finish_head(acc, l, lam, g, lam_init, tq):
    o = acc[:tq] / l[:tq] - lam * (acc[tq:] / l[tq:])
    return _rms(o, g) * (1.0 - lam_init)


def _attn_prompt_kernel(q_ref, k_ref, v_ref, lq1_ref, lk1_ref, lq2_ref, lk2_ref, g_ref,
                        o_ref, kb_ref, vt_ref, *, tq, lam_init):
    seq, hd = v_ref.shape
    kb_ref[...] = k_ref[...].T.astype(BF16)
    vt_ref[:hd, :] = v_ref[...].astype(F32).T.astype(BF16)
    vt_ref[hd:, :] = jnp.ones((vt_ref.shape[0] - hd, seq), BF16)
    lam = _diff_lambda(lq1_ref, lk1_ref, lq2_ref, lk2_ref, lam_init)
    g = g_ref[...]
    r = lax.broadcasted_iota(jnp.int32, (tq, 2 * tq), 0)
    c = lax.broadcasted_iota(jnp.int32, (tq, 2 * tq), 1)
    shift = CHUNK.bit_length() - 1
    diag_bias = jnp.where((r >> shift) <= ((c & (tq - 1)) >> shift), 0.0, NEG_INF)

    nq = seq // tq
    order = list(range(0, nq, 2)) + list(range(nq - 1 - (nq % 2), 0, -2))
    def scores(i):
        lo = i * tq
        qm = _stack_maps(q_ref[lo:lo + tq, :])
        s_d = _dot_nt(kb_ref[lo:lo + tq, :], qm) + diag_bias
        m = jnp.max(s_d, axis=0, keepdims=True)
        s_o = None
        if i:
            s_o = _dot_nt(kb_ref[:lo, :], qm)
            m = jnp.maximum(m, jnp.max(s_o, axis=0, keepdims=True))
        return s_d, s_o, m

    def finish(i, s_d, s_o, m):
        lo = i * tq
        acc = _dot(vt_ref[:, lo:lo + tq], jnp.exp2(s_d - m).astype(BF16))
        if i:
            acc = acc + _dot(vt_ref[:, :lo], jnp.exp2(s_o - m).astype(BF16))
        num, l = acc[:hd, :], acc[hd:hd + 1, :]
        ot = num[:, :tq] / l[:, :tq] - lam * (num[:, tq:] / l[:, tq:])
        ot = ot * lax.rsqrt(jnp.mean(ot * ot, axis=0, keepdims=True) + EPS) * g
        o_ref[lo:lo + tq, :] = (ot * (1.0 - lam_init)).T.astype(BF16)

    ahead = 2
    pending = [scores(i) for i in order[:ahead]]
    for n, i in enumerate(order):
        if n + ahead < nq:
            pending.append(scores(order[n + ahead]))
        finish(i, *pending.pop(0))


def _attn_prompt(q4, kt3, v4, lams, g_sub, *, tq, lam_init):
    b, heads, seq, _ = q4.shape
    blk = lambda: pl.BlockSpec((None, None, seq, LANES), lambda i, h: (i, h, 0, 0))
    lam_spec = _const_spec(lams[0].shape)
    g_col = g_sub.reshape(-1, 1)
    ones_rows = 2 * SUBLANES
    return pl.pallas_call(
        functools.partial(_attn_prompt_kernel, tq=tq, lam_init=lam_init),
        grid=(b, heads),
        in_specs=[blk(), pl.BlockSpec((None, LANES, seq), lambda i, h: (i, h, 0)), blk(),
                  lam_spec, lam_spec, lam_spec, lam_spec, _const_spec(g_col.shape)],
        out_specs=blk(),
        out_shape=jax.ShapeDtypeStruct(q4.shape, BF16),
        scratch_shapes=[pltpu.VMEM((seq, LANES), BF16),
                        pltpu.VMEM((LANES + ones_rows, seq), BF16)],
        compiler_params=_params(2), name="attn_prompt",
    )(q4, kt3, v4, *lams, g_col)


def _attn_sample_kernel(q_ref, kn_ref, vn_ref, kp_ref, vp_ref, lq1_ref, lk1_ref, lq2_ref,
                        lk2_ref, g_ref, o_ref, *, lam_init):
    t_new = q_ref.shape[1]
    heads = q_ref.shape[2] // LANES
    past = kp_ref.shape[2]
    lam = _diff_lambda(lq1_ref, lk1_ref, lq2_ref, lk2_ref, lam_init)
    g = g_ref[...]
    pad = jnp.zeros((LANES - t_new, LANES), BF16)
    lane = lax.broadcasted_iota(jnp.int32, (2 * t_new, LANES), 1)
    for bi, h in [(bi, h) for bi in range(q_ref.shape[0]) for h in range(heads)]:
        cols = slice(h * LANES, (h + 1) * LANES)
        qm = _stack_maps(q_ref[bi, :, cols])
        kn = jnp.concatenate([kn_ref[bi, :, cols].astype(BF16), pad], axis=0)
        vn = jnp.concatenate([vn_ref[bi, :, cols].astype(BF16), pad], axis=0)
        s_p = _dot(qm, kp_ref[bi, cols, :].astype(BF16))
        vp = vp_ref[bi, pl.ds(h, past, stride=heads), :].astype(BF16)
        s_n = jnp.where(lane < t_new, _dot_nt(qm, kn), NEG_INF)
        m = jnp.maximum(jnp.max(s_p, axis=-1, keepdims=True),
                        jnp.max(s_n, axis=-1, keepdims=True))
        e_p = jnp.exp2(s_p - m)
        e_n = jnp.exp2(s_n - m)
        l = jnp.sum(e_p, axis=-1, keepdims=True) + jnp.sum(e_n, axis=-1, keepdims=True)
        acc = _dot(e_p.astype(BF16), vp) + _dot(e_n.astype(BF16), vn)
        o_ref[bi, :, cols] = _finish_head(acc, l, lam, g, lam_init, t_new).astype(BF16)


def _attn_sample(q3, kn3, vn3, kpt3, vp4, lams, g_sub, *, lam_init, nb):
    b, t_new, width = q3.shape
    new = lambda: pl.BlockSpec((nb, t_new, width), lambda i: (i, 0, 0))
    whole = lambda a: pl.BlockSpec((nb,) + a.shape[1:], lambda i: (i, 0, 0))
    lam_spec = _const_spec(lams[0].shape)
    return pl.pallas_call(
        functools.partial(_attn_sample_kernel, lam_init=lam_init),
        grid=(b // nb,),
        in_specs=[new(), new(), new(), whole(kpt3), whole(vp4), lam_spec, lam_spec, lam_spec,
                  lam_spec, _const_spec(g_sub.shape)],
        out_specs=new(),
        out_shape=jax.ShapeDtypeStruct((b, t_new, width), BF16),
        compiler_params=_params(1), name="attn_sample",
    )(q3, kn3, vn3, kpt3, vp4, *lams, g_sub)


def _mix_and_query(x_ref, o_ref, y_ref, wout_ref, gx_ref, wxq_ref, heads):
    if len(o_ref.shape) == 3:
        o = jnp.concatenate([o_ref[h] for h in range(o_ref.shape[0])], axis=1)
    else:
        o = o_ref[...]
    aw = o.shape[-1]
    x1 = x_ref[...] + _dot(o, wout_ref[:aw, :]) + _dot(y_ref[...], wout_ref[aw:, :])
    hd = x1.shape[-1] // heads
    scale = hd ** -0.5 * math.log2(math.e)
    hq = (_dot(_rms(x1, gx_ref[...]).astype(BF16), wxq_ref[...]) * scale).astype(BF16)
    return x1, hq


def _cross_attend(hq, mk_ref, mv_ref, bi, heads, put):
    hd = hq.shape[-1] // heads
    for h in range(heads):
        cols = slice(h * hd, (h + 1) * hd)
        s = _dot_nt(hq[:, cols], _load_mem_head(mk_ref, bi, h, heads, hd))
        e = jnp.exp2(s - jnp.max(s, axis=-1, keepdims=True))
        l = jnp.sum(e, axis=-1, keepdims=True)
        xo = _dot(e.astype(BF16), _load_mem_head(mv_ref, bi, h, heads, hd)) / l
        put(cols, xo.astype(BF16))


def _post_kernel(x_ref, o_ref, y_ref, wout_ref, gx_ref, wxq_ref, mk_ref, mv_ref, wxo_ref,
                 out_ref, xo_ref, *, heads):
    x1, hq = _mix_and_query(x_ref, o_ref, y_ref, wout_ref, gx_ref, wxq_ref, heads)

    def put(cols, xo):
        xo_ref[:, cols] = xo

    _cross_attend(hq, mk_ref, mv_ref, 0, heads, put)
    out_ref[...] = x1 + _dot(xo_ref[...], wxo_ref[...])


def _post_rows_kernel(x_ref, o_ref, y_ref, wout_ref, gx_ref, wxq_ref, mk_ref, mv_ref, wxo_ref,
                      out_ref, x1_ref, hq_ref, xo_ref, *, heads, seq_len):
    b = pl.program_id(0)
    nb = mk_ref.shape[0]

    @pl.when(b == 0)
    def _():
        x1_ref[...], hq_ref[...] = _mix_and_query(x_ref, o_ref, y_ref, wout_ref, gx_ref,
                                                  wxq_ref, heads)

    for bi in range(nb):
        rows = pl.ds(pl.multiple_of((b * nb + bi) * seq_len, seq_len), seq_len)

        def put(cols, xo, rows=rows):
            xo_ref[rows, cols] = xo

        _cross_attend(hq_ref[rows, :], mk_ref, mv_ref, bi, heads, put)

    @pl.when(b == pl.num_programs(0) - 1)
    def _():
        out_ref[...] = x1_ref[...] + _dot(xo_ref[...], wxo_ref[...])


def _post_rows(x2d, o2d, y2d, w_out, g_x, w_xq, mk3, mv3, w_xo, *, batch, seq_len, heads, nb):
    n, d = x2d.shape
    mem = pl.BlockSpec((nb,) + mk3.shape[1:], lambda b: (b, 0, 0))
    return pl.pallas_call(
        functools.partial(_post_rows_kernel, heads=heads, seq_len=seq_len),
        grid=(batch // nb,),
        in_specs=[_const_spec(x2d.shape), _const_spec(o2d.shape), _const_spec(y2d.shape),
                  _const_spec(w_out.shape), _const_spec((1, d)), _const_spec(w_xq.shape),
                  mem, mem, _const_spec(w_xo.shape)],
        out_specs=pl.BlockSpec((n, d), lambda b: (0, 0)),
        out_shape=jax.ShapeDtypeStruct((n, d), F32),
        scratch_shapes=[pltpu.VMEM((n, d), F32), pltpu.VMEM((n, d), BF16),
                        pltpu.VMEM((n, d), BF16)],
        compiler_params=_params(1), name="post_sample",
    )(x2d, o2d, y2d, w_out, g_x, w_xq, mk3, mv3, w_xo)


def _post(x2d, o4, y2d, w_out, g_x, w_xq, mk3, mv3, w_xo, *, batch, seq_len, tm, heads):
    n, d = x2d.shape
    cw = y2d.shape[-1]
    nj = seq_len // tm
    row = lambda b, j: (b * nj + j, 0)
    mem = pl.BlockSpec((1,) + mk3.shape[1:], lambda b, j: (b, 0, 0))
    o_spec = pl.BlockSpec((None, o4.shape[1], tm, LANES), lambda b, j: (b, 0, j, 0))
    return pl.pallas_call(
        functools.partial(_post_kernel, heads=heads),
        grid=(batch, nj),
        in_specs=[pl.BlockSpec((tm, d), row), o_spec,
                  pl.BlockSpec((tm, cw), row), _const_spec(w_out.shape), _const_spec((1, d)),
                  _const_spec(w_xq.shape), mem, mem, _const_spec(w_xo.shape)],
        out_specs=pl.BlockSpec((tm, d), row),
        out_shape=jax.ShapeDtypeStruct((n, d), F32),
        scratch_shapes=[pltpu.VMEM((tm, d), BF16)],
        compiler_params=_params(2), name="post",
    )(x2d, o4, y2d, w_out, g_x, w_xq, mk3, mv3, w_xo)


FF_CHUNK = 256


def _ffn_body(x_ref, g_ref, wup_ref, wgate_ref, wconv_ref, wdown_ref, gfin_ref, y_ref,
              a_ref, fix, emit_up, final_norm):
    x = x_ref[...]
    hf = _rms(x, g_ref[...]).astype(BF16)
    dff = wup_ref.shape[-1]
    for c in range(dff // FF_CHUNK):
        cols = slice(c * FF_CHUNK, (c + 1) * FF_CHUNK)
        up = _dot(hf, wup_ref[:, cols])
        emit_up(cols, up)
        uc = _conv3(up, wconv_ref[:, cols], fix(cols, up.shape))
        gate = _dot(hf, wgate_ref[:, cols])
        a_ref[:, cols] = (uc * jax.nn.sigmoid(uc) * gate).astype(BF16)
    half = x.shape[0] // 2
    for r0 in (0, half):
        x3 = x[r0:r0 + half] + _dot(a_ref[r0:r0 + half, :], wdown_ref[...])
        y_ref[r0:r0 + half, :] = _rms(x3, gfin_ref[...]) if final_norm else x3


def _ffn_carry_kernel(x_ref, g_ref, wup_ref, wgate_ref, wconv_ref, wdown_ref, gfin_ref,
                      y_ref, tail_ref, a_ref, carry_ref, new_ref, *, final_norm):
    @pl.when(pl.program_id(1) == 0)
    def _():
        carry_ref[...] = jnp.zeros_like(carry_ref)

    def emit_up(cols, up):
        new_ref[:, cols] = up[up.shape[0] - SUBLANES:, :]

    fix = lambda cols, shape: _carry_fix(carry_ref, cols, shape)
    _ffn_body(x_ref, g_ref, wup_ref, wgate_ref, wconv_ref, wdown_ref, gfin_ref, y_ref,
              a_ref, fix, emit_up, final_norm)
    carry_ref[...] = new_ref[...]
    tail_ref[0] = new_ref[...]


def _ffn_state_kernel(x_ref, g_ref, wup_ref, wgate_ref, wconv_ref, wdown_ref, gfin_ref,
                      sp1_ref, sp2_ref, y_ref, up_ref, a_ref, *, seq_len, final_norm):
    def emit_up(cols, up):
        up_ref[:, cols] = up

    fix = lambda cols, shape: _state_fix(sp1_ref[:, cols], sp2_ref[:, cols], seq_len, shape)
    _ffn_body(x_ref, g_ref, wup_ref, wgate_ref, wconv_ref, wdown_ref, gfin_ref, y_ref,
              a_ref, fix, emit_up, final_norm)


def _ffn(x2d, g_ffn, w_up, w_gate, w_conv, w_down, g_final, *, batch, seq_len, tm, final_norm,
         state=None):
    n, d = x2d.shape
    dff = w_up.shape[-1]
    weights = [_const_spec((1, d)), _const_spec(w_up.shape), _const_spec(w_gate.shape),
               _const_spec(w_conv.shape), _const_spec(w_down.shape), _const_spec((1, d))]
    if state is None:
        nj = seq_len // tm
        row = lambda b, j: (b * nj + j, 0)
        return pl.pallas_call(
            functools.partial(_ffn_carry_kernel, final_norm=final_norm),
            grid=(batch, nj),
            in_specs=[pl.BlockSpec((tm, d), row)] + weights,
            out_specs=[pl.BlockSpec((tm, d), row),
                       pl.BlockSpec((1, SUBLANES, dff), lambda b, j: (b, 0, 0))],
            out_shape=[jax.ShapeDtypeStruct((n, d), F32),
                       jax.ShapeDtypeStruct((batch, SUBLANES, dff), F32)],
            scratch_shapes=[pltpu.VMEM((tm, dff), BF16), pltpu.VMEM((SUBLANES, dff), F32),
                            pltpu.VMEM((SUBLANES, dff), F32)],
            compiler_params=_params(2), name="ffn_prompt",
        )(x2d, g_ffn, w_up, w_gate, w_conv, w_down, g_final)
    sp1, sp2 = state
    row = lambda i: (i, 0)
    return pl.pallas_call(
        functools.partial(_ffn_state_kernel, seq_len=seq_len, final_norm=final_norm),
        grid=(n // tm,),
        in_specs=[pl.BlockSpec((tm, d), row)] + weights + [pl.BlockSpec((tm, dff), row)] * 2,
        out_specs=[pl.BlockSpec((tm, d), row), pl.BlockSpec((tm, dff), row)],
        out_shape=[jax.ShapeDtypeStruct((n, d), F32), jax.ShapeDtypeStruct((n, dff), F32)],
        scratch_shapes=[pltpu.VMEM((tm, dff), BF16)],
        compiler_params=_params(1), name="ffn_sample",
    )(x2d, g_ffn, w_up, w_gate, w_conv, w_down, g_final, sp1, sp2)


def _rope_tables(pos, qk_head_dim):
    rot = qk_head_dim // 4
    half = rot // 2
    inv = 1.0 / (ROPE_THETA ** (jnp.arange(half, dtype=F32) * 2.0 / rot))
    ang = pos.astype(F32)[:, None] * inv[None, :]
    cos, sin = jnp.cos(ang), jnp.sin(ang)
    t = pos.shape[0]
    zeros = lambda w: jnp.zeros((t, w), F32)
    c = jnp.concatenate([cos, cos, jnp.ones((t, qk_head_dim - rot), F32)], axis=1)
    a = jnp.concatenate([-sin, zeros(qk_head_dim - half)], axis=1)
    b = jnp.concatenate([zeros(half), sin, zeros(qk_head_dim - rot)], axis=1)
    rep = LANES // qk_head_dim
    return tuple(jnp.tile(m, (1, rep)) for m in (c, a, b))


def _expand_state(state, seq_len):
    b, _, c = state.shape
    z = jnp.zeros((b, seq_len, c), state.dtype)
    sp1 = z.at[:, 0].set(state[:, 1])
    sp2 = z.at[:, 0].set(state[:, 0]).at[:, 1].set(state[:, 1])
    return sp1.reshape(b * seq_len, c), sp2.reshape(b * seq_len, c)


def kernel(x_prompt, x_sample, cache_attn_k, cache_attn_v, state_short_conv, state_ffn_conv,
           cache_mem_k, cache_mem_v, mem_prompt, g_mix, w_in, lam_q1, lam_k1, lam_q2, lam_k2,
           g_sub, w_sc, w_out, g_mem, g_x, w_xq, w_xk, w_xv, w_xo, g_ffn, w_up, w_gate,
           w_ffconv, w_down, g_final):
    depth = w_in.shape[0]
    bp, seq, d = x_prompt.shape
    bs, t_new, _ = x_sample.shape
    past = cache_attn_k.shape[2]
    heads_a, qk_dim = cache_attn_k.shape[3], cache_attn_k.shape[5]
    v_dim = cache_attn_v.shape[4]
    cw = state_short_conv.shape[-1]
    dff = state_ffn_conv.shape[-1]
    n_mem, heads_x, x_dim = cache_mem_k.shape[2:]
    qc = kc = heads_a * 2 * qk_dim
    vc = heads_a * v_dim
    dims = (qc, kc, vc, cw, qk_dim ** -0.5 * math.log2(math.e))
    assert 2 * qk_dim == LANES and v_dim == LANES and qk_dim == CHUNK

    tabs_p = _rope_tables(jnp.arange(seq, dtype=jnp.int32), qk_dim)
    tabs_s = tuple(jnp.tile(m, (bs, 1)) for m in
                   _rope_tables(past + jnp.arange(t_new, dtype=jnp.int32), qk_dim))
    row = lambda v: v.reshape(1, -1)

    hp = x_prompt.reshape(bp * seq, d)
    hs = x_sample.reshape(bs * t_new, d)
    outs_p = [[] for _ in range(6)]
    outs_s = [[] for _ in range(4)]
    for l in range(depth):
        lam_init = _lambda_init(l)
        wi, wo = w_in[l].astype(BF16), w_out[l].astype(BF16)
        wq, wxo_b = w_xq[l].astype(BF16), w_xo[l].astype(BF16)
        wkv = jnp.concatenate([w_xk[l], w_xv[l]], axis=1).astype(BF16)
        wu, wg, wd = w_up[l].astype(BF16), w_gate[l].astype(BF16), w_down[l].astype(BF16)
        lams = tuple(row(v[l]) for v in (lam_q1, lam_k1, lam_q2, lam_k2))
        gsub = row(g_sub[l])

        mk8, mv8, mk, mv = _mem_kv(mem_prompt.reshape(bp * n_mem, d), row(g_mem[l]), wkv,
                                   MEM_TILE, heads_x)
        q, kt, v4, vb, ysc, sc_tail = _in_proj(hp, row(g_mix[l]), wi, tabs_p, w_sc[l], dims,
                                               batch=bp, seq_len=seq, tm=PROJ_TILE)
        o = _attn_prompt(q, kt, vb, lams, gsub, tq=Q_TILE, lam_init=lam_init)
        x2 = _post(hp, o, ysc, wo, row(g_x[l]), wq,
                   mk.reshape(bp, n_mem, d), mv.reshape(bp, n_mem, d), wxo_b,
                   batch=bp, seq_len=seq, tm=POST_TILE, heads=heads_x)
        hp, ff_tail = _ffn(x2, row(g_ffn[l]), wu, wg, w_ffconv[l], wd, row(g_final),
                           batch=bp, seq_len=seq, tm=FFN_TILE, final_norm=l == depth - 1)
        outs_p[0].append(jnp.transpose(kt.reshape(bp, heads_a, 2, qk_dim, seq), (0, 4, 1, 2, 3)))
        outs_p[1].append(v4.reshape(bp, seq, heads_a, v_dim))
        outs_p[2].append(sc_tail[:, SUBLANES - 2:])
        outs_p[3].append(ff_tail[:, SUBLANES - 2:])
        outs_p[4].append(_from_mem_layout(mk8, bp, n_mem, heads_x, x_dim))
        outs_p[5].append(_from_mem_layout(mv8, bp, n_mem, heads_x, x_dim))

        n_s = bs * t_new
        qs, ks, vs, yscs, u_s = _in_proj(
            hs, row(g_mix[l]), wi, tabs_s, w_sc[l], dims, batch=bs, seq_len=t_new, tm=n_s,
            state=_expand_state(state_short_conv[l], t_new))
        o_s = _attn_sample(qs.reshape(bs, t_new, qc), ks.reshape(bs, t_new, kc),
                           vs.reshape(bs, t_new, vc),
                           jnp.transpose(cache_attn_k[l].reshape(bs, past, kc), (0, 2, 1)),
                           cache_attn_v[l].reshape(bs, past * heads_a, v_dim), lams, gsub,
                           lam_init=lam_init, nb=SAMPLE_ATTN_SEQS)
        x2s = _post_rows(hs, o_s.reshape(n_s, vc), yscs, wo, row(g_x[l]), wq,
                         _to_mem_layout(cache_mem_k[l], x_dim // LANES),
                         _to_mem_layout(cache_mem_v[l], x_dim // LANES), wxo_b,
                         batch=bs, seq_len=t_new, heads=heads_x, nb=SAMPLE_POST_SEQS)
        hs, up_s = _ffn(x2s, row(g_ffn[l]), wu, wg, w_ffconv[l], wd, row(g_final),
                        batch=bs, seq_len=t_new, tm=n_s, final_norm=l == depth - 1,
                        state=_expand_state(state_ffn_conv[l], t_new))
        outs_s[0].append(ks.reshape(bs, t_new, heads_a, 2, qk_dim))
        outs_s[1].append(vs.reshape(bs, t_new, heads_a, v_dim))
        outs_s[2].append(u_s.reshape(bs, t_new, cw)[:, t_new - 2:])
        outs_s[3].append(up_s.reshape(bs, t_new, dff)[:, t_new - 2:])

    return (hp.reshape(bp, seq, d), hs.reshape(bs, t_new, d),
            *(jnp.stack(o) for o in outs_p), *(jnp.stack(o) for o in outs_s))
```

```python
import functools
import math

import jax
import jax.numpy as jnp
from jax import lax
from jax.experimental import pallas as pl
from jax.experimental.pallas import tpu as pltpu

EPS = 1e-6
CHUNK = 64
ROPE_THETA = 500000.0
LANES = 128
SUBLANES = 8
VMEM_LIMIT = 56 * 1024 * 1024
MEM_TILE = 1024
PROJ_TILE = 1024
POST_TILE = 1024
FFN_TILE = 1024
Q_TILE = 128
SAMPLE_ATTN_SEQS = 2
SAMPLE_POST_SEQS = 4
BF16 = jnp.bfloat16
F32 = jnp.float32
NEG_INF = float("-inf")


def _lambda_init(layer_idx):
    return 0.8 - 0.6 * math.exp(-0.3 * layer_idx)


def _rms(x, g):
    return x * lax.rsqrt(jnp.mean(x * x, axis=-1, keepdims=True) + EPS) * g


def _dot(a, b):
    return jnp.dot(a, b, preferred_element_type=F32)


def _dot_nt(a, b):
    return lax.dot_general(a, b, (((1,), (1,)), ((), ())), preferred_element_type=F32)


def _params(n_grid):
    return pltpu.CompilerParams(dimension_semantics=("arbitrary",) * n_grid,
                                vmem_limit_bytes=VMEM_LIMIT)


def _const_spec(shape):
    nd = len(shape)
    return pl.BlockSpec(shape, lambda *_: (0,) * nd, pipeline_mode=pl.Buffered(1))


def _conv3(u, w, fix):
    p1 = pltpu.roll(u, 1, axis=0)
    p2 = pltpu.roll(u, 2, axis=0)
    p1, p2 = fix(p1, p2)
    return w[0:1, :] * p2 + w[1:2, :] * p1 + w[2:3, :] * u


def _carry_fix(carry_ref, cols, shape):
    row = lax.broadcasted_iota(jnp.int32, shape, 0)
    c0 = carry_ref[SUBLANES - 2:SUBLANES - 1, cols]
    c1 = carry_ref[SUBLANES - 1:SUBLANES, cols]

    def fix(p1, p2):
        p1 = jnp.where(row == 0, c1, p1)
        p2 = jnp.where(row == 0, c0, jnp.where(row == 1, c1, p2))
        return p1, p2
    return fix


def _state_fix(sp1, sp2, seq_len, shape):
    row = lax.broadcasted_iota(jnp.int32, shape, 0)
    t = lax.rem(row, seq_len)

    def fix(p1, p2):
        return jnp.where(t == 0, sp1, p1), jnp.where(t < 2, sp2, p2)
    return fix


def _store_mem_layout(ref, blk, heads):
    rows, width = blk.shape
    lt = width // heads // LANES
    for h in range(heads):
        for t in range(lt):
            c0 = (h * lt + t) * LANES
            ref[pl.ds(t * heads + h, rows, stride=heads * lt), :] = blk[:, c0:c0 + LANES]


def _load_mem_head(ref, bi, h, heads, hd):
    if ref.shape[-1] != LANES:
        return ref[bi, :, h * hd:(h + 1) * hd]
    lt = hd // LANES
    rows = ref.shape[1] // (heads * lt)
    parts = [ref[bi, pl.ds(t * heads + h, rows, stride=heads * lt), :] for t in range(lt)]
    return jnp.concatenate(parts, axis=1).astype(BF16)


def _to_mem_layout(x, lt):
    b, n, heads, hd = x.shape
    x = jnp.transpose(x.reshape(b, n, heads, lt, LANES), (0, 1, 3, 2, 4))
    return x.reshape(b, n * heads * lt, LANES)


def _from_mem_layout(x8, b, n, heads, hd):
    lt = hd // LANES
    x = jnp.transpose(x8.reshape(b, n, lt, heads, LANES), (0, 1, 3, 2, 4))
    return x.reshape(b, n, heads, hd)


def _memkv_kernel(m_ref, g_ref, w_ref, k8_ref, v8_ref, kb_ref, vb_ref, *, heads):
    d = kb_ref.shape[-1]
    h = _rms(m_ref[...], g_ref[...]).astype(BF16)
    kv = _dot(h, w_ref[...])
    for part, (o8, ob) in enumerate(((k8_ref, kb_ref), (v8_ref, vb_ref))):
        blk = kv[:, part * d:(part + 1) * d]
        ob[...] = blk.astype(BF16)
        _store_mem_layout(o8, blk, heads)


def _mem_kv(mem2d, g_mem, w_xkv, tm, heads):
    n, d = mem2d.shape
    group = d // LANES
    row = lambda i: (i, 0)
    return pl.pallas_call(
        functools.partial(_memkv_kernel, heads=heads),
        grid=(n // tm,),
        in_specs=[pl.BlockSpec((tm, d), row), _const_spec((1, d)), _const_spec(w_xkv.shape)],
        out_specs=[pl.BlockSpec((tm * group, LANES), row)] * 2 + [pl.BlockSpec((tm, d), row)] * 2,
        out_shape=[jax.ShapeDtypeStruct((n * group, LANES), F32)] * 2
        + [jax.ShapeDtypeStruct((n, d), BF16)] * 2,
        compiler_params=_params(1),
        name="mem_kv",
    )(mem2d, g_mem, w_xkv)


def _inproj_body(x_ref, g_ref, w_ref, cos_ref, sa_ref, sb_ref, wsc_ref,
                 put_q, put_k, put_v, y_ref, fix, dims):
    qc, kc, vc, cw, q_scale = dims
    h = _rms(x_ref[...], g_ref[...]).astype(BF16)
    cos, sa, sb = cos_ref[...], sa_ref[...], sb_ref[...]

    def rope(t):
        return (t * cos + pltpu.roll(t, LANES - 8, axis=1) * sa
                + pltpu.roll(t, 8, axis=1) * sb)

    o = qc + kc
    cx = _dot(h, w_ref[:, o + vc + cw:])
    u = cx[:, :cw] * cx[:, cw:]
    conv = _conv3(u, wsc_ref[...], fix(u.shape))
    qk = _dot(h, w_ref[:, :qc + kc])
    for c in range(qc // LANES):
        blk = rope(qk[:, c * LANES:(c + 1) * LANES])
        put_q(c, (blk * q_scale).astype(BF16))
    for c in range(kc // LANES):
        lo = qc + c * LANES
        put_k(c, rope(qk[:, lo:lo + LANES]))
    put_v(_dot(h, w_ref[:, o:o + vc]))
    bg = _dot(h, w_ref[:, o + vc:o + vc + cw])
    y_ref[...] = (bg * conv).astype(BF16)
    return u


def _inproj_carry_kernel(x_ref, g_ref, w_ref, cos_ref, sa_ref, sb_ref, wsc_ref,
                         q_ref, kt_ref, v4_ref, vb_ref, y_ref, tail_ref, carry_ref, *, dims):
    @pl.when(pl.program_id(1) == 0)
    def _():
        carry_ref[...] = jnp.zeros_like(carry_ref)

    def put_q(c, blk):
        q_ref[c] = blk

    def put_k(c, blk):
        kt_ref[c * LANES:(c + 1) * LANES, :] = blk.T

    def put_v(v):
        heads = v.shape[1] // LANES
        for hd in range(heads):
            vh = v[:, hd * LANES:(hd + 1) * LANES]
            vb_ref[hd] = vh.astype(BF16)
            v4_ref[pl.ds(hd, v.shape[0], stride=heads), :] = vh

    fix = lambda shape: _carry_fix(carry_ref, slice(None), shape)
    u = _inproj_body(x_ref, g_ref, w_ref, cos_ref, sa_ref, sb_ref, wsc_ref,
                     put_q, put_k, put_v, y_ref, fix, dims)
    last = u[u.shape[0] - SUBLANES:, :]
    carry_ref[...] = last
    tail_ref[0] = last


def _inproj_state_kernel(x_ref, g_ref, w_ref, cos_ref, sa_ref, sb_ref, wsc_ref,
                         sp1_ref, sp2_ref, q_ref, k_ref, v_ref, y_ref, u_ref,
                         *, dims, seq_len):
    def put_q(c, blk):
        q_ref[:, c * LANES:(c + 1) * LANES] = blk

    def put_k(c, blk):
        k_ref[:, c * LANES:(c + 1) * LANES] = blk

    def put_v(v):
        v_ref[...] = v

    fix = lambda shape: _state_fix(sp1_ref[...], sp2_ref[...], seq_len, shape)
    u_ref[...] = _inproj_body(x_ref, g_ref, w_ref, cos_ref, sa_ref, sb_ref, wsc_ref,
                              put_q, put_k, put_v, y_ref, fix, dims)


def _in_proj(x2d, g, w, tabs, w_sc, dims, *, batch, seq_len, tm, state=None):
    n, d = x2d.shape
    qc, kc, vc, cw, _ = dims
    common_in = [None, _const_spec((1, d)), _const_spec(w.shape), None, None, None,
                 _const_spec(w_sc.shape)]
    if state is None:
        nj = seq_len // tm
        heads = vc // LANES
        row = lambda b, j: (b * nj + j, 0)
        tab = pl.BlockSpec((tm, LANES), lambda b, j: (j, 0))
        in_specs = list(common_in)
        in_specs[0] = pl.BlockSpec((tm, d), row)
        in_specs[3:6] = [tab, tab, tab]
        per_head = pl.BlockSpec((None, heads, tm, LANES), lambda b, j: (b, 0, j, 0))
        out_specs = [per_head,
                     pl.BlockSpec((None, kc, tm), lambda b, j: (b, 0, j)),
                     pl.BlockSpec((tm * heads, LANES), row),
                     per_head, pl.BlockSpec((tm, cw), row),
                     pl.BlockSpec((1, SUBLANES, cw), lambda b, j: (b, 0, 0))]
        out_shape = [jax.ShapeDtypeStruct((batch, heads, seq_len, LANES), BF16),
                     jax.ShapeDtypeStruct((batch, kc, seq_len), F32),
                     jax.ShapeDtypeStruct((n * heads, LANES), F32),
                     jax.ShapeDtypeStruct((batch, heads, seq_len, LANES), BF16),
                     jax.ShapeDtypeStruct((n, cw), BF16),
                     jax.ShapeDtypeStruct((batch, SUBLANES, cw), F32)]
        return pl.pallas_call(
            functools.partial(_inproj_carry_kernel, dims=dims),
            grid=(batch, nj), in_specs=in_specs, out_specs=out_specs, out_shape=out_shape,
            scratch_shapes=[pltpu.VMEM((SUBLANES, cw), F32)],
            compiler_params=_params(2), name="in_proj_prompt",
        )(x2d, g, w, *tabs, w_sc)
    sp1, sp2 = state
    row = lambda i: (i, 0)
    tab = pl.BlockSpec((tm, LANES), row)
    in_specs = list(common_in)
    in_specs[0] = pl.BlockSpec((tm, d), row)
    in_specs[3:6] = [tab, tab, tab]
    in_specs += [pl.BlockSpec((tm, cw), row)] * 2
    out_specs = [pl.BlockSpec((tm, c), row) for c in (qc, kc, vc, cw, cw)]
    out_shape = [jax.ShapeDtypeStruct((n, qc), BF16), jax.ShapeDtypeStruct((n, kc), F32),
                 jax.ShapeDtypeStruct((n, vc), F32), jax.ShapeDtypeStruct((n, cw), BF16),
                 jax.ShapeDtypeStruct((n, cw), F32)]
    return pl.pallas_call(
        functools.partial(_inproj_state_kernel, dims=dims, seq_len=seq_len),
        grid=(n // tm,), in_specs=in_specs, out_specs=out_specs, out_shape=out_shape,
        compiler_params=_params(1), name="in_proj_sample",
    )(x2d, g, w, *tabs, w_sc, sp1, sp2)


def _diff_lambda(lq1_ref, lk1_ref, lq2_ref, lk2_ref, lam_init):
    a = jnp.sum(lq1_ref[...] * lk1_ref[...], axis=-1, keepdims=True)
    b = jnp.sum(lq2_ref[...] * lk2_ref[...], axis=-1, keepdims=True)
    return jnp.exp(a) - jnp.exp(b) + lam_init


def _stack_maps(qt):
    lane = lax.broadcasted_iota(jnp.int32, qt.shape, 1)
    first = lane < (LANES // 2)
    zero = jnp.zeros_like(qt)
    return jnp.concatenate([jnp.where(first, qt, zero), jnp.where(first, zero, qt)], axis=0)


def _finish_head(acc, l, lam, g, lam_init, tq):
    o = acc[:tq] / l[:tq] - lam * (acc[tq:] / l[tq:])
    return _rms(o, g) * (1.0 - lam_init)


def _attn_prompt_kernel(q_ref, k_ref, v_ref, lq1_ref, lk1_ref, lq2_ref, lk2_ref, g_ref,
                        o_ref, kb_ref, vt_ref, *, tq, lam_init):
    seq, hd = v_ref.shape
    kb_ref[...] = k_ref[...].T.astype(BF16)
    vt_ref[:hd, :] = v_ref[...].astype(F32).T.astype(BF16)
    vt_ref[hd:, :] = jnp.ones((vt_ref.shape[0] - hd, seq), BF16)
    lam = _diff_lambda(lq1_ref, lk1_ref, lq2_ref, lk2_ref, lam_init)
    g = g_ref[...]
    r = lax.broadcasted_iota(jnp.int32, (tq, 2 * tq), 0)
    c = lax.broadcasted_iota(jnp.int32, (tq, 2 * tq), 1)
    shift = CHUNK.bit_length() - 1
    diag_bias = jnp.where((r >> shift) <= ((c & (tq - 1)) >> shift), 0.0, NEG_INF)

    nq = seq // tq
    order = list(range(0, nq, 2)) + list(range(nq - 1 - (nq % 2), 0, -2))
    def scores(i):
        lo = i * tq
        qm = _stack_maps(q_ref[lo:lo + tq, :])
        s_d = _dot_nt(kb_ref[lo:lo + tq, :], qm) + diag_bias
        m = jnp.max(s_d, axis=0, keepdims=True)
        s_o = None
        if i:
            s_o = _dot_nt(kb_ref[:lo, :], qm)
            m = jnp.maximum(m, jnp.max(s_o, axis=0, keepdims=True))
        return s_d, s_o, m

    def finish(i, s_d, s_o, m):
        lo = i * tq
        acc = _dot(vt_ref[:, lo:lo + tq], jnp.exp2(s_d - m).astype(BF16))
        if i:
            acc = acc + _dot(vt_ref[:, :lo], jnp.exp2(s_o - m).astype(BF16))
        num, l = acc[:hd, :], acc[hd:hd + 1, :]
        ot = num[:, :tq] / l[:, :tq] - lam * (num[:, tq:] / l[:, tq:])
        ot = ot * lax.rsqrt(jnp.mean(ot * ot, axis=0, keepdims=True) + EPS) * g
        o_ref[lo:lo + tq, :] = (ot * (1.0 - lam_init)).T.astype(BF16)

    ahead = 2
    pending = [scores(i) for i in order[:ahead]]
    for n, i in enumerate(order):
        if n + ahead < nq:
            pending.append(scores(order[n + ahead]))
        finish(i, *pending.pop(0))


def _attn_prompt(q4, kt3, v4, lams, g_sub, *, tq, lam_init):
    b, heads, seq, _ = q4.shape
    blk = lambda: pl.BlockSpec((None, None, seq, LANES), lambda i, h: (i, h, 0, 0))
    lam_spec = _const_spec(lams[0].shape)
    g_col = g_sub.reshape(-1, 1)
    ones_rows = 2 * SUBLANES
    return pl.pallas_call(
        functools.partial(_attn_prompt_kernel, tq=tq, lam_init=lam_init),
        grid=(b, heads),
        in_specs=[blk(), pl.BlockSpec((None, LANES, seq), lambda i, h: (i, h, 0)), blk(),
                  lam_spec, lam_spec, lam_spec, lam_spec, _const_spec(g_col.shape)],
        out_specs=blk(),
        out_shape=jax.ShapeDtypeStruct(q4.shape, BF16),
        scratch_shapes=[pltpu.VMEM((seq, LANES), BF16),
                        pltpu.VMEM((LANES + ones_rows, seq), BF16)],
        compiler_params=_params(2), name="attn_prompt",
    )(q4, kt3, v4, *lams, g_col)


def _attn_sample_kernel(q_ref, kn_ref, vn_ref, kp_ref, vp_ref, lq1_ref, lk1_ref, lq2_ref,
                        lk2_ref, g_ref, o_ref, *, lam_init):
    t_new = q_ref.shape[1]
    heads = q_ref.shape[2] // LANES
    past = kp_ref.shape[2]
    lam = _diff_lambda(lq1_ref, lk1_ref, lq2_ref, lk2_ref, lam_init)
    g = g_ref[...]
    pad = jnp.zeros((LANES - t_new, LANES), BF16)
    lane = lax.broadcasted_iota(jnp.int32, (2 * t_new, LANES), 1)
    for bi, h in [(bi, h) for bi in range(q_ref.shape[0]) for h in range(heads)]:
        cols = slice(h * LANES, (h + 1) * LANES)
        qm = _stack_maps(q_ref[bi, :, cols])
        kn = jnp.concatenate([kn_ref[bi, :, cols].astype(BF16), pad], axis=0)
        vn = jnp.concatenate([vn_ref[bi, :, cols].astype(BF16), pad], axis=0)
        s_p = _dot(qm, kp_ref[bi, cols, :].astype(BF16))
        vp = vp_ref[bi, pl.ds(h, past, stride=heads), :].astype(BF16)
        s_n = jnp.where(lane < t_new, _dot_nt(qm, kn), NEG_INF)
        m = jnp.maximum(jnp.max(s_p, axis=-1, keepdims=True),
                        jnp.max(s_n, axis=-1, keepdims=True))
        e_p = jnp.exp2(s_p - m)
        e_n = jnp.exp2(s_n - m)
        l = jnp.sum(e_p, axis=-1, keepdims=True) + jnp.sum(e_n, axis=-1, keepdims=True)
        acc = _dot(e_p.astype(BF16), vp) + _dot(e_n.astype(BF16), vn)
        o_ref[bi, :, cols] = _finish_head(acc, l, lam, g, lam_init, t_new).astype(BF16)


def _attn_sample(q3, kn3, vn3, kpt3, vp4, lams, g_sub, *, lam_init, nb):
    b, t_new, width = q3.shape
    new = lambda: pl.BlockSpec((nb, t_new, width), lambda i: (i, 0, 0))
    whole = lambda a: pl.BlockSpec((nb,) + a.shape[1:], lambda i: (i, 0, 0))
    lam_spec = _const_spec(lams[0].shape)
    return pl.pallas_call(
        functools.partial(_attn_sample_kernel, lam_init=lam_init),
        grid=(b // nb,),
        in_specs=[new(), new(), new(), whole(kpt3), whole(vp4), lam_spec, lam_spec, lam_spec,
                  lam_spec, _const_spec(g_sub.shape)],
        out_specs=new(),
        out_shape=jax.ShapeDtypeStruct((b, t_new, width), BF16),
        compiler_params=_params(1), name="attn_sample",
    )(q3, kn3, vn3, kpt3, vp4, *lams, g_sub)


def _mix_and_query(x_ref, o_ref, y_ref, wout_ref, gx_ref, wxq_ref, heads):
    if len(o_ref.shape) == 3:
        o = jnp.concatenate([o_ref[h] for h in range(o_ref.shape[0])], axis=1)
    else:
        o = o_ref[...]
    aw = o.shape[-1]
    x1 = x_ref[...] + _dot(o, wout_ref[:aw, :]) + _dot(y_ref[...], wout_ref[aw:, :])
    hd = x1.shape[-1] // heads
    scale = hd ** -0.5 * math.log2(math.e)
    hq = (_dot(_rms(x1, gx_ref[...]).astype(BF16), wxq_ref[...]) * scale).astype(BF16)
    return x1, hq


def _cross_attend(hq, mk_ref, mv_ref, bi, heads, put):
    hd = hq.shape[-1] // heads
    for h in range(heads):
        cols = slice(h * hd, (h + 1) * hd)
        s = _dot_nt(hq[:, cols], _load_mem_head(mk_ref, bi, h, heads, hd))
        e = jnp.exp2(s - jnp.max(s, axis=-1, keepdims=True))
        l = jnp.sum(e, axis=-1, keepdims=True)
        xo = _dot(e.astype(BF16), _load_mem_head(mv_ref, bi, h, heads, hd)) / l
        put(cols, xo.astype(BF16))


def _post_kernel(x_ref, o_ref, y_ref, wout_ref, gx_ref, wxq_ref, mk_ref, mv_ref, wxo_ref,
                 out_ref, xo_ref, *, heads):
    x1, hq = _mix_and_query(x_ref, o_ref, y_ref, wout_ref, gx_ref, wxq_ref, heads)

    def put(cols, xo):
        xo_ref[:, cols] = xo

    _cross_attend(hq, mk_ref, mv_ref, 0, heads, put)
    out_ref[...] = x1 + _dot(xo_ref[...], wxo_ref[...])


def _post_rows_kernel(x_ref, o_ref, y_ref, wout_ref, gx_ref, wxq_ref, mk_ref, mv_ref, wxo_ref,
                      out_ref, x1_ref, hq_ref, xo_ref, *, heads, seq_len):
    b = pl.program_id(0)
    nb = mk_ref.shape[0]

    @pl.when(b == 0)
    def _():
        x1_ref[...], hq_ref[...] = _mix_and_query(x_ref, o_ref, y_ref, wout_ref, gx_ref,
                                                  wxq_ref, heads)

    for bi in range(nb):
        rows = pl.ds(pl.multiple_of((b * nb + bi) * seq_len, seq_len), seq_len)

        def put(cols, xo, rows=rows):
            xo_ref[rows, cols] = xo

        _cross_attend(hq_ref[rows, :], mk_ref, mv_ref, bi, heads, put)

    @pl.when(b == pl.num_programs(0) - 1)
    def _():
        out_ref[...] = x1_ref[...] + _dot(xo_ref[...], wxo_ref[...])


def _post_rows(x2d, o2d, y2d, w_out, g_x, w_xq, mk3, mv3, w_xo, *, batch, seq_len, heads, nb):
    n, d = x2d.shape
    mem = pl.BlockSpec((nb,) + mk3.shape[1:], lambda b: (b, 0, 0))
    return pl.pallas_call(
        functools.partial(_post_rows_kernel, heads=heads, seq_len=seq_len),
        grid=(batch // nb,),
        in_specs=[_const_spec(x2d.shape), _const_spec(o2d.shape), _const_spec(y2d.shape),
                  _const_spec(w_out.shape), _const_spec((1, d)), _const_spec(w_xq.shape),
                  mem, mem, _const_spec(w_xo.shape)],
        out_specs=pl.BlockSpec((n, d), lambda b: (0, 0)),
        out_shape=jax.ShapeDtypeStruct((n, d), F32),
        scratch_shapes=[pltpu.VMEM((n, d), F32), pltpu.VMEM((n, d), BF16),
                        pltpu.VMEM((n, d), BF16)],
        compiler_params=_params(1), name="post_sample",
    )(x2d, o2d, y2d, w_out, g_x, w_xq, mk3, mv3, w_xo)


def _post(x2d, o4, y2d, w_out, g_x, w_xq, mk3, mv3, w_xo, *, batch, seq_len, tm, heads):
    n, d = x2d.shape
    cw = y2d.shape[-1]
    nj = seq_len // tm
    row = lambda b, j: (b * nj + j, 0)
    mem = pl.BlockSpec((1,) + mk3.shape[1:], lambda b, j: (b, 0, 0))
    o_spec = pl.BlockSpec((None, o4.shape[1], tm, LANES), lambda b, j: (b, 0, j, 0))
    return pl.pallas_call(
        functools.partial(_post_kernel, heads=heads),
        grid=(batch, nj),
        in_specs=[pl.BlockSpec((tm, d), row), o_spec,
                  pl.BlockSpec((tm, cw), row), _const_spec(w_out.shape), _const_spec((1, d)),
                  _const_spec(w_xq.shape), mem, mem, _const_spec(w_xo.shape)],
        out_specs=pl.BlockSpec((tm, d), row),
        out_shape=jax.ShapeDtypeStruct((n, d), F32),
        scratch_shapes=[pltpu.VMEM((tm, d), BF16)],
        compiler_params=_params(2), name="post",
    )(x2d, o4, y2d, w_out, g_x, w_xq, mk3, mv3, w_xo)


FF_CHUNK = 256


def _ffn_body(x_ref, g_ref, wup_ref, wgate_ref, wconv_ref, wdown_ref, gfin_ref, y_ref,
              a_ref, fix, emit_up, final_norm):
    x = x_ref[...]
    hf = _rms(x, g_ref[...]).astype(BF16)
    dff = wup_ref.shape[-1]
    for c in range(dff // FF_CHUNK):
        cols = slice(c * FF_CHUNK, (c + 1) * FF_CHUNK)
        up = _dot(hf, wup_ref[:, cols])
        emit_up(cols, up)
        uc = _conv3(up, wconv_ref[:, cols], fix(cols, up.shape))
        gate = _dot(hf, wgate_ref[:, cols])
        a_ref[:, cols] = (uc * jax.nn.sigmoid(uc) * gate).astype(BF16)
    half = x.shape[0] // 2
    for r0 in (0, half):
        x3 = x[r0:r0 + half] + _dot(a_ref[r0:r0 + half, :], wdown_ref[...])
        y_ref[r0:r0 + half, :] = _rms(x3, gfin_ref[...]) if final_norm else x3


def _ffn_carry_kernel(x_ref, g_ref, wup_ref, wgate_ref, wconv_ref, wdown_ref, gfin_ref,
                      y_ref, tail_ref, a_ref, carry_ref, new_ref, *, final_norm):
    @pl.when(pl.program_id(1) == 0)
    def _():
        carry_ref[...] = jnp.zeros_like(carry_ref)

    def emit_up(cols, up):
        new_ref[:, cols] = up[up.shape[0] - SUBLANES:, :]

    fix = lambda cols, shape: _carry_fix(carry_ref, cols, shape)
    _ffn_body(x_ref, g_ref, wup_ref, wgate_ref, wconv_ref, wdown_ref, gfin_ref, y_ref,
              a_ref, fix, emit_up, final_norm)
    carry_ref[...] = new_ref[...]
    tail_ref[0] = new_ref[...]


def _ffn_state_kernel(x_ref, g_ref, wup_ref, wgate_ref, wconv_ref, wdown_ref, gfin_ref,
                      sp1_ref, sp2_ref, y_ref, up_ref, a_ref, *, seq_len, final_norm):
    def emit_up(cols, up):
        up_ref[:, cols] = up

    fix = lambda cols, shape: _state_fix(sp1_ref[:, cols], sp2_ref[:, cols], seq_len, shape)
    _ffn_body(x_ref, g_ref, wup_ref, wgate_ref, wconv_ref, wdown_ref, gfin_ref, y_ref,
              a_ref, fix, emit_up, final_norm)


def _ffn(x2d, g_ffn, w_up, w_gate, w_conv, w_down, g_final, *, batch, seq_len, tm, final_norm,
         state=None):
    n, d = x2d.shape
    dff = w_up.shape[-1]
    weights = [_const_spec((1, d)), _const_spec(w_up.shape), _const_spec(w_gate.shape),
               _const_spec(w_conv.shape), _const_spec(w_down.shape), _const_spec((1, d))]
    if state is None:
        nj = seq_len // tm
        row = lambda b, j: (b * nj + j, 0)
        return pl.pallas_call(
            functools.partial(_ffn_carry_kernel, final_norm=final_norm),
            grid=(batch, nj),
            in_specs=[pl.BlockSpec((tm, d), row)] + weights,
            out_specs=[pl.BlockSpec((tm, d), row),
                       pl.BlockSpec((1, SUBLANES, dff), lambda b, j: (b, 0, 0))],
            out_shape=[jax.ShapeDtypeStruct((n, d), F32),
                       jax.ShapeDtypeStruct((batch, SUBLANES, dff), F32)],
            scratch_shapes=[pltpu.VMEM((tm, dff), BF16), pltpu.VMEM((SUBLANES, dff), F32),
                            pltpu.VMEM((SUBLANES, dff), F32)],
            compiler_params=_params(2), name="ffn_prompt",
        )(x2d, g_ffn, w_up, w_gate, w_conv, w_down, g_final)
    sp1, sp2 = state
    row = lambda i: (i, 0)
    return pl.pallas_call(
        functools.partial(_ffn_state_kernel, seq_len=seq_len, final_norm=final_norm),
        grid=(n // tm,),
        in_specs=[pl.BlockSpec((tm, d), row)] + weights + [pl.BlockSpec((tm, dff), row)] * 2,
        out_specs=[pl.BlockSpec((tm, d), row), pl.BlockSpec((tm, dff), row)],
        out_shape=[jax.ShapeDtypeStruct((n, d), F32), jax.ShapeDtypeStruct((n, dff), F32)],
        scratch_shapes=[pltpu.VMEM((tm, dff), BF16)],
        compiler_params=_params(1), name="ffn_sample",
    )(x2d, g_ffn, w_up, w_gate, w_conv, w_down, g_final, sp1, sp2)


def _rope_tables(pos, qk_head_dim):
    rot = qk_head_dim // 4
    half = rot // 2
    inv = 1.0 / (ROPE_THETA ** (jnp.arange(half, dtype=F32) * 2.0 / rot))
    ang = pos.astype(F32)[:, None] * inv[None, :]
    cos, sin = jnp.cos(ang), jnp.sin(ang)
    t = pos.shape[0]
    zeros = lambda w: jnp.zeros((t, w), F32)
    c = jnp.concatenate([cos, cos, jnp.ones((t, qk_head_dim - rot), F32)], axis=1)
    a = jnp.concatenate([-sin, zeros(qk_head_dim - half)], axis=1)
    b = jnp.concatenate([zeros(half), sin, zeros(qk_head_dim - rot)], axis=1)
    rep = LANES // qk_head_dim
    return tuple(jnp.tile(m, (1, rep)) for m in (c, a, b))


def _expand_state(state, seq_len):
    b, k, c = state.shape
    sp1 = jnp.pad(state[:, k - 1:], ((0, 0), (0, seq_len - 1), (0, 0)))
    sp2 = jnp.pad(state, ((0, 0), (0, seq_len - k), (0, 0)))
    return sp1.reshape(b * seq_len, c), sp2.reshape(b * seq_len, c)


def kernel(x_prompt, x_sample, cache_attn_k, cache_attn_v, state_short_conv, state_ffn_conv,
           cache_mem_k, cache_mem_v, mem_prompt, g_mix, w_in, lam_q1, lam_k1, lam_q2, lam_k2,
           g_sub, w_sc, w_out, g_mem, g_x, w_xq, w_xk, w_xv, w_xo, g_ffn, w_up, w_gate,
           w_ffconv, w_down, g_final):
    depth = w_in.shape[0]
    bp, seq, d = x_prompt.shape
    bs, t_new, _ = x_sample.shape
    past = cache_attn_k.shape[2]
    heads_a, qk_dim = cache_attn_k.shape[3], cache_attn_k.shape[5]
    v_dim = cache_attn_v.shape[4]
    cw = state_short_conv.shape[-1]
    dff = state_ffn_conv.shape[-1]
    n_mem, heads_x, x_dim = cache_mem_k.shape[2:]
    qc = kc = heads_a * 2 * qk_dim
    vc = heads_a * v_dim
    dims = (qc, kc, vc, cw, qk_dim ** -0.5 * math.log2(math.e))
    assert 2 * qk_dim == LANES and v_dim == LANES and qk_dim == CHUNK

    tabs_p = _rope_tables(jnp.arange(seq, dtype=jnp.int32), qk_dim)
    tabs_s = tuple(jnp.tile(m, (bs, 1)) for m in
                   _rope_tables(past + jnp.arange(t_new, dtype=jnp.int32), qk_dim))
    row = lambda v: v.reshape(1, -1)

    hp = x_prompt.reshape(bp * seq, d)
    hs = x_sample.reshape(bs * t_new, d)
    outs_p = [[] for _ in range(6)]
    outs_s = [[] for _ in range(4)]
    for l in range(depth):
        lam_init = _lambda_init(l)
        wi, wo = w_in[l].astype(BF16), w_out[l].astype(BF16)
        wq, wxo_b = w_xq[l].astype(BF16), w_xo[l].astype(BF16)
        wkv = jnp.concatenate([w_xk[l], w_xv[l]], axis=1).astype(BF16)
        wu, wg, wd = w_up[l].astype(BF16), w_gate[l].astype(BF16), w_down[l].astype(BF16)
        lams = tuple(row(v[l]) for v in (lam_q1, lam_k1, lam_q2, lam_k2))
        gsub = row(g_sub[l])

        mk8, mv8, mk, mv = _mem_kv(mem_prompt.reshape(bp * n_mem, d), row(g_mem[l]), wkv,
                                   MEM_TILE, heads_x)
        q, kt, v4, vb, ysc, sc_tail = _in_proj(hp, row(g_mix[l]), wi, tabs_p, w_sc[l], dims,
                                               batch=bp, seq_len=seq, tm=PROJ_TILE)
        o = _attn_prompt(q, kt, vb, lams, gsub, tq=Q_TILE, lam_init=lam_init)
        x2 = _post(hp, o, ysc, wo, row(g_x[l]), wq,
                   mk.reshape(bp, n_mem, d), mv.reshape(bp, n_mem, d), wxo_b,
                   batch=bp, seq_len=seq, tm=POST_TILE, heads=heads_x)
        hp, ff_tail = _ffn(x2, row(g_ffn[l]), wu, wg, w_ffconv[l], wd, row(g_final),
                           batch=bp, seq_len=seq, tm=FFN_TILE, final_norm=l == depth - 1)
        outs_p[0].append(jnp.transpose(kt.reshape(bp, heads_a, 2, qk_dim, seq), (0, 4, 1, 2, 3)))
        outs_p[1].append(v4.reshape(bp, seq, heads_a, v_dim))
        outs_p[2].append(sc_tail[:, SUBLANES - 2:])
        outs_p[3].append(ff_tail[:, SUBLANES - 2:])
        outs_p[4].append(_from_mem_layout(mk8, bp, n_mem, heads_x, x_dim))
        outs_p[5].append(_from_mem_layout(mv8, bp, n_mem, heads_x, x_dim))

        n_s = bs * t_new
        qs, ks, vs, yscs, u_s = _in_proj(
            hs, row(g_mix[l]), wi, tabs_s, w_sc[l], dims, batch=bs, seq_len=t_new, tm=n_s,
            state=_expand_state(state_short_conv[l], t_new))
        o_s = _attn_sample(qs.reshape(bs, t_new, qc), ks.reshape(bs, t_new, kc),
                           vs.reshape(bs, t_new, vc),
                           jnp.transpose(cache_attn_k[l].reshape(bs, past, kc), (0, 2, 1)),
                           cache_attn_v[l].reshape(bs, past * heads_a, v_dim), lams, gsub,
                           lam_init=lam_init, nb=SAMPLE_ATTN_SEQS)
        x2s = _post_rows(hs, o_s.reshape(n_s, vc), yscs, wo, row(g_x[l]), wq,
                         _to_mem_layout(cache_mem_k[l], x_dim // LANES),
                         _to_mem_layout(cache_mem_v[l], x_dim // LANES), wxo_b,
                         batch=bs, seq_len=t_new, heads=heads_x, nb=SAMPLE_POST_SEQS)
        hs, up_s = _ffn(x2s, row(g_ffn[l]), wu, wg, w_ffconv[l], wd, row(g_final),
                        batch=bs, seq_len=t_new, tm=n_s, final_norm=l == depth - 1,
                        state=_expand_state(state_ffn_conv[l], t_new))
        outs_s[0].append(ks.reshape(bs, t_new, heads_a, 2, qk_dim))
        outs_s[1].append(vs.reshape(bs, t_new, heads_a, v_dim))
        outs_s[2].append(u_s.reshape(bs, t_new, cw)[:, t_new - 2:])
        outs_s[3].append(up_s.reshape(bs, t_new, dff)[:, t_new - 2:])

    return (hp.reshape(bp, seq, d), hs.reshape(bs, t_new, d),
            *(jnp.stack(o) for o in outs_p), *(jnp.stack(o) for o in outs_s))
```

```python
import functools
import math

import jax
import jax.numpy as jnp
from jax import lax
from jax.experimental import pallas as pl
from jax.experimental.pallas import tpu as pltpu

EPS = 1e-6
CHUNK = 64
ROPE_THETA = 500000.0
LANES = 128
SUBLANES = 8
VMEM_LIMIT = 56 * 1024 * 1024
MEM_TILE = 1024
PROJ_TILE = 1024
POST_TILE = 1024
FFN_TILE = 1024
Q_TILE = 128
ATTN_HEADS_PER_STEP = 4
SAMPLE_ATTN_SEQS = 2
SAMPLE_POST_SEQS = 4
BF16 = jnp.bfloat16
F32 = jnp.float32
NEG_INF = float("-inf")


def _lambda_init(layer_idx):
    return 0.8 - 0.6 * math.exp(-0.3 * layer_idx)


def _rms(x, g):
    return x * lax.rsqrt(jnp.mean(x * x, axis=-1, keepdims=True) + EPS) * g


def _dot(a, b):
    return jnp.dot(a, b, preferred_element_type=F32)


def _dot_nt(a, b):
    return lax.dot_general(a, b, (((1,), (1,)), ((), ())), preferred_element_type=F32)


def _params(n_grid):
    return pltpu.CompilerParams(dimension_semantics=("arbitrary",) * n_grid,
                                vmem_limit_bytes=VMEM_LIMIT)


def _const_spec(shape):
    nd = len(shape)
    return pl.BlockSpec(shape, lambda *_: (0,) * nd, pipeline_mode=pl.Buffered(1))


def _conv3(u, w, fix):
    p1 = pltpu.roll(u, 1, axis=0)
    p2 = pltpu.roll(u, 2, axis=0)
    p1, p2 = fix(p1, p2)
    return w[0:1, :] * p2 + w[1:2, :] * p1 + w[2:3, :] * u


def _carry_fix(carry_ref, cols, shape):
    row = lax.broadcasted_iota(jnp.int32, shape, 0)
    c0 = carry_ref[SUBLANES - 2:SUBLANES - 1, cols]
    c1 = carry_ref[SUBLANES - 1:SUBLANES, cols]

    def fix(p1, p2):
        p1 = jnp.where(row == 0, c1, p1)
        p2 = jnp.where(row == 0, c0, jnp.where(row == 1, c1, p2))
        return p1, p2
    return fix


def _state_fix(sp1, sp2, seq_len, shape):
    row = lax.broadcasted_iota(jnp.int32, shape, 0)
    t = lax.rem(row, seq_len)

    def fix(p1, p2):
        return jnp.where(t == 0, sp1, p1), jnp.where(t < 2, sp2, p2)
    return fix


def _store_mem_layout(ref, blk, heads):
    rows, width = blk.shape
    lt = width // heads // LANES
    for h in range(heads):
        for t in range(lt):
            c0 = (h * lt + t) * LANES
            ref[pl.ds(t * heads + h, rows, stride=heads * lt), :] = blk[:, c0:c0 + LANES]


def _load_mem_head(ref, bi, h, heads, hd):
    if ref.shape[-1] != LANES:
        return ref[bi, :, h * hd:(h + 1) * hd]
    lt = hd // LANES
    rows = ref.shape[1] // (heads * lt)
    parts = [ref[bi, pl.ds(t * heads + h, rows, stride=heads * lt), :] for t in range(lt)]
    return jnp.concatenate(parts, axis=1).astype(BF16)


def _to_mem_layout(x, lt):
    b, n, heads, hd = x.shape
    x = jnp.transpose(x.reshape(b, n, heads, lt, LANES), (0, 1, 3, 2, 4))
    return x.reshape(b, n * heads * lt, LANES)


def _from_mem_layout(x8, b, n, heads, hd):
    lt = hd // LANES
    x = jnp.transpose(x8.reshape(b, n, lt, heads, LANES), (0, 1, 3, 2, 4))
    return x.reshape(b, n, heads, hd)


def _memkv_kernel(m_ref, g_ref, w_ref, k8_ref, v8_ref, kb_ref, vb_ref, *, heads):
    d = kb_ref.shape[-1]
    h = _rms(m_ref[...], g_ref[...]).astype(BF16)
    kv = _dot(h, w_ref[...])
    for part, (o8, ob) in enumerate(((k8_ref, kb_ref), (v8_ref, vb_ref))):
        blk = kv[:, part * d:(part + 1) * d]
        ob[...] = blk.astype(BF16)
        _store_mem_layout(o8, blk, heads)


def _mem_kv(mem2d, g_mem, w_xkv, tm, heads):
    n, d = mem2d.shape
    group = d // LANES
    row = lambda i: (i, 0)
    return pl.pallas_call(
        functools.partial(_memkv_kernel, heads=heads),
        grid=(n // tm,),
        in_specs=[pl.BlockSpec((tm, d), row), _const_spec((1, d)), _const_spec(w_xkv.shape)],
        out_specs=[pl.BlockSpec((tm * group, LANES), row)] * 2 + [pl.BlockSpec((tm, d), row)] * 2,
        out_shape=[jax.ShapeDtypeStruct((n * group, LANES), F32)] * 2
        + [jax.ShapeDtypeStruct((n, d), BF16)] * 2,
        compiler_params=_params(1),
        name="mem_kv",
    )(mem2d, g_mem, w_xkv)


def _inproj_body(x_ref, g_ref, w_ref, cos_ref, sa_ref, sb_ref, wsc_ref,
                 put_q, put_k, put_v, y_ref, fix, dims):
    qc, kc, vc, cw, q_scale = dims
    h = _rms(x_ref[...], g_ref[...]).astype(BF16)
    cos, sa, sb = cos_ref[...], sa_ref[...], sb_ref[...]

    def rope(t):
        return (t * cos + pltpu.roll(t, LANES - 8, axis=1) * sa
                + pltpu.roll(t, 8, axis=1) * sb)

    o = qc + kc
    cx = _dot(h, w_ref[:, o + vc + cw:])
    u = cx[:, :cw] * cx[:, cw:]
    conv = _conv3(u, wsc_ref[...], fix(u.shape))
    qk = _dot(h, w_ref[:, :qc + kc])
    for c in range(qc // LANES):
        blk = rope(qk[:, c * LANES:(c + 1) * LANES])
        put_q(c, (blk * q_scale).astype(BF16))
    for c in range(kc // LANES):
        lo = qc + c * LANES
        put_k(c, rope(qk[:, lo:lo + LANES]))
    put_v(_dot(h, w_ref[:, o:o + vc]))
    bg = _dot(h, w_ref[:, o + vc:o + vc + cw])
    y_ref[...] = (bg * conv).astype(BF16)
    return u


def _inproj_carry_kernel(x_ref, g_ref, w_ref, cos_ref, sa_ref, sb_ref, wsc_ref,
                         q_ref, kt_ref, v4_ref, vb_ref, y_ref, tail_ref, carry_ref, *, dims):
    @pl.when(pl.program_id(1) == 0)
    def _():
        carry_ref[...] = jnp.zeros_like(carry_ref)

    def put_q(c, blk):
        q_ref[c] = blk

    def put_k(c, blk):
        kt_ref[c * LANES:(c + 1) * LANES, :] = blk.T

    def put_v(v):
        heads = v.shape[1] // LANES
        for hd in range(heads):
            vh = v[:, hd * LANES:(hd + 1) * LANES]
            vb_ref[hd] = vh.astype(BF16)
            v4_ref[pl.ds(hd, v.shape[0], stride=heads), :] = vh

    fix = lambda shape: _carry_fix(carry_ref, slice(None), shape)
    u = _inproj_body(x_ref, g_ref, w_ref, cos_ref, sa_ref, sb_ref, wsc_ref,
                     put_q, put_k, put_v, y_ref, fix, dims)
    last = u[u.shape[0] - SUBLANES:, :]
    carry_ref[...] = last
    tail_ref[0] = last


def _inproj_state_kernel(x_ref, g_ref, w_ref, cos_ref, sa_ref, sb_ref, wsc_ref,
                         sp1_ref, sp2_ref, q_ref, k_ref, v_ref, y_ref, u_ref,
                         *, dims, seq_len):
    def put_q(c, blk):
        q_ref[:, c * LANES:(c + 1) * LANES] = blk

    def put_k(c, blk):
        k_ref[:, c * LANES:(c + 1) * LANES] = blk

    def put_v(v):
        v_ref[...] = v

    fix = lambda shape: _state_fix(sp1_ref[...], sp2_ref[...], seq_len, shape)
    u_ref[...] = _inproj_body(x_ref, g_ref, w_ref, cos_ref, sa_ref, sb_ref, wsc_ref,
                              put_q, put_k, put_v, y_ref, fix, dims)


def _in_proj(x2d, g, w, tabs, w_sc, dims, *, batch, seq_len, tm, state=None):
    n, d = x2d.shape
    qc, kc, vc, cw, _ = dims
    common_in = [None, _const_spec((1, d)), _const_spec(w.shape), None, None, None,
                 _const_spec(w_sc.shape)]
    if state is None:
        nj = seq_len // tm
        heads = vc // LANES
        row = lambda b, j: (b * nj + j, 0)
        tab = pl.BlockSpec((tm, LANES), lambda b, j: (j, 0))
        in_specs = list(common_in)
        in_specs[0] = pl.BlockSpec((tm, d), row)
        in_specs[3:6] = [tab, tab, tab]
        per_head = pl.BlockSpec((None, heads, tm, LANES), lambda b, j: (b, 0, j, 0))
        out_specs = [per_head,
                     pl.BlockSpec((None, kc, tm), lambda b, j: (b, 0, j)),
                     pl.BlockSpec((tm * heads, LANES), row),
                     per_head, pl.BlockSpec((tm, cw), row),
                     pl.BlockSpec((1, SUBLANES, cw), lambda b, j: (b, 0, 0))]
        out_shape = [jax.ShapeDtypeStruct((batch, heads, seq_len, LANES), BF16),
                     jax.ShapeDtypeStruct((batch, kc, seq_len), F32),
                     jax.ShapeDtypeStruct((n * heads, LANES), F32),
                     jax.ShapeDtypeStruct((batch, heads, seq_len, LANES), BF16),
                     jax.ShapeDtypeStruct((n, cw), BF16),
                     jax.ShapeDtypeStruct((batch, SUBLANES, cw), F32)]
        return pl.pallas_call(
            functools.partial(_inproj_carry_kernel, dims=dims),
            grid=(batch, nj), in_specs=in_specs, out_specs=out_specs, out_shape=out_shape,
            scratch_shapes=[pltpu.VMEM((SUBLANES, cw), F32)],
            compiler_params=_params(2), name="in_proj_prompt",
        )(x2d, g, w, *tabs, w_sc)
    sp1, sp2 = state
    row = lambda i: (i, 0)
    tab = pl.BlockSpec((tm, LANES), row)
    in_specs = list(common_in)
    in_specs[0] = pl.BlockSpec((tm, d), row)
    in_specs[3:6] = [tab, tab, tab]
    in_specs += [pl.BlockSpec((tm, cw), row)] * 2
    out_specs = [pl.BlockSpec((tm, c), row) for c in (qc, kc, vc, cw, cw)]
    out_shape = [jax.ShapeDtypeStruct((n, qc), BF16), jax.ShapeDtypeStruct((n, kc), F32),
                 jax.ShapeDtypeStruct((n, vc), F32), jax.ShapeDtypeStruct((n, cw), BF16),
                 jax.ShapeDtypeStruct((n, cw), F32)]
    return pl.pallas_call(
        functools.partial(_inproj_state_kernel, dims=dims, seq_len=seq_len),
        grid=(n // tm,), in_specs=in_specs, out_specs=out_specs, out_shape=out_shape,
        compiler_params=_params(1), name="in_proj_sample",
    )(x2d, g, w, *tabs, w_sc, sp1, sp2)


def _diff_lambda(lq1_ref, lk1_ref, lq2_ref, lk2_ref, lam_init):
    a = jnp.sum(lq1_ref[...] * lk1_ref[...], axis=-1, keepdims=True)
    b = jnp.sum(lq2_ref[...] * lk2_ref[...], axis=-1, keepdims=True)
    return jnp.exp(a) - jnp.exp(b) + lam_init


def _stack_maps(qt):
    lane = lax.broadcasted_iota(jnp.int32, qt.shape, 1)
    first = lane < (LANES // 2)
    zero = jnp.zeros_like(qt)
    return jnp.concatenate([jnp.where(first, qt, zero), jnp.where(first, zero, qt)], axis=0)


def _finish_head(acc, l, lam, g, lam_init, tq):
    o = acc[:tq] / l[:tq] - lam * (acc[tq:] / l[tq:])
    return _rms(o, g) * (1.0 - lam_init)


def _attn_prompt_kernel(q_ref, k_ref, v_ref, lq1_ref, lk1_ref, lq2_ref, lk2_ref, g_ref,
                        o_ref, kb_ref, vt_ref, *, tq, lam_init):
    nh, seq, hd = v_ref.shape
    for h in range(nh):
        kb_ref[h] = k_ref[h * hd:(h + 1) * hd, :].T.astype(BF16)
        vt_ref[h, :hd, :] = v_ref[h].astype(F32).T.astype(BF16)
        vt_ref[h, hd:, :] = jnp.ones((vt_ref.shape[1] - hd, seq), BF16)
    lam = _diff_lambda(lq1_ref, lk1_ref, lq2_ref, lk2_ref, lam_init)
    g = g_ref[...]
    r = lax.broadcasted_iota(jnp.int32, (tq, 2 * tq), 0)
    c = lax.broadcasted_iota(jnp.int32, (tq, 2 * tq), 1)
    shift = CHUNK.bit_length() - 1
    diag_bias = jnp.where((r >> shift) <= ((c & (tq - 1)) >> shift), 0.0, NEG_INF)

    nq = seq // tq
    order = list(range(0, nq, 2)) + list(range(nq - 1 - (nq % 2), 0, -2))
    def scores(h, i):
        lo = i * tq
        qm = _stack_maps(q_ref[h, lo:lo + tq, :])
        s_d = _dot_nt(kb_ref[h, lo:lo + tq, :], qm) + diag_bias
        m = jnp.max(s_d, axis=0, keepdims=True)
        s_o = None
        if i:
            s_o = _dot_nt(kb_ref[h, :lo, :], qm)
            m = jnp.maximum(m, jnp.max(s_o, axis=0, keepdims=True))
        return s_d, s_o, m

    def finish(h, i, s_d, s_o, m):
        lo = i * tq
        acc = _dot(vt_ref[h, :, lo:lo + tq], jnp.exp2(s_d - m).astype(BF16))
        if i:
            acc = acc + _dot(vt_ref[h, :, :lo], jnp.exp2(s_o - m).astype(BF16))
        num, l = acc[:hd, :], acc[hd:hd + 1, :]
        ot = num[:, :tq] / l[:, :tq] - lam * (num[:, tq:] / l[:, tq:])
        ot = ot * lax.rsqrt(jnp.mean(ot * ot, axis=0, keepdims=True) + EPS) * g
        o_ref[h, lo:lo + tq, :] = (ot * (1.0 - lam_init)).T.astype(BF16)

    tasks = [(h, i) for h in range(nh) for i in order]
    ahead = 2
    pending = [scores(*t) for t in tasks[:ahead]]
    for n, t in enumerate(tasks):
        if n + ahead < len(tasks):
            pending.append(scores(*tasks[n + ahead]))
        finish(*t, *pending.pop(0))


def _attn_prompt(q4, kt3, v4, lams, g_sub, *, tq, lam_init):
    b, heads, seq, _ = q4.shape
    nh = ATTN_HEADS_PER_STEP
    blk = lambda: pl.BlockSpec((None, nh, seq, LANES), lambda i, h: (i, h, 0, 0))
    lam_spec = _const_spec(lams[0].shape)
    g_col = g_sub.reshape(-1, 1)
    ones_rows = 2 * SUBLANES
    return pl.pallas_call(
        functools.partial(_attn_prompt_kernel, tq=tq, lam_init=lam_init),
        grid=(b, heads // nh),
        in_specs=[blk(), pl.BlockSpec((None, nh * LANES, seq), lambda i, h: (i, h, 0)), blk(),
                  lam_spec, lam_spec, lam_spec, lam_spec, _const_spec(g_col.shape)],
        out_specs=blk(),
        out_shape=jax.ShapeDtypeStruct(q4.shape, BF16),
        scratch_shapes=[pltpu.VMEM((nh, seq, LANES), BF16),
                        pltpu.VMEM((nh, LANES + ones_rows, seq), BF16)],
        compiler_params=_params(2), name="attn_prompt",
    )(q4, kt3, v4, *lams, g_col)


def _attn_sample_kernel(q_ref, kn_ref, vn_ref, kp_ref, vp_ref, lq1_ref, lk1_ref, lq2_ref,
                        lk2_ref, g_ref, o_ref, *, lam_init):
    t_new = q_ref.shape[1]
    heads = q_ref.shape[2] // LANES
    past = kp_ref.shape[2]
    lam = _diff_lambda(lq1_ref, lk1_ref, lq2_ref, lk2_ref, lam_init)
    g = g_ref[...]
    pad = jnp.zeros((LANES - t_new, LANES), BF16)
    lane = lax.broadcasted_iota(jnp.int32, (2 * t_new, LANES), 1)
    for bi, h in [(bi, h) for bi in range(q_ref.shape[0]) for h in range(heads)]:
        cols = slice(h * LANES, (h + 1) * LANES)
        qm = _stack_maps(q_ref[bi, :, cols])
        kn = jnp.concatenate([kn_ref[bi, :, cols].astype(BF16), pad], axis=0)
        vn = jnp.concatenate([vn_ref[bi, :, cols].astype(BF16), pad], axis=0)
        s_p = _dot(qm, kp_ref[bi, cols, :].astype(BF16))
        vp = vp_ref[bi, pl.ds(h, past, stride=heads), :].astype(BF16)
        s_n = jnp.where(lane < t_new, _dot_nt(qm, kn), NEG_INF)
        m = jnp.maximum(jnp.max(s_p, axis=-1, keepdims=True),
                        jnp.max(s_n, axis=-1, keepdims=True))
        e_p = jnp.exp2(s_p - m)
        e_n = jnp.exp2(s_n - m)
        l = jnp.sum(e_p, axis=-1, keepdims=True) + jnp.sum(e_n, axis=-1, keepdims=True)
        acc = _dot(e_p.astype(BF16), vp) + _dot(e_n.astype(BF16), vn)
        o_ref[bi, :, cols] = _finish_head(acc, l, lam, g, lam_init, t_new).astype(BF16)


def _attn_sample(q3, kn3, vn3, kpt3, vp4, lams, g_sub, *, lam_init, nb):
    b, t_new, width = q3.shape
    new = lambda: pl.BlockSpec((nb, t_new, width), lambda i: (i, 0, 0))
    whole = lambda a: pl.BlockSpec((nb,) + a.shape[1:], lambda i: (i, 0, 0))
    lam_spec = _const_spec(lams[0].shape)
    return pl.pallas_call(
        functools.partial(_attn_sample_kernel, lam_init=lam_init),
        grid=(b // nb,),
        in_specs=[new(), new(), new(), whole(kpt3), whole(vp4), lam_spec, lam_spec, lam_spec,
                  lam_spec, _const_spec(g_sub.shape)],
        out_specs=new(),
        out_shape=jax.ShapeDtypeStruct((b, t_new, width), BF16),
        compiler_params=_params(1), name="attn_sample",
    )(q3, kn3, vn3, kpt3, vp4, *lams, g_sub)


def _mix_and_query(x_ref, o_ref, y_ref, wout_ref, gx_ref, wxq_ref, heads):
    if len(o_ref.shape) == 3:
        o = jnp.concatenate([o_ref[h] for h in range(o_ref.shape[0])], axis=1)
    else:
        o = o_ref[...]
    aw = o.shape[-1]
    x1 = x_ref[...] + _dot(o, wout_ref[:aw, :]) + _dot(y_ref[...], wout_ref[aw:, :])
    hd = x1.shape[-1] // heads
    scale = hd ** -0.5 * math.log2(math.e)
    hq = (_dot(_rms(x1, gx_ref[...]).astype(BF16), wxq_ref[...]) * scale).astype(BF16)
    return x1, hq


def _cross_attend(hq, mk_ref, mv_ref, bi, heads, put):
    hd = hq.shape[-1] // heads
    for h in range(heads):
        cols = slice(h * hd, (h + 1) * hd)
        s = _dot_nt(hq[:, cols], _load_mem_head(mk_ref, bi, h, heads, hd))
        e = jnp.exp2(s - jnp.max(s, axis=-1, keepdims=True))
        l = jnp.sum(e, axis=-1, keepdims=True)
        xo = _dot(e.astype(BF16), _load_mem_head(mv_ref, bi, h, heads, hd)) / l
        put(cols, xo.astype(BF16))


def _post_kernel(x_ref, o_ref, y_ref, wout_ref, gx_ref, wxq_ref, mk_ref, mv_ref, wxo_ref,
                 out_ref, xo_ref, *, heads):
    x1, hq = _mix_and_query(x_ref, o_ref, y_ref, wout_ref, gx_ref, wxq_ref, heads)

    def put(cols, xo):
        xo_ref[:, cols] = xo

    _cross_attend(hq, mk_ref, mv_ref, 0, heads, put)
    out_ref[...] = x1 + _dot(xo_ref[...], wxo_ref[...])


def _post_rows_kernel(x_ref, o_ref, y_ref, wout_ref, gx_ref, wxq_ref, mk_ref, mv_ref, wxo_ref,
                      out_ref, x1_ref, hq_ref, xo_ref, *, heads, seq_len):
    b = pl.program_id(0)
    nb = mk_ref.shape[0]

    @pl.when(b == 0)
    def _():
        x1_ref[...], hq_ref[...] = _mix_and_query(x_ref, o_ref, y_ref, wout_ref, gx_ref,
                                                  wxq_ref, heads)

    for bi in range(nb):
        rows = pl.ds(pl.multiple_of((b * nb + bi) * seq_len, seq_len), seq_len)

        def put(cols, xo, rows=rows):
            xo_ref[rows, cols] = xo

        _cross_attend(hq_ref[rows, :], mk_ref, mv_ref, bi, heads, put)

    @pl.when(b == pl.num_programs(0) - 1)
    def _():
        out_ref[...] = x1_ref[...] + _dot(xo_ref[...], wxo_ref[...])


def _post_rows(x2d, o2d, y2d, w_out, g_x, w_xq, mk3, mv3, w_xo, *, batch, seq_len, heads, nb):
    n, d = x2d.shape
    mem = pl.BlockSpec((nb,) + mk3.shape[1:], lambda b: (b, 0, 0))
    return pl.pallas_call(
        functools.partial(_post_rows_kernel, heads=heads, seq_len=seq_len),
        grid=(batch // nb,),
        in_specs=[_const_spec(x2d.shape), _const_spec(o2d.shape), _const_spec(y2d.shape),
                  _const_spec(w_out.shape), _const_spec((1, d)), _const_spec(w_xq.shape),
                  mem, mem, _const_spec(w_xo.shape)],
        out_specs=pl.BlockSpec((n, d), lambda b: (0, 0)),
        out_shape=jax.ShapeDtypeStruct((n, d), F32),
        scratch_shapes=[pltpu.VMEM((n, d), F32), pltpu.VMEM((n, d), BF16),
                        pltpu.VMEM((n, d), BF16)],
        compiler_params=_params(1), name="post_sample",
    )(x2d, o2d, y2d, w_out, g_x, w_xq, mk3, mv3, w_xo)


def _post(x2d, o4, y2d, w_out, g_x, w_xq, mk3, mv3, w_xo, *, batch, seq_len, tm, heads):
    n, d = x2d.shape
    cw = y2d.shape[-1]
    nj = seq_len // tm
    row = lambda b, j: (b * nj + j, 0)
    mem = pl.BlockSpec((1,) + mk3.shape[1:], lambda b, j: (b, 0, 0))
    o_spec = pl.BlockSpec((None, o4.shape[1], tm, LANES), lambda b, j: (b, 0, j, 0))
    return pl.pallas_call(
        functools.partial(_post_kernel, heads=heads),
        grid=(batch, nj),
        in_specs=[pl.BlockSpec((tm, d), row), o_spec,
                  pl.BlockSpec((tm, cw), row), _const_spec(w_out.shape), _const_spec((1, d)),
                  _const_spec(w_xq.shape), mem, mem, _const_spec(w_xo.shape)],
        out_specs=pl.BlockSpec((tm, d), row),
        out_shape=jax.ShapeDtypeStruct((n, d), F32),
        scratch_shapes=[pltpu.VMEM((tm, d), BF16)],
        compiler_params=_params(2), name="post",
    )(x2d, o4, y2d, w_out, g_x, w_xq, mk3, mv3, w_xo)


FF_CHUNK = 256


def _ffn_body(x_ref, g_ref, wup_ref, wgate_ref, wconv_ref, wdown_ref, gfin_ref, y_ref,
              a_ref, fix, emit_up, final_norm):
    x = x_ref[...]
    hf = _rms(x, g_ref[...]).astype(BF16)
    dff = wup_ref.shape[-1]
    for c in range(dff // FF_CHUNK):
        cols = slice(c * FF_CHUNK, (c + 1) * FF_CHUNK)
        up = _dot(hf, wup_ref[:, cols])
        emit_up(cols, up)
        uc = _conv3(up, wconv_ref[:, cols], fix(cols, up.shape))
        gate = _dot(hf, wgate_ref[:, cols])
        a_ref[:, cols] = (uc * jax.nn.sigmoid(uc) * gate).astype(BF16)
    half = x.shape[0] // 2
    for r0 in (0, half):
        x3 = x[r0:r0 + half] + _dot(a_ref[r0:r0 + half, :], wdown_ref[...])
        y_ref[r0:r0 + half, :] = _rms(x3, gfin_ref[...]) if final_norm else x3


def _ffn_carry_kernel(x_ref, g_ref, wup_ref, wgate_ref, wconv_ref, wdown_ref, gfin_ref,
                      y_ref, tail_ref, a_ref, carry_ref, new_ref, *, final_norm):
    @pl.when(pl.program_id(1) == 0)
    def _():
        carry_ref[...] = jnp.zeros_like(carry_ref)

    def emit_up(cols, up):
        new_ref[:, cols] = up[up.shape[0] - SUBLANES:, :]

    fix = lambda cols, shape: _carry_fix(carry_ref, cols, shape)
    _ffn_body(x_ref, g_ref, wup_ref, wgate_ref, wconv_ref, wdown_ref, gfin_ref, y_ref,
              a_ref, fix, emit_up, final_norm)
    carry_ref[...] = new_ref[...]
    tail_ref[0] = new_ref[...]


def _ffn_state_kernel(x_ref, g_ref, wup_ref, wgate_ref, wconv_ref, wdown_ref, gfin_ref,
                      sp1_ref, sp2_ref, y_ref, up_ref, a_ref, *, seq_len, final_norm):
    def emit_up(cols, up):
        up_ref[:, cols] = up

    fix = lambda cols, shape: _state_fix(sp1_ref[:, cols], sp2_ref[:, cols], seq_len, shape)
    _ffn_body(x_ref, g_ref, wup_ref, wgate_ref, wconv_ref, wdown_ref, gfin_ref, y_ref,
              a_ref, fix, emit_up, final_norm)


def _ffn(x2d, g_ffn, w_up, w_gate, w_conv, w_down, g_final, *, batch, seq_len, tm, final_norm,
         state=None):
    n, d = x2d.shape
    dff = w_up.shape[-1]
    weights = [_const_spec((1, d)), _const_spec(w_up.shape), _const_spec(w_gate.shape),
               _const_spec(w_conv.shape), _const_spec(w_down.shape), _const_spec((1, d))]
    if state is None:
        nj = seq_len // tm
        row = lambda b, j: (b * nj + j, 0)
        return pl.pallas_call(
            functools.partial(_ffn_carry_kernel, final_norm=final_norm),
            grid=(batch, nj),
            in_specs=[pl.BlockSpec((tm, d), row)] + weights,
            out_specs=[pl.BlockSpec((tm, d), row),
                       pl.BlockSpec((1, SUBLANES, dff), lambda b, j: (b, 0, 0))],
            out_shape=[jax.ShapeDtypeStruct((n, d), F32),
                       jax.ShapeDtypeStruct((batch, SUBLANES, dff), F32)],
            scratch_shapes=[pltpu.VMEM((tm, dff), BF16), pltpu.VMEM((SUBLANES, dff), F32),
                            pltpu.VMEM((SUBLANES, dff), F32)],
            compiler_params=_params(2), name="ffn_prompt",
        )(x2d, g_ffn, w_up, w_gate, w_conv, w_down, g_final)
    sp1, sp2 = state
    row = lambda i: (i, 0)
    return pl.pallas_call(
        functools.partial(_ffn_state_kernel, seq_len=seq_len, final_norm=final_norm),
        grid=(n // tm,),
        in_specs=[pl.BlockSpec((tm, d), row)] + weights + [pl.BlockSpec((tm, dff), row)] * 2,
        out_specs=[pl.BlockSpec((tm, d), row), pl.BlockSpec((tm, dff), row)],
        out_shape=[jax.ShapeDtypeStruct((n, d), F32), jax.ShapeDtypeStruct((n, dff), F32)],
        scratch_shapes=[pltpu.VMEM((tm, dff), BF16)],
        compiler_params=_params(1), name="ffn_sample",
    )(x2d, g_ffn, w_up, w_gate, w_conv, w_down, g_final, sp1, sp2)


def _rope_tables(pos, qk_head_dim):
    rot = qk_head_dim // 4
    half = rot // 2
    inv = 1.0 / (ROPE_THETA ** (jnp.arange(half, dtype=F32) * 2.0 / rot))
    ang = pos.astype(F32)[:, None] * inv[None, :]
    cos, sin = jnp.cos(ang), jnp.sin(ang)
    t = pos.shape[0]
    zeros = lambda w: jnp.zeros((t, w), F32)
    c = jnp.concatenate([cos, cos, jnp.ones((t, qk_head_dim - rot), F32)], axis=1)
    a = jnp.concatenate([-sin, zeros(qk_head_dim - half)], axis=1)
    b = jnp.concatenate([zeros(half), sin, zeros(qk_head_dim - rot)], axis=1)
    rep = LANES // qk_head_dim
    return tuple(jnp.tile(m, (1, rep)) for m in (c, a, b))


def _expand_state(state, seq_len):
    b, k, c = state.shape
    sp1 = jnp.pad(state[:, k - 1:], ((0, 0), (0, seq_len - 1), (0, 0)))
    sp2 = jnp.pad(state, ((0, 0), (0, seq_len - k), (0, 0)))
    return sp1.reshape(b * seq_len, c), sp2.reshape(b * seq_len, c)


def kernel(x_prompt, x_sample, cache_attn_k, cache_attn_v, state_short_conv, state_ffn_conv,
           cache_mem_k, cache_mem_v, mem_prompt, g_mix, w_in, lam_q1, lam_k1, lam_q2, lam_k2,
           g_sub, w_sc, w_out, g_mem, g_x, w_xq, w_xk, w_xv, w_xo, g_ffn, w_up, w_gate,
           w_ffconv, w_down, g_final):
    depth = w_in.shape[0]
    bp, seq, d = x_prompt.shape
    bs, t_new, _ = x_sample.shape
    past = cache_attn_k.shape[2]
    heads_a, qk_dim = cache_attn_k.shape[3], cache_attn_k.shape[5]
    v_dim = cache_attn_v.shape[4]
    cw = state_short_conv.shape[-1]
    dff = state_ffn_conv.shape[-1]
    n_mem, heads_x, x_dim = cache_mem_k.shape[2:]
    qc = kc = heads_a * 2 * qk_dim
    vc = heads_a * v_dim
    dims = (qc, kc, vc, cw, qk_dim ** -0.5 * math.log2(math.e))
    assert 2 * qk_dim == LANES and v_dim == LANES and qk_dim == CHUNK

    tabs_p = _rope_tables(jnp.arange(seq, dtype=jnp.int32), qk_dim)
    tabs_s = tuple(jnp.tile(m, (bs, 1)) for m in
                   _rope_tables(past + jnp.arange(t_new, dtype=jnp.int32), qk_dim))
    row = lambda v: v.reshape(1, -1)

    hp = x_prompt.reshape(bp * seq, d)
    hs = x_sample.reshape(bs * t_new, d)
    outs_p = [[] for _ in range(6)]
    outs_s = [[] for _ in range(4)]
    for l in range(depth):
        lam_init = _lambda_init(l)
        wi, wo = w_in[l].astype(BF16), w_out[l].astype(BF16)
        wq, wxo_b = w_xq[l].astype(BF16), w_xo[l].astype(BF16)
        wkv = jnp.concatenate([w_xk[l], w_xv[l]], axis=1).astype(BF16)
        wu, wg, wd = w_up[l].astype(BF16), w_gate[l].astype(BF16), w_down[l].astype(BF16)
        lams = tuple(row(v[l]) for v in (lam_q1, lam_k1, lam_q2, lam_k2))
        gsub = row(g_sub[l])

        mk8, mv8, mk, mv = _mem_kv(mem_prompt.reshape(bp * n_mem, d), row(g_mem[l]), wkv,
                                   MEM_TILE, heads_x)
        q, kt, v4, vb, ysc, sc_tail = _in_proj(hp, row(g_mix[l]), wi, tabs_p, w_sc[l], dims,
                                               batch=bp, seq_len=seq, tm=PROJ_TILE)
        o = _attn_prompt(q, kt, vb, lams, gsub, tq=Q_TILE, lam_init=lam_init)
        x2 = _post(hp, o, ysc, wo, row(g_x[l]), wq,
                   mk.reshape(bp, n_mem, d), mv.reshape(bp, n_mem, d), wxo_b,
                   batch=bp, seq_len=seq, tm=POST_TILE, heads=heads_x)
        hp, ff_tail = _ffn(x2, row(g_ffn[l]), wu, wg, w_ffconv[l], wd, row(g_final),
                           batch=bp, seq_len=seq, tm=FFN_TILE, final_norm=l == depth - 1)
        outs_p[0].append(jnp.transpose(kt.reshape(bp, heads_a, 2, qk_dim, seq), (0, 4, 1, 2, 3)))
        outs_p[1].append(v4.reshape(bp, seq, heads_a, v_dim))
        outs_p[2].append(sc_tail[:, SUBLANES - 2:])
        outs_p[3].append(ff_tail[:, SUBLANES - 2:])
        outs_p[4].append(_from_mem_layout(mk8, bp, n_mem, heads_x, x_dim))
        outs_p[5].append(_from_mem_layout(mv8, bp, n_mem, heads_x, x_dim))

        n_s = bs * t_new
        qs, ks, vs, yscs, u_s = _in_proj(
            hs, row(g_mix[l]), wi, tabs_s, w_sc[l], dims, batch=bs, seq_len=t_new, tm=n_s,
            state=_expand_state(state_short_conv[l], t_new))
        o_s = _attn_sample(qs.reshape(bs, t_new, qc), ks.reshape(bs, t_new, kc),
                           vs.reshape(bs, t_new, vc),
                           jnp.transpose(cache_attn_k[l].reshape(bs, past, kc), (0, 2, 1)),
                           cache_attn_v[l].reshape(bs, past * heads_a, v_dim), lams, gsub,
                           lam_init=lam_init, nb=SAMPLE_ATTN_SEQS)
        x2s = _post_rows(hs, o_s.reshape(n_s, vc), yscs, wo, row(g_x[l]), wq,
                         _to_mem_layout(cache_mem_k[l], x_dim // LANES),
                         _to_mem_layout(cache_mem_v[l], x_dim // LANES), wxo_b,
                         batch=bs, seq_len=t_new, heads=heads_x, nb=SAMPLE_POST_SEQS)
        hs, up_s = _ffn(x2s, row(g_ffn[l]), wu, wg, w_ffconv[l], wd, row(g_final),
                        batch=bs, seq_len=t_new, tm=n_s, final_norm=l == depth - 1,
                        state=_expand_state(state_ffn_conv[l], t_new))
        outs_s[0].append(ks.reshape(bs, t_new, heads_a, 2, qk_dim))
        outs_s[1].append(vs.reshape(bs, t_new, heads_a, v_dim))
        outs_s[2].append(u_s.reshape(bs, t_new, cw)[:, t_new - 2:])
        outs_s[3].append(up_s.reshape(bs, t_new, dff)[:, t_new - 2:])

    return (hp.reshape(bp, seq, d), hs.reshape(bs, t_new, d),
            *(jnp.stack(o) for o in outs_p), *(jnp.stack(o) for o in outs_s))
```

```python
import functools
import math

import jax
import jax.numpy as jnp
from jax import lax
from jax.experimental import pallas as pl
from jax.experimental.pallas import tpu as pltpu

EPS = 1e-6
CHUNK = 64
ROPE_THETA = 500000.0
LANES = 128
SUBLANES = 8
VMEM_LIMIT = 56 * 1024 * 1024
MEM_TILE = 1024
PROJ_TILE = 1024
POST_TILE = 1024
FFN_TILE = 1024
Q_TILE = 128
ATTN_HEADS_PER_STEP = 4
SAMPLE_ATTN_SEQS = 2
SAMPLE_POST_SEQS = 4
BF16 = jnp.bfloat16
F32 = jnp.float32
NEG_INF = float("-inf")


def _lambda_init(layer_idx):
    return 0.8 - 0.6 * math.exp(-0.3 * layer_idx)


def _rms(x, g):
    return x * lax.rsqrt(jnp.mean(x * x, axis=-1, keepdims=True) + EPS) * g


def _dot(a, b):
    return jnp.dot(a, b, preferred_element_type=F32)


def _dot_nt(a, b):
    return lax.dot_general(a, b, (((1,), (1,)), ((), ())), preferred_element_type=F32)


def _params(n_grid):
    return pltpu.CompilerParams(dimension_semantics=("arbitrary",) * n_grid,
                                vmem_limit_bytes=VMEM_LIMIT)


def _const_spec(shape):
    nd = len(shape)
    return pl.BlockSpec(shape, lambda *_: (0,) * nd, pipeline_mode=pl.Buffered(1))


def _conv3(u, w, fix):
    p1 = pltpu.roll(u, 1, axis=0)
    p2 = pltpu.roll(u, 2, axis=0)
    p1, p2 = fix(p1, p2)
    return w[0:1, :] * p2 + w[1:2, :] * p1 + w[2:3, :] * u


def _carry_fix(carry_ref, cols, shape):
    row = lax.broadcasted_iota(jnp.int32, shape, 0)
    c0 = carry_ref[SUBLANES - 2:SUBLANES - 1, cols]
    c1 = carry_ref[SUBLANES - 1:SUBLANES, cols]

    def fix(p1, p2):
        p1 = jnp.where(row == 0, c1, p1)
        p2 = jnp.where(row == 0, c0, jnp.where(row == 1, c1, p2))
        return p1, p2
    return fix


def _state_fix(sp1, sp2, seq_len, shape):
    row = lax.broadcasted_iota(jnp.int32, shape, 0)
    t = lax.rem(row, seq_len)

    def fix(p1, p2):
        return jnp.where(t == 0, sp1, p1), jnp.where(t < 2, sp2, p2)
    return fix


def _store_mem_layout(ref, blk, heads):
    rows, width = blk.shape
    lt = width // heads // LANES
    for h in range(heads):
        for t in range(lt):
            c0 = (h * lt + t) * LANES
            ref[pl.ds(t * heads + h, rows, stride=heads * lt), :] = blk[:, c0:c0 + LANES]


def _load_mem_head(ref, bi, h, heads, hd):
    if ref.shape[-1] != LANES:
        return ref[bi, :, h * hd:(h + 1) * hd]
    lt = hd // LANES
    rows = ref.shape[1] // (heads * lt)
    parts = [ref[bi, pl.ds(t * heads + h, rows, stride=heads * lt), :] for t in range(lt)]
    return jnp.concatenate(parts, axis=1).astype(BF16)


def _to_mem_layout(x, lt):
    b, n, heads, hd = x.shape
    x = jnp.transpose(x.reshape(b, n, heads, lt, LANES), (0, 1, 3, 2, 4))
    return x.reshape(b, n * heads * lt, LANES)


def _from_mem_layout(x8, b, n, heads, hd):
    lt = hd // LANES
    x = jnp.transpose(x8.reshape(b, n, lt, heads, LANES), (0, 1, 3, 2, 4))
    return x.reshape(b, n, heads, hd)


def _memkv_kernel(m_ref, g_ref, w_ref, k8_ref, v8_ref, kb_ref, vb_ref, *, heads):
    d = kb_ref.shape[-1]
    h = _rms(m_ref[...], g_ref[...]).astype(BF16)
    kv = _dot(h, w_ref[...])
    for part, (o8, ob) in enumerate(((k8_ref, kb_ref), (v8_ref, vb_ref))):
        blk = kv[:, part * d:(part + 1) * d]
        ob[...] = blk.astype(BF16)
        _store_mem_layout(o8, blk, heads)


def _mem_kv(mem2d, g_mem, w_xkv, tm, heads):
    n, d = mem2d.shape
    group = d // LANES
    row = lambda i: (i, 0)
    return pl.pallas_call(
        functools.partial(_memkv_kernel, heads=heads),
        grid=(n // tm,),
        in_specs=[pl.BlockSpec((tm, d), row), _const_spec((1, d)), _const_spec(w_xkv.shape)],
        out_specs=[pl.BlockSpec((tm * group, LANES), row)] * 2 + [pl.BlockSpec((tm, d), row)] * 2,
        out_shape=[jax.ShapeDtypeStruct((n * group, LANES), F32)] * 2
        + [jax.ShapeDtypeStruct((n, d), BF16)] * 2,
        compiler_params=_params(1),
        name="mem_kv",
    )(mem2d, g_mem, w_xkv)


def _inproj_body(x_ref, g_ref, w_ref, cos_ref, sa_ref, sb_ref, wsc_ref,
                 put_q, put_k, put_v, y_ref, fix, dims):
    qc, kc, vc, cw, q_scale = dims
    h = _rms(x_ref[...], g_ref[...]).astype(BF16)
    cos, sa, sb = cos_ref[...], sa_ref[...], sb_ref[...]

    def rope(t):
        return (t * cos + pltpu.roll(t, LANES - 8, axis=1) * sa
                + pltpu.roll(t, 8, axis=1) * sb)

    o = qc + kc
    cx = _dot(h, w_ref[:, o + vc + cw:])
    u = cx[:, :cw] * cx[:, cw:]
    conv = _conv3(u, wsc_ref[...], fix(u.shape))
    qk = _dot(h, w_ref[:, :qc + kc])
    for c in range(qc // LANES):
        blk = rope(qk[:, c * LANES:(c + 1) * LANES])
        put_q(c, (blk * q_scale).astype(BF16))
    for c in range(kc // LANES):
        lo = qc + c * LANES
        put_k(c, rope(qk[:, lo:lo + LANES]))
    put_v(_dot(h, w_ref[:, o:o + vc]))
    bg = _dot(h, w_ref[:, o + vc:o + vc + cw])
    y_ref[...] = (bg * conv).astype(BF16)
    return u


def _inproj_carry_kernel(x_ref, g_ref, w_ref, cos_ref, sa_ref, sb_ref, wsc_ref,
                         q_ref, kt_ref, v4_ref, vb_ref, y_ref, tail_ref, carry_ref, *, dims):
    @pl.when(pl.program_id(1) == 0)
    def _():
        carry_ref[...] = jnp.zeros_like(carry_ref)

    def put_q(c, blk):
        q_ref[c] = blk

    def put_k(c, blk):
        kt_ref[c * LANES:(c + 1) * LANES, :] = blk.T

    def put_v(v):
        heads = v.shape[1] // LANES
        for hd in range(heads):
            vh = v[:, hd * LANES:(hd + 1) * LANES]
            vb_ref[hd] = vh.astype(BF16)
            v4_ref[pl.ds(hd, v.shape[0], stride=heads), :] = vh

    fix = lambda shape: _carry_fix(carry_ref, slice(None), shape)
    u = _inproj_body(x_ref, g_ref, w_ref, cos_ref, sa_ref, sb_ref, wsc_ref,
                     put_q, put_k, put_v, y_ref, fix, dims)
    last = u[u.shape[0] - SUBLANES:, :]
    carry_ref[...] = last
    tail_ref[0] = last


def _inproj_state_kernel(x_ref, g_ref, w_ref, cos_ref, sa_ref, sb_ref, wsc_ref,
                         sp1_ref, sp2_ref, q_ref, k_ref, v_ref, y_ref, u_ref,
                         *, dims, seq_len):
    def put_q(c, blk):
        q_ref[:, c * LANES:(c + 1) * LANES] = blk

    def put_k(c, blk):
        k_ref[:, c * LANES:(c + 1) * LANES] = blk

    def put_v(v):
        v_ref[...] = v

    fix = lambda shape: _state_fix(sp1_ref[...], sp2_ref[...], seq_len, shape)
    u_ref[...] = _inproj_body(x_ref, g_ref, w_ref, cos_ref, sa_ref, sb_ref, wsc_ref,
                              put_q, put_k, put_v, y_ref, fix, dims)


def _in_proj(x2d, g, w, tabs, w_sc, dims, *, batch, seq_len, tm, state=None):
    n, d = x2d.shape
    qc, kc, vc, cw, _ = dims
    common_in = [None, _const_spec((1, d)), _const_spec(w.shape), None, None, None,
                 _const_spec(w_sc.shape)]
    if state is None:
        nj = seq_len // tm
        heads = vc // LANES
        row = lambda b, j: (b * nj + j, 0)
        tab = pl.BlockSpec((tm, LANES), lambda b, j: (j, 0))
        in_specs = list(common_in)
        in_specs[0] = pl.BlockSpec((tm, d), row)
        in_specs[3:6] = [tab, tab, tab]
        per_head = pl.BlockSpec((None, heads, tm, LANES), lambda b, j: (b, 0, j, 0))
        out_specs = [per_head,
                     pl.BlockSpec((None, kc, tm), lambda b, j: (b, 0, j)),
                     pl.BlockSpec((tm * heads, LANES), row),
                     per_head, pl.BlockSpec((tm, cw), row),
                     pl.BlockSpec((1, SUBLANES, cw), lambda b, j: (b, 0, 0))]
        out_shape = [jax.ShapeDtypeStruct((batch, heads, seq_len, LANES), BF16),
                     jax.ShapeDtypeStruct((batch, kc, seq_len), F32),
                     jax.ShapeDtypeStruct((n * heads, LANES), F32),
                     jax.ShapeDtypeStruct((batch, heads, seq_len, LANES), BF16),
                     jax.ShapeDtypeStruct((n, cw), BF16),
                     jax.ShapeDtypeStruct((batch, SUBLANES, cw), F32)]
        return pl.pallas_call(
            functools.partial(_inproj_carry_kernel, dims=dims),
            grid=(batch, nj), in_specs=in_specs, out_specs=out_specs, out_shape=out_shape,
            scratch_shapes=[pltpu.VMEM((SUBLANES, cw), F32)],
            compiler_params=_params(2), name="in_proj_prompt",
        )(x2d, g, w, *tabs, w_sc)
    sp1, sp2 = state
    row = lambda i: (i, 0)
    tab = pl.BlockSpec((tm, LANES), row)
    in_specs = list(common_in)
    in_specs[0] = pl.BlockSpec((tm, d), row)
    in_specs[3:6] = [tab, tab, tab]
    in_specs += [pl.BlockSpec((tm, cw), row)] * 2
    out_specs = [pl.BlockSpec((tm, c), row) for c in (qc, kc, vc, cw, cw)]
    out_shape = [jax.ShapeDtypeStruct((n, qc), BF16), jax.ShapeDtypeStruct((n, kc), F32),
                 jax.ShapeDtypeStruct((n, vc), F32), jax.ShapeDtypeStruct((n, cw), BF16),
                 jax.ShapeDtypeStruct((n, cw), F32)]
    return pl.pallas_call(
        functools.partial(_inproj_state_kernel, dims=dims, seq_len=seq_len),
        grid=(n // tm,), in_specs=in_specs, out_specs=out_specs, out_shape=out_shape,
        compiler_params=_params(1), name="in_proj_sample",
    )(x2d, g, w, *tabs, w_sc, sp1, sp2)


def _diff_lambda(lq1_ref, lk1_ref, lq2_ref, lk2_ref, lam_init):
    a = jnp.sum(lq1_ref[...] * lk1_ref[...], axis=-1, keepdims=True)
    b = jnp.sum(lq2_ref[...] * lk2_ref[...], axis=-1, keepdims=True)
    return jnp.exp(a) - jnp.exp(b) + lam_init


def _stack_maps(qt):
    lane = lax.broadcasted_iota(jnp.int32, qt.shape, 1)
    first = lane < (LANES // 2)
    zero = jnp.zeros_like(qt)
    return jnp.concatenate([jnp.where(first, qt, zero), jnp.where(first, zero, qt)], axis=0)


def _finish_head(acc, l, lam, g, lam_init, tq):
    o = acc[:tq] / l[:tq] - lam * (acc[tq:] / l[tq:])
    return _rms(o, g) * (1.0 - lam_init)


def _attn_prompt_kernel(q_ref, k_ref, v_ref, lq1_ref, lk1_ref, lq2_ref, lk2_ref, g_ref,
                        o_ref, kb_ref, vt_ref, *, tq, lam_init):
    nh, seq, hd = v_ref.shape
    for h in range(nh):
        kb_ref[h] = k_ref[h * hd:(h + 1) * hd, :].T.astype(BF16)
        vt_ref[h, :hd, :] = v_ref[h].astype(F32).T.astype(BF16)
        vt_ref[h, hd:, :] = jnp.ones((vt_ref.shape[1] - hd, seq), BF16)
    lam = _diff_lambda(lq1_ref, lk1_ref, lq2_ref, lk2_ref, lam_init)
    g = g_ref[...]
    r = lax.broadcasted_iota(jnp.int32, (tq, 2 * tq), 0)
    c = lax.broadcasted_iota(jnp.int32, (tq, 2 * tq), 1)
    shift = CHUNK.bit_length() - 1
    diag_bias = jnp.where((r >> shift) <= ((c & (tq - 1)) >> shift), 0.0, NEG_INF)

    nq = seq // tq
    order = list(range(0, nq, 2)) + list(range(nq - 1 - (nq % 2), 0, -2))
    def scores(h, i):
        lo = i * tq
        qm = _stack_maps(q_ref[h, lo:lo + tq, :])
        s_d = _dot_nt(kb_ref[h, lo:lo + tq, :], qm) + diag_bias
        m = jnp.max(s_d, axis=0, keepdims=True)
        s_o = None
        if i:
            s_o = _dot_nt(kb_ref[h, :lo, :], qm)
            m = jnp.maximum(m, jnp.max(s_o, axis=0, keepdims=True))
        return s_d, s_o, m

    def finish(h, i, s_d, s_o, m):
        lo = i * tq
        acc = _dot(vt_ref[h, :, lo:lo + tq], jnp.exp2(s_d - m).astype(BF16))
        if i:
            acc = acc + _dot(vt_ref[h, :, :lo], jnp.exp2(s_o - m).astype(BF16))
        num, l = acc[:hd, :], acc[hd:hd + 1, :]
        ot = num[:, :tq] / l[:, :tq] - lam * (num[:, tq:] / l[:, tq:])
        ot = ot * lax.rsqrt(jnp.mean(ot * ot, axis=0, keepdims=True) + EPS) * g
        o_ref[h, lo:lo + tq, :] = (ot * (1.0 - lam_init)).T.astype(BF16)

    tasks = [(h, i) for i in order for h in range(nh)]
    ahead = 2
    pending = [scores(*t) for t in tasks[:ahead]]
    for n, t in enumerate(tasks):
        if n + ahead < len(tasks):
            pending.append(scores(*tasks[n + ahead]))
        finish(*t, *pending.pop(0))


def _attn_prompt(q4, kt3, v4, lams, g_sub, *, tq, lam_init):
    b, heads, seq, _ = q4.shape
    nh = ATTN_HEADS_PER_STEP
    blk = lambda: pl.BlockSpec((None, nh, seq, LANES), lambda i, h: (i, h, 0, 0))
    lam_spec = _const_spec(lams[0].shape)
    g_col = g_sub.reshape(-1, 1)
    ones_rows = 2 * SUBLANES
    return pl.pallas_call(
        functools.partial(_attn_prompt_kernel, tq=tq, lam_init=lam_init),
        grid=(b, heads // nh),
        in_specs=[blk(), pl.BlockSpec((None, nh * LANES, seq), lambda i, h: (i, h, 0)), blk(),
                  lam_spec, lam_spec, lam_spec, lam_spec, _const_spec(g_col.shape)],
        out_specs=blk(),
        out_shape=jax.ShapeDtypeStruct(q4.shape, BF16),
        scratch_shapes=[pltpu.VMEM((nh, seq, LANES), BF16),
                        pltpu.VMEM((nh, LANES + ones_rows, seq), BF16)],
        compiler_params=_params(2), name="attn_prompt",
    )(q4, kt3, v4, *lams, g_col)


def _attn_sample_kernel(q_ref, kn_ref, vn_ref, kp_ref, vp_ref, lq1_ref, lk1_ref, lq2_ref,
                        lk2_ref, g_ref, o_ref, *, lam_init):
    t_new = q_ref.shape[1]
    heads = q_ref.shape[2] // LANES
    past = kp_ref.shape[2]
    lam = _diff_lambda(lq1_ref, lk1_ref, lq2_ref, lk2_ref, lam_init)
    g = g_ref[...]
    pad = jnp.zeros((LANES - t_new, LANES), BF16)
    lane = lax.broadcasted_iota(jnp.int32, (2 * t_new, LANES), 1)
    for bi, h in [(bi, h) for bi in range(q_ref.shape[0]) for h in range(heads)]:
        cols = slice(h * LANES, (h + 1) * LANES)
        qm = _stack_maps(q_ref[bi, :, cols])
        kn = jnp.concatenate([kn_ref[bi, :, cols].astype(BF16), pad], axis=0)
        vn = jnp.concatenate([vn_ref[bi, :, cols].astype(BF16), pad], axis=0)
        s_p = _dot(qm, kp_ref[bi, cols, :].astype(BF16))
        vp = vp_ref[bi, pl.ds(h, past, stride=heads), :].astype(BF16)
        s_n = jnp.where(lane < t_new, _dot_nt(qm, kn), NEG_INF)
        m = jnp.maximum(jnp.max(s_p, axis=-1, keepdims=True),
                        jnp.max(s_n, axis=-1, keepdims=True))
        e_p = jnp.exp2(s_p - m)
        e_n = jnp.exp2(s_n - m)
        l = jnp.sum(e_p, axis=-1, keepdims=True) + jnp.sum(e_n, axis=-1, keepdims=True)
        acc = _dot(e_p.astype(BF16), vp) + _dot(e_n.astype(BF16), vn)
        o_ref[bi, :, cols] = _finish_head(acc, l, lam, g, lam_init, t_new).astype(BF16)


def _attn_sample(q3, kn3, vn3, kpt3, vp4, lams, g_sub, *, lam_init, nb):
    b, t_new, width = q3.shape
    new = lambda: pl.BlockSpec((nb, t_new, width), lambda i: (i, 0, 0))
    whole = lambda a: pl.BlockSpec((nb,) + a.shape[1:], lambda i: (i, 0, 0))
    lam_spec = _const_spec(lams[0].shape)
    return pl.pallas_call(
        functools.partial(_attn_sample_kernel, lam_init=lam_init),
        grid=(b // nb,),
        in_specs=[new(), new(), new(), whole(kpt3), whole(vp4), lam_spec, lam_spec, lam_spec,
                  lam_spec, _const_spec(g_sub.shape)],
        out_specs=new(),
        out_shape=jax.ShapeDtypeStruct((b, t_new, width), BF16),
        compiler_params=_params(1), name="attn_sample",
    )(q3, kn3, vn3, kpt3, vp4, *lams, g_sub)


def _mix_and_query(x_ref, o_ref, y_ref, wout_ref, gx_ref, wxq_ref, heads):
    if len(o_ref.shape) == 3:
        o = jnp.concatenate([o_ref[h] for h in range(o_ref.shape[0])], axis=1)
    else:
        o = o_ref[...]
    aw = o.shape[-1]
    x1 = x_ref[...] + _dot(o, wout_ref[:aw, :]) + _dot(y_ref[...], wout_ref[aw:, :])
    hd = x1.shape[-1] // heads
    scale = hd ** -0.5 * math.log2(math.e)
    hq = (_dot(_rms(x1, gx_ref[...]).astype(BF16), wxq_ref[...]) * scale).astype(BF16)
    return x1, hq


def _cross_attend(hq, mk_ref, mv_ref, bi, heads, put):
    hd = hq.shape[-1] // heads
    for h in range(heads):
        cols = slice(h * hd, (h + 1) * hd)
        s = _dot_nt(hq[:, cols], _load_mem_head(mk_ref, bi, h, heads, hd))
        e = jnp.exp2(s - jnp.max(s, axis=-1, keepdims=True))
        l = jnp.sum(e, axis=-1, keepdims=True)
        xo = _dot(e.astype(BF16), _load_mem_head(mv_ref, bi, h, heads, hd)) / l
        put(cols, xo.astype(BF16))


def _post_kernel(x_ref, o_ref, y_ref, wout_ref, gx_ref, wxq_ref, mk_ref, mv_ref, wxo_ref,
                 out_ref, xo_ref, *, heads):
    x1, hq = _mix_and_query(x_ref, o_ref, y_ref, wout_ref, gx_ref, wxq_ref, heads)

    def put(cols, xo):
        xo_ref[:, cols] = xo

    _cross_attend(hq, mk_ref, mv_ref, 0, heads, put)
    out_ref[...] = x1 + _dot(xo_ref[...], wxo_ref[...])


def _post_rows_kernel(x_ref, o_ref, y_ref, wout_ref, gx_ref, wxq_ref, mk_ref, mv_ref, wxo_ref,
                      out_ref, x1_ref, hq_ref, xo_ref, *, heads, seq_len):
    b = pl.program_id(0)
    nb = mk_ref.shape[0]

    @pl.when(b == 0)
    def _():
        x1_ref[...], hq_ref[...] = _mix_and_query(x_ref, o_ref, y_ref, wout_ref, gx_ref,
                                                  wxq_ref, heads)

    for bi in range(nb):
        rows = pl.ds(pl.multiple_of((b * nb + bi) * seq_len, seq_len), seq_len)

        def put(cols, xo, rows=rows):
            xo_ref[rows, cols] = xo

        _cross_attend(hq_ref[rows, :], mk_ref, mv_ref, bi, heads, put)

    @pl.when(b == pl.num_programs(0) - 1)
    def _():
        out_ref[...] = x1_ref[...] + _dot(xo_ref[...], wxo_ref[...])


def _post_rows(x2d, o2d, y2d, w_out, g_x, w_xq, mk3, mv3, w_xo, *, batch, seq_len, heads, nb):
    n, d = x2d.shape
    mem = pl.BlockSpec((nb,) + mk3.shape[1:], lambda b: (b, 0, 0))
    return pl.pallas_call(
        functools.partial(_post_rows_kernel, heads=heads, seq_len=seq_len),
        grid=(batch // nb,),
        in_specs=[_const_spec(x2d.shape), _const_spec(o2d.shape), _const_spec(y2d.shape),
                  _const_spec(w_out.shape), _const_spec((1, d)), _const_spec(w_xq.shape),
                  mem, mem, _const_spec(w_xo.shape)],
        out_specs=pl.BlockSpec((n, d), lambda b: (0, 0)),
        out_shape=jax.ShapeDtypeStruct((n, d), F32),
        scratch_shapes=[pltpu.VMEM((n, d), F32), pltpu.VMEM((n, d), BF16),
                        pltpu.VMEM((n, d), BF16)],
        compiler_params=_params(1), name="post_sample",
    )(x2d, o2d, y2d, w_out, g_x, w_xq, mk3, mv3, w_xo)


def _post(x2d, o4, y2d, w_out, g_x, w_xq, mk3, mv3, w_xo, *, batch, seq_len, tm, heads):
    n, d = x2d.shape
    cw = y2d.shape[-1]
    nj = seq_len // tm
    row = lambda b, j: (b * nj + j, 0)
    mem = pl.BlockSpec((1,) + mk3.shape[1:], lambda b, j: (b, 0, 0))
    o_spec = pl.BlockSpec((None, o4.shape[1], tm, LANES), lambda b, j: (b, 0, j, 0))
    return pl.pallas_call(
        functools.partial(_post_kernel, heads=heads),
        grid=(batch, nj),
        in_specs=[pl.BlockSpec((tm, d), row), o_spec,
                  pl.BlockSpec((tm, cw), row), _const_spec(w_out.shape), _const_spec((1, d)),
                  _const_spec(w_xq.shape), mem, mem, _const_spec(w_xo.shape)],
        out_specs=pl.BlockSpec((tm, d), row),
        out_shape=jax.ShapeDtypeStruct((n, d), F32),
        scratch_shapes=[pltpu.VMEM((tm, d), BF16)],
        compiler_params=_params(2), name="post",
    )(x2d, o4, y2d, w_out, g_x, w_xq, mk3, mv3, w_xo)


FF_CHUNK = 1024


def _ffn_body(x_ref, g_ref, wup_ref, wgate_ref, wconv_ref, wdown_ref, gfin_ref, y_ref,
              a_ref, fix, emit_up, final_norm):
    x = x_ref[...]
    hf = _rms(x, g_ref[...]).astype(BF16)
    dff = wup_ref.shape[-1]
    for c0 in range(0, dff, FF_CHUNK):
        cols = slice(c0, min(c0 + FF_CHUNK, dff))
        up = _dot(hf, wup_ref[:, cols])
        emit_up(cols, up)
        uc = _conv3(up, wconv_ref[:, cols], fix(cols, up.shape))
        gate = _dot(hf, wgate_ref[:, cols])
        a_ref[:, cols] = (uc * jax.nn.sigmoid(uc) * gate).astype(BF16)
    half = x.shape[0] // 2
    for r0 in (0, half):
        x3 = x[r0:r0 + half] + _dot(a_ref[r0:r0 + half, :], wdown_ref[...])
        y_ref[r0:r0 + half, :] = _rms(x3, gfin_ref[...]) if final_norm else x3


def _ffn_carry_kernel(x_ref, g_ref, wup_ref, wgate_ref, wconv_ref, wdown_ref, gfin_ref,
                      y_ref, tail_ref, a_ref, carry_ref, new_ref, *, final_norm):
    @pl.when(pl.program_id(1) == 0)
    def _():
        carry_ref[...] = jnp.zeros_like(carry_ref)

    def emit_up(cols, up):
        new_ref[:, cols] = up[up.shape[0] - SUBLANES:, :]

    fix = lambda cols, shape: _carry_fix(carry_ref, cols, shape)
    _ffn_body(x_ref, g_ref, wup_ref, wgate_ref, wconv_ref, wdown_ref, gfin_ref, y_ref,
              a_ref, fix, emit_up, final_norm)
    carry_ref[...] = new_ref[...]
    tail_ref[0] = new_ref[...]


def _ffn_state_kernel(x_ref, g_ref, wup_ref, wgate_ref, wconv_ref, wdown_ref, gfin_ref,
                      sp1_ref, sp2_ref, y_ref, up_ref, a_ref, *, seq_len, final_norm):
    def emit_up(cols, up):
        up_ref[:, cols] = up

    fix = lambda cols, shape: _state_fix(sp1_ref[:, cols], sp2_ref[:, cols], seq_len, shape)
    _ffn_body(x_ref, g_ref, wup_ref, wgate_ref, wconv_ref, wdown_ref, gfin_ref, y_ref,
              a_ref, fix, emit_up, final_norm)


def _ffn(x2d, g_ffn, w_up, w_gate, w_conv, w_down, g_final, *, batch, seq_len, tm, final_norm,
         state=None):
    n, d = x2d.shape
    dff = w_up.shape[-1]
    weights = [_const_spec((1, d)), _const_spec(w_up.shape), _const_spec(w_gate.shape),
               _const_spec(w_conv.shape), _const_spec(w_down.shape), _const_spec((1, d))]
    if state is None:
        nj = seq_len // tm
        row = lambda b, j: (b * nj + j, 0)
        return pl.pallas_call(
            functools.partial(_ffn_carry_kernel, final_norm=final_norm),
            grid=(batch, nj),
            in_specs=[pl.BlockSpec((tm, d), row)] + weights,
            out_specs=[pl.BlockSpec((tm, d), row),
                       pl.BlockSpec((1, SUBLANES, dff), lambda b, j: (b, 0, 0))],
            out_shape=[jax.ShapeDtypeStruct((n, d), F32),
                       jax.ShapeDtypeStruct((batch, SUBLANES, dff), F32)],
            scratch_shapes=[pltpu.VMEM((tm, dff), BF16), pltpu.VMEM((SUBLANES, dff), F32),
                            pltpu.VMEM((SUBLANES, dff), F32)],
            compiler_params=_params(2), name="ffn_prompt",
        )(x2d, g_ffn, w_up, w_gate, w_conv, w_down, g_final)
    sp1, sp2 = state
    row = lambda i: (i, 0)
    return pl.pallas_call(
        functools.partial(_ffn_state_kernel, seq_len=seq_len, final_norm=final_norm),
        grid=(n // tm,),
        in_specs=[pl.BlockSpec((tm, d), row)] + weights + [pl.BlockSpec((tm, dff), row)] * 2,
        out_specs=[pl.BlockSpec((tm, d), row), pl.BlockSpec((tm, dff), row)],
        out_shape=[jax.ShapeDtypeStruct((n, d), F32), jax.ShapeDtypeStruct((n, dff), F32)],
        scratch_shapes=[pltpu.VMEM((tm, dff), BF16)],
        compiler_params=_params(1), name="ffn_sample",
    )(x2d, g_ffn, w_up, w_gate, w_conv, w_down, g_final, sp1, sp2)


def _rope_tables(pos, qk_head_dim):
    rot = qk_head_dim // 4
    half = rot // 2
    inv = 1.0 / (ROPE_THETA ** (jnp.arange(half, dtype=F32) * 2.0 / rot))
    ang = pos.astype(F32)[:, None] * inv[None, :]
    cos, sin = jnp.cos(ang), jnp.sin(ang)
    t = pos.shape[0]
    zeros = lambda w: jnp.zeros((t, w), F32)
    c = jnp.concatenate([cos, cos, jnp.ones((t, qk_head_dim - rot), F32)], axis=1)
    a = jnp.concatenate([-sin, zeros(qk_head_dim - half)], axis=1)
    b = jnp.concatenate([zeros(half), sin, zeros(qk_head_dim - rot)], axis=1)
    rep = LANES // qk_head_dim
    return tuple(jnp.tile(m, (1, rep)) for m in (c, a, b))


def _expand_state(state, seq_len):
    b, k, c = state.shape
    sp1 = jnp.pad(state[:, k - 1:], ((0, 0), (0, seq_len - 1), (0, 0)))
    sp2 = jnp.pad(state, ((0, 0), (0, seq_len - k), (0, 0)))
    return sp1.reshape(b * seq_len, c), sp2.reshape(b * seq_len, c)


def kernel(x_prompt, x_sample, cache_attn_k, cache_attn_v, state_short_conv, state_ffn_conv,
           cache_mem_k, cache_mem_v, mem_prompt, g_mix, w_in, lam_q1, lam_k1, lam_q2, lam_k2,
           g_sub, w_sc, w_out, g_mem, g_x, w_xq, w_xk, w_xv, w_xo, g_ffn, w_up, w_gate,
           w_ffconv, w_down, g_final):
    depth = w_in.shape[0]
    bp, seq, d = x_prompt.shape
    bs, t_new, _ = x_sample.shape
    past = cache_attn_k.shape[2]
    heads_a, qk_dim = cache_attn_k.shape[3], cache_attn_k.shape[5]
    v_dim = cache_attn_v.shape[4]
    cw = state_short_conv.shape[-1]
    dff = state_ffn_conv.shape[-1]
    n_mem, heads_x, x_dim = cache_mem_k.shape[2:]
    qc = kc = heads_a * 2 * qk_dim
    vc = heads_a * v_dim
    dims = (qc, kc, vc, cw, qk_dim ** -0.5 * math.log2(math.e))
    assert 2 * qk_dim == LANES and v_dim == LANES and qk_dim == CHUNK

    tabs_p = _rope_tables(jnp.arange(seq, dtype=jnp.int32), qk_dim)
    tabs_s = tuple(jnp.tile(m, (bs, 1)) for m in
                   _rope_tables(past + jnp.arange(t_new, dtype=jnp.int32), qk_dim))
    row = lambda v: v.reshape(1, -1)

    hp = x_prompt.reshape(bp * seq, d)
    hs = x_sample.reshape(bs * t_new, d)
    outs_p = [[] for _ in range(6)]
    outs_s = [[] for _ in range(4)]
    for l in range(depth):
        lam_init = _lambda_init(l)
        wi, wo = w_in[l].astype(BF16), w_out[l].astype(BF16)
        wq, wxo_b = w_xq[l].astype(BF16), w_xo[l].astype(BF16)
        wkv = jnp.concatenate([w_xk[l], w_xv[l]], axis=1).astype(BF16)
        wu, wg, wd = w_up[l].astype(BF16), w_gate[l].astype(BF16), w_down[l].astype(BF16)
        lams = tuple(row(v[l]) for v in (lam_q1, lam_k1, lam_q2, lam_k2))
        gsub = row(g_sub[l])

        mk8, mv8, mk, mv = _mem_kv(mem_prompt.reshape(bp * n_mem, d), row(g_mem[l]), wkv,
                                   MEM_TILE, heads_x)
        q, kt, v4, vb, ysc, sc_tail = _in_proj(hp, row(g_mix[l]), wi, tabs_p, w_sc[l], dims,
                                               batch=bp, seq_len=seq, tm=PROJ_TILE)
        o = _attn_prompt(q, kt, vb, lams, gsub, tq=Q_TILE, lam_init=lam_init)
        x2 = _post(hp, o, ysc, wo, row(g_x[l]), wq,
                   mk.reshape(bp, n_mem, d), mv.reshape(bp, n_mem, d), wxo_b,
                   batch=bp, seq_len=seq, tm=POST_TILE, heads=heads_x)
        hp, ff_tail = _ffn(x2, row(g_ffn[l]), wu, wg, w_ffconv[l], wd, row(g_final),
                           batch=bp, seq_len=seq, tm=FFN_TILE, final_norm=l == depth - 1)
        outs_p[0].append(jnp.transpose(kt.reshape(bp, heads_a, 2, qk_dim, seq), (0, 4, 1, 2, 3)))
        outs_p[1].append(v4.reshape(bp, seq, heads_a, v_dim))
        outs_p[2].append(sc_tail[:, SUBLANES - 2:])
        outs_p[3].append(ff_tail[:, SUBLANES - 2:])
        outs_p[4].append(_from_mem_layout(mk8, bp, n_mem, heads_x, x_dim))
        outs_p[5].append(_from_mem_layout(mv8, bp, n_mem, heads_x, x_dim))

        n_s = bs * t_new
        qs, ks, vs, yscs, u_s = _in_proj(
            hs, row(g_mix[l]), wi, tabs_s, w_sc[l], dims, batch=bs, seq_len=t_new, tm=n_s,
            state=_expand_state(state_short_conv[l], t_new))
        o_s = _attn_sample(qs.reshape(bs, t_new, qc), ks.reshape(bs, t_new, kc),
                           vs.reshape(bs, t_new, vc),
                           jnp.transpose(cache_attn_k[l].reshape(bs, past, kc), (0, 2, 1)),
                           cache_attn_v[l].reshape(bs, past * heads_a, v_dim), lams, gsub,
                           lam_init=lam_init, nb=SAMPLE_ATTN_SEQS)
        x2s = _post_rows(hs, o_s.reshape(n_s, vc), yscs, wo, row(g_x[l]), wq,
                         _to_mem_layout(cache_mem_k[l], x_dim // LANES),
                         _to_mem_layout(cache_mem_v[l], x_dim // LANES), wxo_b,
                         batch=bs, seq_len=t_new, heads=heads_x, nb=SAMPLE_POST_SEQS)
        hs, up_s = _ffn(x2s, row(g_ffn[l]), wu, wg, w_ffconv[l], wd, row(g_final),
                        batch=bs, seq_len=t_new, tm=n_s, final_norm=l == depth - 1,
                        state=_expand_state(state_ffn_conv[l], t_new))
        outs_s[0].append(ks.reshape(bs, t_new, heads_a, 2, qk_dim))
        outs_s[1].append(vs.reshape(bs, t_new, heads_a, v_dim))
        outs_s[2].append(u_s.reshape(bs, t_new, cw)[:, t_new - 2:])
        outs_s[3].append(up_s.reshape(bs, t_new, dff)[:, t_new - 2:])

    return (hp.reshape(bp, seq, d), hs.reshape(bs, t_new, d),
            *(jnp.stack(o) for o in outs_p), *(jnp.stack(o) for o in outs_s))
```

```python
import functools
import math

import jax
import jax.numpy as jnp
from jax import lax
from jax.experimental import pallas as pl
from jax.experimental.pallas import tpu as pltpu

EPS = 1e-6
CHUNK = 64
ROPE_THETA = 500000.0
LANES = 128
SUBLANES = 8
VMEM_LIMIT = 56 * 1024 * 1024
MEM_TILE = 1024
PROJ_TILE = 1024
POST_TILE = 1024
FFN_TILE = 1024
Q_TILE = 128
ATTN_HEADS_PER_STEP = 4
ATTN_LOOKAHEAD = 2
SAMPLE_ATTN_SEQS = 4
SAMPLE_POST_SEQS = 4
BF16 = jnp.bfloat16
F32 = jnp.float32
NEG_INF = float("-inf")


def _lambda_init(layer_idx):
    return 0.8 - 0.6 * math.exp(-0.3 * layer_idx)


def _rms(x, g):
    return x * lax.rsqrt(jnp.mean(x * x, axis=-1, keepdims=True) + EPS) * g


def _dot(a, b):
    return jnp.dot(a, b, preferred_element_type=F32)


def _dot_nt(a, b):
    return lax.dot_general(a, b, (((1,), (1,)), ((), ())), preferred_element_type=F32)


def _params(n_grid):
    return pltpu.CompilerParams(dimension_semantics=("arbitrary",) * n_grid,
                                vmem_limit_bytes=VMEM_LIMIT)


def _const_spec(shape):
    nd = len(shape)
    return pl.BlockSpec(shape, lambda *_: (0,) * nd, pipeline_mode=pl.Buffered(1))


def _conv3(u, w, fix):
    p1 = pltpu.roll(u, 1, axis=0)
    p2 = pltpu.roll(u, 2, axis=0)
    p1, p2 = fix(p1, p2)
    return w[0:1, :] * p2 + w[1:2, :] * p1 + w[2:3, :] * u


def _carry_fix(carry_ref, cols, shape):
    row = lax.broadcasted_iota(jnp.int32, shape, 0)
    c0 = carry_ref[SUBLANES - 2:SUBLANES - 1, cols]
    c1 = carry_ref[SUBLANES - 1:SUBLANES, cols]

    def fix(p1, p2):
        p1 = jnp.where(row == 0, c1, p1)
        p2 = jnp.where(row == 0, c0, jnp.where(row == 1, c1, p2))
        return p1, p2
    return fix


def _state_fix(sp1, sp2, seq_len, shape):
    row = lax.broadcasted_iota(jnp.int32, shape, 0)
    t = lax.rem(row, seq_len)

    def fix(p1, p2):
        return jnp.where(t == 0, sp1, p1), jnp.where(t < 2, sp2, p2)
    return fix


def _store_mem_layout(ref, blk, heads):
    rows, width = blk.shape
    lt = width // heads // LANES
    for h in range(heads):
        for t in range(lt):
            c0 = (h * lt + t) * LANES
            ref[pl.ds(t * heads + h, rows, stride=heads * lt), :] = blk[:, c0:c0 + LANES]


def _load_mem_head(ref, bi, h, heads, hd):
    if ref.shape[-1] != LANES:
        return ref[bi, :, h * hd:(h + 1) * hd]
    lt = hd // LANES
    rows = ref.shape[1] // (heads * lt)
    parts = [ref[bi, pl.ds(t * heads + h, rows, stride=heads * lt), :] for t in range(lt)]
    return jnp.concatenate(parts, axis=1).astype(BF16)


def _to_mem_layout(x, lt):
    b, n, heads, hd = x.shape
    x = jnp.transpose(x.reshape(b, n, heads, lt, LANES), (0, 1, 3, 2, 4))
    return x.reshape(b, n * heads * lt, LANES)


def _from_mem_layout(x8, b, n, heads, hd):
    lt = hd // LANES
    x = jnp.transpose(x8.reshape(b, n, lt, heads, LANES), (0, 1, 3, 2, 4))
    return x.reshape(b, n, heads, hd)


def _memkv_kernel(m_ref, g_ref, w_ref, k8_ref, v8_ref, kb_ref, vb_ref, *, heads):
    d = kb_ref.shape[-1]
    h = _rms(m_ref[...], g_ref[...]).astype(BF16)
    kv = _dot(h, w_ref[...])
    for part, (o8, ob) in enumerate(((k8_ref, kb_ref), (v8_ref, vb_ref))):
        blk = kv[:, part * d:(part + 1) * d]
        ob[...] = blk.astype(BF16)
        _store_mem_layout(o8, blk, heads)


def _mem_kv(mem2d, g_mem, w_xkv, tm, heads):
    n, d = mem2d.shape
    group = d // LANES
    row = lambda i: (i, 0)
    return pl.pallas_call(
        functools.partial(_memkv_kernel, heads=heads),
        grid=(n // tm,),
        in_specs=[pl.BlockSpec((tm, d), row), _const_spec((1, d)), _const_spec(w_xkv.shape)],
        out_specs=[pl.BlockSpec((tm * group, LANES), row)] * 2 + [pl.BlockSpec((tm, d), row)] * 2,
        out_shape=[jax.ShapeDtypeStruct((n * group, LANES), F32)] * 2
        + [jax.ShapeDtypeStruct((n, d), BF16)] * 2,
        compiler_params=_params(1),
        name="mem_kv",
    )(mem2d, g_mem, w_xkv)


def _inproj_body(x_ref, g_ref, w_ref, cos_ref, sa_ref, sb_ref, wsc_ref,
                 put_q, put_k, put_v, y_ref, fix, dims):
    qc, kc, vc, cw, q_scale, rot_half = dims
    h = _rms(x_ref[...], g_ref[...]).astype(BF16)
    cos, sa, sb = cos_ref[...], sa_ref[...], sb_ref[...]

    def rope(t):
        return (t * cos + pltpu.roll(t, LANES - rot_half, axis=1) * sa
                + pltpu.roll(t, rot_half, axis=1) * sb)

    o = qc + kc
    cx = _dot(h, w_ref[:, o + vc + cw:])
    u = cx[:, :cw] * cx[:, cw:]
    conv = _conv3(u, wsc_ref[...], fix(u.shape))
    qk = _dot(h, w_ref[:, :qc + kc])
    for c in range(qc // LANES):
        blk = rope(qk[:, c * LANES:(c + 1) * LANES])
        put_q(c, (blk * q_scale).astype(BF16))
    for c in range(kc // LANES):
        lo = qc + c * LANES
        put_k(c, rope(qk[:, lo:lo + LANES]))
    put_v(_dot(h, w_ref[:, o:o + vc]))
    bg = _dot(h, w_ref[:, o + vc:o + vc + cw])
    y_ref[...] = (bg * conv).astype(BF16)
    return u


def _inproj_carry_kernel(x_ref, g_ref, w_ref, cos_ref, sa_ref, sb_ref, wsc_ref,
                         q_ref, kt_ref, v4_ref, vb_ref, y_ref, tail_ref, carry_ref, *, dims):
    @pl.when(pl.program_id(1) == 0)
    def _():
        carry_ref[...] = jnp.zeros_like(carry_ref)

    def put_q(c, blk):
        q_ref[c] = blk

    def put_k(c, blk):
        kt_ref[c * LANES:(c + 1) * LANES, :] = blk.T

    def put_v(v):
        heads = v.shape[1] // LANES
        for hd in range(heads):
            vh = v[:, hd * LANES:(hd + 1) * LANES]
            vb_ref[hd] = vh.astype(BF16)
            v4_ref[pl.ds(hd, v.shape[0], stride=heads), :] = vh

    fix = lambda shape: _carry_fix(carry_ref, slice(None), shape)
    u = _inproj_body(x_ref, g_ref, w_ref, cos_ref, sa_ref, sb_ref, wsc_ref,
                     put_q, put_k, put_v, y_ref, fix, dims)
    last = u[u.shape[0] - SUBLANES:, :]
    carry_ref[...] = last
    tail_ref[0] = last


def _inproj_state_kernel(x_ref, g_ref, w_ref, cos_ref, sa_ref, sb_ref, wsc_ref,
                         sp1_ref, sp2_ref, q_ref, k_ref, v_ref, y_ref, u_ref,
                         *, dims, seq_len):
    def put_q(c, blk):
        q_ref[:, c * LANES:(c + 1) * LANES] = blk

    def put_k(c, blk):
        k_ref[:, c * LANES:(c + 1) * LANES] = blk

    def put_v(v):
        v_ref[...] = v

    fix = lambda shape: _state_fix(sp1_ref[...], sp2_ref[...], seq_len, shape)
    u_ref[...] = _inproj_body(x_ref, g_ref, w_ref, cos_ref, sa_ref, sb_ref, wsc_ref,
                              put_q, put_k, put_v, y_ref, fix, dims)


def _in_proj(x2d, g, w, tabs, w_sc, dims, *, batch, seq_len, tm, state=None):
    n, d = x2d.shape
    qc, kc, vc, cw = dims[:4]
    common_in = [None, _const_spec((1, d)), _const_spec(w.shape), None, None, None,
                 _const_spec(w_sc.shape)]
    if state is None:
        nj = seq_len // tm
        heads = vc // LANES
        row = lambda b, j: (b * nj + j, 0)
        tab = pl.BlockSpec((tm, LANES), lambda b, j: (j, 0))
        in_specs = list(common_in)
        in_specs[0] = pl.BlockSpec((tm, d), row)
        in_specs[3:6] = [tab, tab, tab]
        per_head = pl.BlockSpec((None, heads, tm, LANES), lambda b, j: (b, 0, j, 0))
        out_specs = [per_head,
                     pl.BlockSpec((None, kc, tm), lambda b, j: (b, 0, j)),
                     pl.BlockSpec((tm * heads, LANES), row),
                     per_head, pl.BlockSpec((tm, cw), row),
                     pl.BlockSpec((1, SUBLANES, cw), lambda b, j: (b, 0, 0))]
        out_shape = [jax.ShapeDtypeStruct((batch, heads, seq_len, LANES), BF16),
                     jax.ShapeDtypeStruct((batch, kc, seq_len), F32),
                     jax.ShapeDtypeStruct((n * heads, LANES), F32),
                     jax.ShapeDtypeStruct((batch, heads, seq_len, LANES), BF16),
                     jax.ShapeDtypeStruct((n, cw), BF16),
                     jax.ShapeDtypeStruct((batch, SUBLANES, cw), F32)]
        return pl.pallas_call(
            functools.partial(_inproj_carry_kernel, dims=dims),
            grid=(batch, nj), in_specs=in_specs, out_specs=out_specs, out_shape=out_shape,
            scratch_shapes=[pltpu.VMEM((SUBLANES, cw), F32)],
            compiler_params=_params(2), name="in_proj_prompt",
        )(x2d, g, w, *tabs, w_sc)
    sp1, sp2 = state
    row = lambda i: (i, 0)
    tab = pl.BlockSpec((tm, LANES), row)
    in_specs = list(common_in)
    in_specs[0] = pl.BlockSpec((tm, d), row)
    in_specs[3:6] = [tab, tab, tab]
    in_specs += [pl.BlockSpec((tm, cw), row)] * 2
    out_specs = [pl.BlockSpec((tm, c), row) for c in (qc, kc, vc, cw, cw)]
    out_shape = [jax.ShapeDtypeStruct((n, qc), BF16), jax.ShapeDtypeStruct((n, kc), F32),
                 jax.ShapeDtypeStruct((n, vc), F32), jax.ShapeDtypeStruct((n, cw), BF16),
                 jax.ShapeDtypeStruct((n, cw), F32)]
    return pl.pallas_call(
        functools.partial(_inproj_state_kernel, dims=dims, seq_len=seq_len),
        grid=(n // tm,), in_specs=in_specs, out_specs=out_specs, out_shape=out_shape,
        compiler_params=_params(1), name="in_proj_sample",
    )(x2d, g, w, *tabs, w_sc, sp1, sp2)


def _diff_lambda(lq1_ref, lk1_ref, lq2_ref, lk2_ref, lam_init):
    a = jnp.sum(lq1_ref[...] * lk1_ref[...], axis=-1, keepdims=True)
    b = jnp.sum(lq2_ref[...] * lk2_ref[...], axis=-1, keepdims=True)
    return jnp.exp(a) - jnp.exp(b) + lam_init


def _stack_maps(qt):
    lane = lax.broadcasted_iota(jnp.int32, qt.shape, 1)
    first = lane < (LANES // 2)
    zero = jnp.zeros_like(qt)
    return jnp.concatenate([jnp.where(first, qt, zero), jnp.where(first, zero, qt)], axis=0)


def _finish_head(acc, l, lam, g, lam_init, tq):
    o = acc[:tq] / l[:tq] - lam * (acc[tq:] / l[tq:])
    return _rms(o, g) * (1.0 - lam_init)


def _attn_prompt_kernel(q_ref, k_ref, v_ref, lq1_ref, lk1_ref, lq2_ref, lk2_ref, g_ref,
                        o_ref, kb_ref, vt_ref, *, tq, lam_init):
    nh, seq, hd = v_ref.shape
    for h in range(nh):
        kb_ref[h] = k_ref[h * hd:(h + 1) * hd, :].T.astype(BF16)
        vt_ref[h, :hd, :] = v_ref[h].astype(F32).T.astype(BF16)
        vt_ref[h, hd:, :] = jnp.ones((vt_ref.shape[1] - hd, seq), BF16)
    lam = _diff_lambda(lq1_ref, lk1_ref, lq2_ref, lk2_ref, lam_init)
    g = g_ref[...]
    r = lax.broadcasted_iota(jnp.int32, (tq, 2 * tq), 0)
    c = lax.broadcasted_iota(jnp.int32, (tq, 2 * tq), 1)
    shift = CHUNK.bit_length() - 1
    diag_bias = jnp.where((r >> shift) <= ((c & (tq - 1)) >> shift), 0.0, NEG_INF)

    nq = seq // tq
    order = list(range(0, nq, 2)) + list(range(nq - 1 - (nq % 2), 0, -2))
    def scores(h, i):
        lo = i * tq
        qm = _stack_maps(q_ref[h, lo:lo + tq, :])
        s_d = _dot_nt(kb_ref[h, lo:lo + tq, :], qm) + diag_bias
        m = jnp.max(s_d, axis=0, keepdims=True)
        s_o = None
        if i:
            s_o = _dot_nt(kb_ref[h, :lo, :], qm)
            m = jnp.maximum(m, jnp.max(s_o, axis=0, keepdims=True))
        return s_d, s_o, m

    def finish(h, i, s_d, s_o, m):
        lo = i * tq
        acc = _dot(vt_ref[h, :, lo:lo + tq], jnp.exp2(s_d - m).astype(BF16))
        if i:
            acc = acc + _dot(vt_ref[h, :, :lo], jnp.exp2(s_o - m).astype(BF16))
        num, l = acc[:hd, :], acc[hd:hd + 1, :]
        ot = num[:, :tq] / l[:, :tq] - lam * (num[:, tq:] / l[:, tq:])
        ot = ot * lax.rsqrt(jnp.mean(ot * ot, axis=0, keepdims=True) + EPS) * g
        o_ref[h, lo:lo + tq, :] = (ot * (1.0 - lam_init)).T.astype(BF16)

    tasks = [(h, i) for i in order for h in range(nh)]
    pending = [scores(*t) for t in tasks[:ATTN_LOOKAHEAD]]
    for n, t in enumerate(tasks):
        if n + ATTN_LOOKAHEAD < len(tasks):
            pending.append(scores(*tasks[n + ATTN_LOOKAHEAD]))
        finish(*t, *pending.pop(0))


def _attn_prompt(q4, kt3, v4, lams, g_sub, *, tq, lam_init):
    b, heads, seq, _ = q4.shape
    nh = ATTN_HEADS_PER_STEP
    blk = lambda: pl.BlockSpec((None, nh, seq, LANES), lambda i, h: (i, h, 0, 0))
    lam_spec = _const_spec(lams[0].shape)
    g_col = g_sub.reshape(-1, 1)
    ones_rows = 2 * SUBLANES
    return pl.pallas_call(
        functools.partial(_attn_prompt_kernel, tq=tq, lam_init=lam_init),
        grid=(b, heads // nh),
        in_specs=[blk(), pl.BlockSpec((None, nh * LANES, seq), lambda i, h: (i, h, 0)), blk(),
                  lam_spec, lam_spec, lam_spec, lam_spec, _const_spec(g_col.shape)],
        out_specs=blk(),
        out_shape=jax.ShapeDtypeStruct(q4.shape, BF16),
        scratch_shapes=[pltpu.VMEM((nh, seq, LANES), BF16),
                        pltpu.VMEM((nh, LANES + ones_rows, seq), BF16)],
        compiler_params=_params(2), name="attn_prompt",
    )(q4, kt3, v4, *lams, g_col)


def _attn_sample_kernel(q_ref, kn_ref, vn_ref, kp_ref, vp_ref, lq1_ref, lk1_ref, lq2_ref,
                        lk2_ref, g_ref, o_ref, *, lam_init):
    t_new = q_ref.shape[1]
    heads = q_ref.shape[2] // LANES
    past = kp_ref.shape[2]
    lam = _diff_lambda(lq1_ref, lk1_ref, lq2_ref, lk2_ref, lam_init)
    g = g_ref[...]
    pad = jnp.zeros((LANES - t_new, LANES), BF16)
    lane = lax.broadcasted_iota(jnp.int32, (2 * t_new, LANES), 1)
    for bi, h in [(bi, h) for bi in range(q_ref.shape[0]) for h in range(heads)]:
        cols = slice(h * LANES, (h + 1) * LANES)
        qm = _stack_maps(q_ref[bi, :, cols])
        kn = jnp.concatenate([kn_ref[bi, :, cols].astype(BF16), pad], axis=0)
        vn = jnp.concatenate([vn_ref[bi, :, cols].astype(BF16), pad], axis=0)
        s_p = _dot(qm, kp_ref[bi, cols, :].astype(BF16))
        vp = vp_ref[bi, pl.ds(h, past, stride=heads), :].astype(BF16)
        s_n = jnp.where(lane < t_new, _dot_nt(qm, kn), NEG_INF)
        m = jnp.maximum(jnp.max(s_p, axis=-1, keepdims=True),
                        jnp.max(s_n, axis=-1, keepdims=True))
        e_p = jnp.exp2(s_p - m)
        e_n = jnp.exp2(s_n - m)
        l = jnp.sum(e_p, axis=-1, keepdims=True) + jnp.sum(e_n, axis=-1, keepdims=True)
        acc = _dot(e_p.astype(BF16), vp) + _dot(e_n.astype(BF16), vn)
        o_ref[bi, :, cols] = _finish_head(acc, l, lam, g, lam_init, t_new).astype(BF16)


def _attn_sample(q3, kn3, vn3, kpt3, vp4, lams, g_sub, *, lam_init, nb):
    b, t_new, width = q3.shape
    new = lambda: pl.BlockSpec((nb, t_new, width), lambda i: (i, 0, 0))
    whole = lambda a: pl.BlockSpec((nb,) + a.shape[1:], lambda i: (i, 0, 0))
    lam_spec = _const_spec(lams[0].shape)
    return pl.pallas_call(
        functools.partial(_attn_sample_kernel, lam_init=lam_init),
        grid=(b // nb,),
        in_specs=[new(), new(), new(), whole(kpt3), whole(vp4), lam_spec, lam_spec, lam_spec,
                  lam_spec, _const_spec(g_sub.shape)],
        out_specs=new(),
        out_shape=jax.ShapeDtypeStruct((b, t_new, width), BF16),
        compiler_params=_params(1), name="attn_sample",
    )(q3, kn3, vn3, kpt3, vp4, *lams, g_sub)


def _mix_and_query(x_ref, o_ref, y_ref, wout_ref, gx_ref, wxq_ref, heads):
    if len(o_ref.shape) == 3:
        o = jnp.concatenate([o_ref[h] for h in range(o_ref.shape[0])], axis=1)
    else:
        o = o_ref[...]
    aw = o.shape[-1]
    x1 = x_ref[...] + _dot(o, wout_ref[:aw, :]) + _dot(y_ref[...], wout_ref[aw:, :])
    hd = x1.shape[-1] // heads
    scale = hd ** -0.5 * math.log2(math.e)
    hq = (_dot(_rms(x1, gx_ref[...]).astype(BF16), wxq_ref[...]) * scale).astype(BF16)
    return x1, hq


def _cross_attend(hq, mk_ref, mv_ref, bi, heads, put):
    hd = hq.shape[-1] // heads
    for h in range(heads):
        cols = slice(h * hd, (h + 1) * hd)
        s = _dot_nt(hq[:, cols], _load_mem_head(mk_ref, bi, h, heads, hd))
        e = jnp.exp2(s - jnp.max(s, axis=-1, keepdims=True))
        l = jnp.sum(e, axis=-1, keepdims=True)
        xo = _dot(e.astype(BF16), _load_mem_head(mv_ref, bi, h, heads, hd)) / l
        put(cols, xo.astype(BF16))


def _post_kernel(x_ref, o_ref, y_ref, wout_ref, gx_ref, wxq_ref, mk_ref, mv_ref, wxo_ref,
                 out_ref, xo_ref, *, heads):
    x1, hq = _mix_and_query(x_ref, o_ref, y_ref, wout_ref, gx_ref, wxq_ref, heads)

    def put(cols, xo):
        xo_ref[:, cols] = xo

    _cross_attend(hq, mk_ref, mv_ref, 0, heads, put)
    out_ref[...] = x1 + _dot(xo_ref[...], wxo_ref[...])


def _post_rows_kernel(x_ref, o_ref, y_ref, wout_ref, gx_ref, wxq_ref, mk_ref, mv_ref, wxo_ref,
                      out_ref, x1_ref, hq_ref, xo_ref, *, heads, seq_len):
    b = pl.program_id(0)
    nb = mk_ref.shape[0]

    @pl.when(b == 0)
    def _():
        x1_ref[...], hq_ref[...] = _mix_and_query(x_ref, o_ref, y_ref, wout_ref, gx_ref,
                                                  wxq_ref, heads)

    for bi in range(nb):
        rows = pl.ds(pl.multiple_of((b * nb + bi) * seq_len, seq_len), seq_len)

        def put(cols, xo, rows=rows):
            xo_ref[rows, cols] = xo

        _cross_attend(hq_ref[rows, :], mk_ref, mv_ref, bi, heads, put)

    @pl.when(b == pl.num_programs(0) - 1)
    def _():
        out_ref[...] = x1_ref[...] + _dot(xo_ref[...], wxo_ref[...])


def _post_rows(x2d, o2d, y2d, w_out, g_x, w_xq, mk3, mv3, w_xo, *, batch, seq_len, heads, nb):
    n, d = x2d.shape
    mem = pl.BlockSpec((nb,) + mk3.shape[1:], lambda b: (b, 0, 0))
    return pl.pallas_call(
        functools.partial(_post_rows_kernel, heads=heads, seq_len=seq_len),
        grid=(batch // nb,),
        in_specs=[_const_spec(x2d.shape), _const_spec(o2d.shape), _const_spec(y2d.shape),
                  _const_spec(w_out.shape), _const_spec((1, d)), _const_spec(w_xq.shape),
                  mem, mem, _const_spec(w_xo.shape)],
        out_specs=pl.BlockSpec((n, d), lambda b: (0, 0)),
        out_shape=jax.ShapeDtypeStruct((n, d), F32),
        scratch_shapes=[pltpu.VMEM((n, d), F32), pltpu.VMEM((n, d), BF16),
                        pltpu.VMEM((n, d), BF16)],
        compiler_params=_params(1), name="post_sample",
    )(x2d, o2d, y2d, w_out, g_x, w_xq, mk3, mv3, w_xo)


def _post(x2d, o4, y2d, w_out, g_x, w_xq, mk3, mv3, w_xo, *, batch, seq_len, tm, heads):
    n, d = x2d.shape
    cw = y2d.shape[-1]
    nj = seq_len // tm
    row = lambda b, j: (b * nj + j, 0)
    mem = pl.BlockSpec((1,) + mk3.shape[1:], lambda b, j: (b, 0, 0))
    o_spec = pl.BlockSpec((None, o4.shape[1], tm, LANES), lambda b, j: (b, 0, j, 0))
    return pl.pallas_call(
        functools.partial(_post_kernel, heads=heads),
        grid=(batch, nj),
        in_specs=[pl.BlockSpec((tm, d), row), o_spec,
                  pl.BlockSpec((tm, cw), row), _const_spec(w_out.shape), _const_spec((1, d)),
                  _const_spec(w_xq.shape), mem, mem, _const_spec(w_xo.shape)],
        out_specs=pl.BlockSpec((tm, d), row),
        out_shape=jax.ShapeDtypeStruct((n, d), F32),
        scratch_shapes=[pltpu.VMEM((tm, d), BF16)],
        compiler_params=_params(2), name="post",
    )(x2d, o4, y2d, w_out, g_x, w_xq, mk3, mv3, w_xo)


FF_CHUNK = 1024


def _ffn_body(x_ref, g_ref, wup_ref, wgate_ref, wconv_ref, wdown_ref, gfin_ref, y_ref,
              a_ref, fix, emit_up, final_norm):
    x = x_ref[...]
    hf = _rms(x, g_ref[...]).astype(BF16)
    dff = wup_ref.shape[-1]
    for c0 in range(0, dff, FF_CHUNK):
        cols = slice(c0, min(c0 + FF_CHUNK, dff))
        up = _dot(hf, wup_ref[:, cols])
        emit_up(cols, up)
        uc = _conv3(up, wconv_ref[:, cols], fix(cols, up.shape))
        gate = _dot(hf, wgate_ref[:, cols])
        a_ref[:, cols] = (uc * jax.nn.sigmoid(uc) * gate).astype(BF16)
    half = x.shape[0] // 2
    for r0 in (0, half):
        x3 = x[r0:r0 + half] + _dot(a_ref[r0:r0 + half, :], wdown_ref[...])
        y_ref[r0:r0 + half, :] = _rms(x3, gfin_ref[...]) if final_norm else x3


def _ffn_carry_kernel(x_ref, g_ref, wup_ref, wgate_ref, wconv_ref, wdown_ref, gfin_ref,
                      y_ref, tail_ref, a_ref, carry_ref, new_ref, *, final_norm):
    @pl.when(pl.program_id(1) == 0)
    def _():
        carry_ref[...] = jnp.zeros_like(carry_ref)

    def emit_up(cols, up):
        new_ref[:, cols] = up[up.shape[0] - SUBLANES:, :]

    fix = lambda cols, shape: _carry_fix(carry_ref, cols, shape)
    _ffn_body(x_ref, g_ref, wup_ref, wgate_ref, wconv_ref, wdown_ref, gfin_ref, y_ref,
              a_ref, fix, emit_up, final_norm)
    carry_ref[...] = new_ref[...]
    tail_ref[0] = new_ref[...]


def _ffn_state_kernel(x_ref, g_ref, wup_ref, wgate_ref, wconv_ref, wdown_ref, gfin_ref,
                      sp1_ref, sp2_ref, y_ref, up_ref, a_ref, *, seq_len, final_norm):
    def emit_up(cols, up):
        up_ref[:, cols] = up

    fix = lambda cols, shape: _state_fix(sp1_ref[:, cols], sp2_ref[:, cols], seq_len, shape)
    _ffn_body(x_ref, g_ref, wup_ref, wgate_ref, wconv_ref, wdown_ref, gfin_ref, y_ref,
              a_ref, fix, emit_up, final_norm)


def _ffn(x2d, g_ffn, w_up, w_gate, w_conv, w_down, g_final, *, batch, seq_len, tm, final_norm,
         state=None):
    n, d = x2d.shape
    dff = w_up.shape[-1]
    weights = [_const_spec((1, d)), _const_spec(w_up.shape), _const_spec(w_gate.shape),
               _const_spec(w_conv.shape), _const_spec(w_down.shape), _const_spec((1, d))]
    if state is None:
        nj = seq_len // tm
        row = lambda b, j: (b * nj + j, 0)
        return pl.pallas_call(
            functools.partial(_ffn_carry_kernel, final_norm=final_norm),
            grid=(batch, nj),
            in_specs=[pl.BlockSpec((tm, d), row)] + weights,
            out_specs=[pl.BlockSpec((tm, d), row),
                       pl.BlockSpec((1, SUBLANES, dff), lambda b, j: (b, 0, 0))],
            out_shape=[jax.ShapeDtypeStruct((n, d), F32),
                       jax.ShapeDtypeStruct((batch, SUBLANES, dff), F32)],
            scratch_shapes=[pltpu.VMEM((tm, dff), BF16), pltpu.VMEM((SUBLANES, dff), F32),
                            pltpu.VMEM((SUBLANES, dff), F32)],
            compiler_params=_params(2), name="ffn_prompt",
        )(x2d, g_ffn, w_up, w_gate, w_conv, w_down, g_final)
    sp1, sp2 = state
    row = lambda i: (i, 0)
    return pl.pallas_call(
        functools.partial(_ffn_state_kernel, seq_len=seq_len, final_norm=final_norm),
        grid=(n // tm,),
        in_specs=[pl.BlockSpec((tm, d), row)] + weights + [pl.BlockSpec((tm, dff), row)] * 2,
        out_specs=[pl.BlockSpec((tm, d), row), pl.BlockSpec((tm, dff), row)],
        out_shape=[jax.ShapeDtypeStruct((n, d), F32), jax.ShapeDtypeStruct((n, dff), F32)],
        scratch_shapes=[pltpu.VMEM((tm, dff), BF16)],
        compiler_params=_params(1), name="ffn_sample",
    )(x2d, g_ffn, w_up, w_gate, w_conv, w_down, g_final, sp1, sp2)


def _rope_tables(pos, qk_head_dim):
    rot = qk_head_dim // 4
    half = rot // 2
    inv = 1.0 / (ROPE_THETA ** (jnp.arange(half, dtype=F32) * 2.0 / rot))
    ang = pos.astype(F32)[:, None] * inv[None, :]
    cos, sin = jnp.cos(ang), jnp.sin(ang)
    t = pos.shape[0]
    zeros = lambda w: jnp.zeros((t, w), F32)
    c = jnp.concatenate([cos, cos, jnp.ones((t, qk_head_dim - rot), F32)], axis=1)
    a = jnp.concatenate([-sin, zeros(qk_head_dim - half)], axis=1)
    b = jnp.concatenate([zeros(half), sin, zeros(qk_head_dim - rot)], axis=1)
    rep = LANES // qk_head_dim
    return tuple(jnp.tile(m, (1, rep)) for m in (c, a, b))


def _expand_state(state, seq_len):
    b, k, c = state.shape
    sp1 = jnp.pad(state[:, k - 1:], ((0, 0), (0, seq_len - 1), (0, 0)))
    sp2 = jnp.pad(state, ((0, 0), (0, seq_len - k), (0, 0)))
    return sp1.reshape(b * seq_len, c), sp2.reshape(b * seq_len, c)


def kernel(x_prompt, x_sample, cache_attn_k, cache_attn_v, state_short_conv, state_ffn_conv,
           cache_mem_k, cache_mem_v, mem_prompt, g_mix, w_in, lam_q1, lam_k1, lam_q2, lam_k2,
           g_sub, w_sc, w_out, g_mem, g_x, w_xq, w_xk, w_xv, w_xo, g_ffn, w_up, w_gate,
           w_ffconv, w_down, g_final):
    depth = w_in.shape[0]
    bp, seq, d = x_prompt.shape
    bs, t_new, _ = x_sample.shape
    past = cache_attn_k.shape[2]
    heads_a, qk_dim = cache_attn_k.shape[3], cache_attn_k.shape[5]
    v_dim = cache_attn_v.shape[4]
    cw = state_short_conv.shape[-1]
    dff = state_ffn_conv.shape[-1]
    n_mem, heads_x, x_dim = cache_mem_k.shape[2:]
    qc = kc = heads_a * 2 * qk_dim
    vc = heads_a * v_dim
    dims = (qc, kc, vc, cw, qk_dim ** -0.5 * math.log2(math.e), qk_dim // 8)
    assert 2 * qk_dim == LANES and v_dim == LANES
    assert Q_TILE % CHUNK == 0 and CHUNK & (CHUNK - 1) == 0 and Q_TILE & (Q_TILE - 1) == 0

    tabs_p = _rope_tables(jnp.arange(seq, dtype=jnp.int32), qk_dim)
    tabs_s = tuple(jnp.tile(m, (bs, 1)) for m in
                   _rope_tables(past + jnp.arange(t_new, dtype=jnp.int32), qk_dim))
    row = lambda v: v.reshape(1, -1)

    hp = x_prompt.reshape(bp * seq, d)
    hs = x_sample.reshape(bs * t_new, d)
    outs_p = [[] for _ in range(6)]
    outs_s = [[] for _ in range(4)]
    for l in range(depth):
        lam_init = _lambda_init(l)
        wi, wo = w_in[l].astype(BF16), w_out[l].astype(BF16)
        wq, wxo_b = w_xq[l].astype(BF16), w_xo[l].astype(BF16)
        wkv = jnp.concatenate([w_xk[l], w_xv[l]], axis=1).astype(BF16)
        wu, wg, wd = w_up[l].astype(BF16), w_gate[l].astype(BF16), w_down[l].astype(BF16)
        lams = tuple(row(v[l]) for v in (lam_q1, lam_k1, lam_q2, lam_k2))
        gsub = row(g_sub[l])

        mk8, mv8, mk, mv = _mem_kv(mem_prompt.reshape(bp * n_mem, d), row(g_mem[l]), wkv,
                                   MEM_TILE, heads_x)
        q, kt, v4, vb, ysc, sc_tail = _in_proj(hp, row(g_mix[l]), wi, tabs_p, w_sc[l], dims,
                                               batch=bp, seq_len=seq, tm=PROJ_TILE)
        o = _attn_prompt(q, kt, vb, lams, gsub, tq=Q_TILE, lam_init=lam_init)
        x2 = _post(hp, o, ysc, wo, row(g_x[l]), wq,
                   mk.reshape(bp, n_mem, d), mv.reshape(bp, n_mem, d), wxo_b,
                   batch=bp, seq_len=seq, tm=POST_TILE, heads=heads_x)
        hp, ff_tail = _ffn(x2, row(g_ffn[l]), wu, wg, w_ffconv[l], wd, row(g_final),
                           batch=bp, seq_len=seq, tm=FFN_TILE, final_norm=l == depth - 1)
        outs_p[0].append(jnp.transpose(kt.reshape(bp, heads_a, 2, qk_dim, seq), (0, 4, 1, 2, 3)))
        outs_p[1].append(v4.reshape(bp, seq, heads_a, v_dim))
        outs_p[2].append(sc_tail[:, SUBLANES - 2:])
        outs_p[3].append(ff_tail[:, SUBLANES - 2:])
        outs_p[4].append(_from_mem_layout(mk8, bp, n_mem, heads_x, x_dim))
        outs_p[5].append(_from_mem_layout(mv8, bp, n_mem, heads_x, x_dim))

        n_s = bs * t_new
        qs, ks, vs, yscs, u_s = _in_proj(
            hs, row(g_mix[l]), wi, tabs_s, w_sc[l], dims, batch=bs, seq_len=t_new, tm=n_s,
            state=_expand_state(state_short_conv[l], t_new))
        o_s = _attn_sample(qs.reshape(bs, t_new, qc), ks.reshape(bs, t_new, kc),
                           vs.reshape(bs, t_new, vc),
                           jnp.transpose(cache_attn_k[l].reshape(bs, past, kc), (0, 2, 1)),
                           cache_attn_v[l].reshape(bs, past * heads_a, v_dim), lams, gsub,
                           lam_init=lam_init, nb=SAMPLE_ATTN_SEQS)
        x2s = _post_rows(hs, o_s.reshape(n_s, vc), yscs, wo, row(g_x[l]), wq,
                         _to_mem_layout(cache_mem_k[l], x_dim // LANES),
                         _to_mem_layout(cache_mem_v[l], x_dim // LANES), wxo_b,
                         batch=bs, seq_len=t_new, heads=heads_x, nb=SAMPLE_POST_SEQS)
        hs, up_s = _ffn(x2s, row(g_ffn[l]), wu, wg, w_ffconv[l], wd, row(g_final),
                        batch=bs, seq_len=t_new, tm=n_s, final_norm=l == depth - 1,
                        state=_expand_state(state_ffn_conv[l], t_new))
        outs_s[0].append(ks.reshape(bs, t_new, heads_a, 2, qk_dim))
        outs_s[1].append(vs.reshape(bs, t_new, heads_a, v_dim))
        outs_s[2].append(u_s.reshape(bs, t_new, cw)[:, t_new - 2:])
        outs_s[3].append(up_s.reshape(bs, t_new, dff)[:, t_new - 2:])

    return (hp.reshape(bp, seq, d), hs.reshape(bs, t_new, d),
            *(jnp.stack(o) for o in outs_p), *(jnp.stack(o) for o in outs_s))
```

```python
import functools
import math

import jax
import jax.numpy as jnp
from jax import lax
from jax.experimental import pallas as pl
from jax.experimental.pallas import tpu as pltpu

EPS = 1e-6
CHUNK = 64
ROPE_THETA = 500000.0
LANES = 128
SUBLANES = 8
VMEM_LIMIT = 56 * 1024 * 1024
MEM_TILE = 1024
PROJ_TILE = 1024
POST_TILE = 1024
FFN_TILE = 1024
Q_TILE = 128
ATTN_HEADS_PER_STEP = 4
ATTN_LOOKAHEAD = 2
SAMPLE_ATTN_SEQS = 4
SAMPLE_POST_SEQS = 4
BF16 = jnp.bfloat16
F32 = jnp.float32
NEG_INF = float("-inf")


def _lambda_init(layer_idx):
    return 0.8 - 0.6 * math.exp(-0.3 * layer_idx)


def _rms(x, g):
    return x * lax.rsqrt(jnp.mean(x * x, axis=-1, keepdims=True) + EPS) * g


def _dot(a, b):
    return jnp.dot(a, b, preferred_element_type=F32)


def _dot_nt(a, b):
    return lax.dot_general(a, b, (((1,), (1,)), ((), ())), preferred_element_type=F32)


def _params(n_grid):
    return pltpu.CompilerParams(dimension_semantics=("arbitrary",) * n_grid,
                                vmem_limit_bytes=VMEM_LIMIT)


def _const_spec(shape):
    nd = len(shape)
    return pl.BlockSpec(shape, lambda *_: (0,) * nd, pipeline_mode=pl.Buffered(1))


def _conv3(u, w, fix):
    p1 = pltpu.roll(u, 1, axis=0)
    p2 = pltpu.roll(u, 2, axis=0)
    p1, p2 = fix(p1, p2)
    return w[0:1, :] * p2 + w[1:2, :] * p1 + w[2:3, :] * u


def _carry_fix(carry_ref, cols, shape):
    row = lax.broadcasted_iota(jnp.int32, shape, 0)
    c0 = carry_ref[SUBLANES - 2:SUBLANES - 1, cols]
    c1 = carry_ref[SUBLANES - 1:SUBLANES, cols]

    def fix(p1, p2):
        p1 = jnp.where(row == 0, c1, p1)
        p2 = jnp.where(row == 0, c0, jnp.where(row == 1, c1, p2))
        return p1, p2
    return fix


def _state_fix(sp1, sp2, seq_len, shape):
    row = lax.broadcasted_iota(jnp.int32, shape, 0)
    t = lax.rem(row, seq_len)

    def fix(p1, p2):
        return jnp.where(t == 0, sp1, p1), jnp.where(t < 2, sp2, p2)
    return fix


def _store_mem_layout(ref, blk, heads):
    rows, width = blk.shape
    lt = width // heads // LANES
    for h in range(heads):
        for t in range(lt):
            c0 = (h * lt + t) * LANES
            ref[pl.ds(t * heads + h, rows, stride=heads * lt), :] = blk[:, c0:c0 + LANES]


def _load_mem_head(ref, bi, h, heads, hd):
    if ref.shape[-1] != LANES:
        return ref[bi, :, h * hd:(h + 1) * hd]
    lt = hd // LANES
    rows = ref.shape[1] // (heads * lt)
    parts = [ref[bi, pl.ds(t * heads + h, rows, stride=heads * lt), :] for t in range(lt)]
    return jnp.concatenate(parts, axis=1).astype(BF16)


def _to_mem_layout(x, lt):
    b, n, heads, hd = x.shape
    x = jnp.transpose(x.reshape(b, n, heads, lt, LANES), (0, 1, 3, 2, 4))
    return x.reshape(b, n * heads * lt, LANES)


def _from_mem_layout(x8, b, n, heads, hd):
    lt = hd // LANES
    x = jnp.transpose(x8.reshape(b, n, lt, heads, LANES), (0, 1, 3, 2, 4))
    return x.reshape(b, n, heads, hd)


def _memkv_kernel(m_ref, g_ref, w_ref, k8_ref, v8_ref, kb_ref, vb_ref, *, heads):
    d = kb_ref.shape[-1]
    h = _rms(m_ref[...], g_ref[...]).astype(BF16)
    kv = _dot(h, w_ref[...])
    for part, (o8, ob) in enumerate(((k8_ref, kb_ref), (v8_ref, vb_ref))):
        blk = kv[:, part * d:(part + 1) * d]
        ob[...] = blk.astype(BF16)
        _store_mem_layout(o8, blk, heads)


def _mem_kv(mem2d, g_mem, w_xkv, tm, heads):
    n, d = mem2d.shape
    group = d // LANES
    row = lambda i: (i, 0)
    return pl.pallas_call(
        functools.partial(_memkv_kernel, heads=heads),
        grid=(n // tm,),
        in_specs=[pl.BlockSpec((tm, d), row), _const_spec((1, d)), _const_spec(w_xkv.shape)],
        out_specs=[pl.BlockSpec((tm * group, LANES), row)] * 2 + [pl.BlockSpec((tm, d), row)] * 2,
        out_shape=[jax.ShapeDtypeStruct((n * group, LANES), F32)] * 2
        + [jax.ShapeDtypeStruct((n, d), BF16)] * 2,
        compiler_params=_params(1),
        name="mem_kv",
    )(mem2d, g_mem, w_xkv)


def _inproj_body(x_ref, g_ref, w_ref, cos_ref, sa_ref, sb_ref, wsc_ref,
                 put_q, put_k, put_v, y_ref, fix, dims):
    qc, kc, vc, cw, q_scale, rot_half = dims
    h = _rms(x_ref[...], g_ref[...]).astype(BF16)
    cos, sa, sb = cos_ref[...], sa_ref[...], sb_ref[...]

    def rope(t):
        return (t * cos + pltpu.roll(t, LANES - rot_half, axis=1) * sa
                + pltpu.roll(t, rot_half, axis=1) * sb)

    o = qc + kc
    cx = _dot(h, w_ref[:, o + vc + cw:])
    u = cx[:, :cw] * cx[:, cw:]
    conv = _conv3(u, wsc_ref[...], fix(u.shape))
    qk = _dot(h, w_ref[:, :qc + kc])
    for c in range(qc // LANES):
        blk = rope(qk[:, c * LANES:(c + 1) * LANES])
        put_q(c, (blk * q_scale).astype(BF16))
    for c in range(kc // LANES):
        lo = qc + c * LANES
        put_k(c, rope(qk[:, lo:lo + LANES]))
    put_v(_dot(h, w_ref[:, o:o + vc]))
    bg = _dot(h, w_ref[:, o + vc:o + vc + cw])
    y_ref[...] = (bg * conv).astype(BF16)
    return u


def _inproj_carry_kernel(x_ref, g_ref, w_ref, cos_ref, sa_ref, sb_ref, wsc_ref,
                         q_ref, kt_ref, v4_ref, vb_ref, y_ref, tail_ref, carry_ref, *, dims):
    @pl.when(pl.program_id(1) == 0)
    def _():
        carry_ref[...] = jnp.zeros_like(carry_ref)

    def put_q(c, blk):
        q_ref[c] = blk

    def put_k(c, blk):
        kt_ref[c * LANES:(c + 1) * LANES, :] = blk.T

    def put_v(v):
        heads = v.shape[1] // LANES
        for hd in range(heads):
            vh = v[:, hd * LANES:(hd + 1) * LANES]
            vb_ref[hd] = vh.astype(BF16)
            v4_ref[pl.ds(hd, v.shape[0], stride=heads), :] = vh

    fix = lambda shape: _carry_fix(carry_ref, slice(None), shape)
    u = _inproj_body(x_ref, g_ref, w_ref, cos_ref, sa_ref, sb_ref, wsc_ref,
                     put_q, put_k, put_v, y_ref, fix, dims)
    last = u[u.shape[0] - SUBLANES:, :]
    carry_ref[...] = last
    tail_ref[0] = last


def _inproj_state_kernel(x_ref, g_ref, w_ref, cos_ref, sa_ref, sb_ref, wsc_ref,
                         sp1_ref, sp2_ref, q_ref, k_ref, v_ref, y_ref, u_ref,
                         *, dims, seq_len):
    def put_q(c, blk):
        q_ref[:, c * LANES:(c + 1) * LANES] = blk

    def put_k(c, blk):
        k_ref[:, c * LANES:(c + 1) * LANES] = blk

    def put_v(v):
        v_ref[...] = v

    fix = lambda shape: _state_fix(sp1_ref[...], sp2_ref[...], seq_len, shape)
    u_ref[...] = _inproj_body(x_ref, g_ref, w_ref, cos_ref, sa_ref, sb_ref, wsc_ref,
                              put_q, put_k, put_v, y_ref, fix, dims)


def _in_proj(x2d, g, w, tabs, w_sc, dims, *, batch, seq_len, tm, state=None):
    n, d = x2d.shape
    qc, kc, vc, cw = dims[:4]
    common_in = [None, _const_spec((1, d)), _const_spec(w.shape), None, None, None,
                 _const_spec(w_sc.shape)]
    if state is None:
        nj = seq_len // tm
        heads = vc // LANES
        row = lambda b, j: (b * nj + j, 0)
        tab = pl.BlockSpec((tm, LANES), lambda b, j: (j, 0))
        in_specs = list(common_in)
        in_specs[0] = pl.BlockSpec((tm, d), row)
        in_specs[3:6] = [tab, tab, tab]
        per_head = pl.BlockSpec((None, heads, tm, LANES), lambda b, j: (b, 0, j, 0))
        out_specs = [per_head,
                     pl.BlockSpec((None, kc, tm), lambda b, j: (b, 0, j)),
                     pl.BlockSpec((tm * heads, LANES), row),
                     per_head, pl.BlockSpec((tm, cw), row),
                     pl.BlockSpec((1, SUBLANES, cw), lambda b, j: (b, 0, 0))]
        out_shape = [jax.ShapeDtypeStruct((batch, heads, seq_len, LANES), BF16),
                     jax.ShapeDtypeStruct((batch, kc, seq_len), F32),
                     jax.ShapeDtypeStruct((n * heads, LANES), F32),
                     jax.ShapeDtypeStruct((batch, heads, seq_len, LANES), BF16),
                     jax.ShapeDtypeStruct((n, cw), BF16),
                     jax.ShapeDtypeStruct((batch, SUBLANES, cw), F32)]
        return pl.pallas_call(
            functools.partial(_inproj_carry_kernel, dims=dims),
            grid=(batch, nj), in_specs=in_specs, out_specs=out_specs, out_shape=out_shape,
            scratch_shapes=[pltpu.VMEM((SUBLANES, cw), F32)],
            compiler_params=_params(2), name="in_proj_prompt",
        )(x2d, g, w, *tabs, w_sc)
    sp1, sp2 = state
    row = lambda i: (i, 0)
    tab = pl.BlockSpec((tm, LANES), row)
    in_specs = list(common_in)
    in_specs[0] = pl.BlockSpec((tm, d), row)
    in_specs[3:6] = [tab, tab, tab]
    in_specs += [pl.BlockSpec((tm, cw), row)] * 2
    out_specs = [pl.BlockSpec((tm, c), row) for c in (qc, kc, vc, cw, cw)]
    out_shape = [jax.ShapeDtypeStruct((n, qc), BF16), jax.ShapeDtypeStruct((n, kc), F32),
                 jax.ShapeDtypeStruct((n, vc), F32), jax.ShapeDtypeStruct((n, cw), BF16),
                 jax.ShapeDtypeStruct((n, cw), F32)]
    return pl.pallas_call(
        functools.partial(_inproj_state_kernel, dims=dims, seq_len=seq_len),
        grid=(n // tm,), in_specs=in_specs, out_specs=out_specs, out_shape=out_shape,
        compiler_params=_params(1), name="in_proj_sample",
    )(x2d, g, w, *tabs, w_sc, sp1, sp2)


def _diff_lambda(lq1_ref, lk1_ref, lq2_ref, lk2_ref, lam_init):
    a = jnp.sum(lq1_ref[...] * lk1_ref[...], axis=-1, keepdims=True)
    b = jnp.sum(lq2_ref[...] * lk2_ref[...], axis=-1, keepdims=True)
    return jnp.exp(a) - jnp.exp(b) + lam_init


def _stack_maps(qt):
    lane = lax.broadcasted_iota(jnp.int32, qt.shape, 1)
    first = lane < (LANES // 2)
    zero = jnp.zeros_like(qt)
    return jnp.concatenate([jnp.where(first, qt, zero), jnp.where(first, zero, qt)], axis=0)


def _finish_head(acc, l, lam, g, lam_init, tq):
    o = acc[:tq] / l[:tq] - lam * (acc[tq:] / l[tq:])
    return _rms(o, g) * (1.0 - lam_init)


def _attn_prompt_kernel(q_ref, k_ref, v_ref, lq1_ref, lk1_ref, lq2_ref, lk2_ref, g_ref,
                        o_ref, kb_ref, vt_ref, *, tq, lam_init):
    nh, seq, hd = v_ref.shape
    for h in range(nh):
        kb_ref[h] = k_ref[h * hd:(h + 1) * hd, :].T.astype(BF16)
        vt_ref[h, :hd, :] = v_ref[h].astype(F32).T.astype(BF16)
        vt_ref[h, hd:, :] = jnp.ones((vt_ref.shape[1] - hd, seq), BF16)
    lam = _diff_lambda(lq1_ref, lk1_ref, lq2_ref, lk2_ref, lam_init)
    g = g_ref[...]
    r = lax.broadcasted_iota(jnp.int32, (tq, 2 * tq), 0)
    c = lax.broadcasted_iota(jnp.int32, (tq, 2 * tq), 1)
    shift = CHUNK.bit_length() - 1
    diag_bias = jnp.where((r >> shift) <= ((c & (tq - 1)) >> shift), 0.0, NEG_INF)

    nq = seq // tq
    order = list(range(0, nq, 2)) + list(range(nq - 1 - (nq % 2), 0, -2))
    def scores(h, i):
        lo = i * tq
        qm = _stack_maps(q_ref[h, lo:lo + tq, :])
        s_d = _dot_nt(kb_ref[h, lo:lo + tq, :], qm) + diag_bias
        m = jnp.max(s_d, axis=0, keepdims=True)
        s_o = None
        if i:
            s_o = _dot_nt(kb_ref[h, :lo, :], qm)
            m = jnp.maximum(m, jnp.max(s_o, axis=0, keepdims=True))
        return s_d, s_o, m

    def finish(h, i, s_d, s_o, m):
        lo = i * tq
        acc = _dot(vt_ref[h, :, lo:lo + tq], jnp.exp2(s_d - m).astype(BF16))
        if i:
            acc = acc + _dot(vt_ref[h, :, :lo], jnp.exp2(s_o - m).astype(BF16))
        num, l = acc[:hd, :], acc[hd:hd + 1, :]
        ot = num[:, :tq] / l[:, :tq] - lam * (num[:, tq:] / l[:, tq:])
        ot = ot * lax.rsqrt(jnp.mean(ot * ot, axis=0, keepdims=True) + EPS) * g
        o_ref[h, lo:lo + tq, :] = (ot * (1.0 - lam_init)).T.astype(BF16)

    tasks = [(h, i) for i in order for h in range(nh)]
    pending = [scores(*t) for t in tasks[:ATTN_LOOKAHEAD]]
    for n, t in enumerate(tasks):
        if n + ATTN_LOOKAHEAD < len(tasks):
            pending.append(scores(*tasks[n + ATTN_LOOKAHEAD]))
        finish(*t, *pending.pop(0))


def _attn_prompt(q4, kt3, v4, lams, g_sub, *, tq, lam_init):
    b, heads, seq, _ = q4.shape
    nh = ATTN_HEADS_PER_STEP
    blk = lambda: pl.BlockSpec((None, nh, seq, LANES), lambda i, h: (i, h, 0, 0))
    lam_spec = _const_spec(lams[0].shape)
    g_col = g_sub.reshape(-1, 1)
    ones_rows = 2 * SUBLANES
    return pl.pallas_call(
        functools.partial(_attn_prompt_kernel, tq=tq, lam_init=lam_init),
        grid=(b, heads // nh),
        in_specs=[blk(), pl.BlockSpec((None, nh * LANES, seq), lambda i, h: (i, h, 0)), blk(),
                  lam_spec, lam_spec, lam_spec, lam_spec, _const_spec(g_col.shape)],
        out_specs=blk(),
        out_shape=jax.ShapeDtypeStruct(q4.shape, BF16),
        scratch_shapes=[pltpu.VMEM((nh, seq, LANES), BF16),
                        pltpu.VMEM((nh, LANES + ones_rows, seq), BF16)],
        compiler_params=_params(2), name="attn_prompt",
    )(q4, kt3, v4, *lams, g_col)


def _attn_sample_kernel(q_ref, kn_ref, vn_ref, kp_ref, vp_ref, lq1_ref, lk1_ref, lq2_ref,
                        lk2_ref, g_ref, o_ref, *, lam_init):
    t_new = q_ref.shape[1]
    heads = q_ref.shape[2] // LANES
    past = kp_ref.shape[2]
    lam = _diff_lambda(lq1_ref, lk1_ref, lq2_ref, lk2_ref, lam_init)
    g = g_ref[...]
    pad = jnp.zeros((LANES - t_new, LANES), BF16)
    lane = lax.broadcasted_iota(jnp.int32, (2 * t_new, LANES), 1)
    for bi, h in [(bi, h) for bi in range(q_ref.shape[0]) for h in range(heads)]:
        cols = slice(h * LANES, (h + 1) * LANES)
        qm = _stack_maps(q_ref[bi, :, cols])
        kn = jnp.concatenate([kn_ref[bi, :, cols].astype(BF16), pad], axis=0)
        vn = jnp.concatenate([vn_ref[bi, :, cols].astype(BF16), pad], axis=0)
        s_p = _dot(qm, kp_ref[bi, cols, :].astype(BF16))
        vp = vp_ref[bi, pl.ds(h, past, stride=heads), :].astype(BF16)
        s_n = jnp.where(lane < t_new, _dot_nt(qm, kn), NEG_INF)
        m = jnp.maximum(jnp.max(s_p, axis=-1, keepdims=True),
                        jnp.max(s_n, axis=-1, keepdims=True))
        e_p = jnp.exp2(s_p - m)
        e_n = jnp.exp2(s_n - m)
        l = jnp.sum(e_p, axis=-1, keepdims=True) + jnp.sum(e_n, axis=-1, keepdims=True)
        acc = _dot(e_p.astype(BF16), vp) + _dot(e_n.astype(BF16), vn)
        o_ref[bi, :, cols] = _finish_head(acc, l, lam, g, lam_init, t_new).astype(BF16)


def _attn_sample(q3, kn3, vn3, kpt3, vp4, lams, g_sub, *, lam_init, nb):
    b, t_new, width = q3.shape
    new = lambda: pl.BlockSpec((nb, t_new, width), lambda i: (i, 0, 0))
    whole = lambda a: pl.BlockSpec((nb,) + a.shape[1:], lambda i: (i, 0, 0))
    lam_spec = _const_spec(lams[0].shape)
    return pl.pallas_call(
        functools.partial(_attn_sample_kernel, lam_init=lam_init),
        grid=(b // nb,),
        in_specs=[new(), new(), new(), whole(kpt3), whole(vp4), lam_spec, lam_spec, lam_spec,
                  lam_spec, _const_spec(g_sub.shape)],
        out_specs=new(),
        out_shape=jax.ShapeDtypeStruct((b, t_new, width), BF16),
        compiler_params=_params(1), name="attn_sample",
    )(q3, kn3, vn3, kpt3, vp4, *lams, g_sub)


def _mix_and_query(x_ref, o_ref, y_ref, wout_ref, gx_ref, wxq_ref, heads):
    if len(o_ref.shape) == 3:
        o = jnp.concatenate([o_ref[h] for h in range(o_ref.shape[0])], axis=1)
    else:
        o = o_ref[...]
    aw = o.shape[-1]
    x1 = x_ref[...] + _dot(o, wout_ref[:aw, :]) + _dot(y_ref[...], wout_ref[aw:, :])
    hd = x1.shape[-1] // heads
    scale = hd ** -0.5 * math.log2(math.e)
    hq = (_dot(_rms(x1, gx_ref[...]).astype(BF16), wxq_ref[...]) * scale).astype(BF16)
    return x1, hq


def _cross_attend(hq, mk_ref, mv_ref, bi, heads, put):
    hd = hq.shape[-1] // heads
    for h in range(heads):
        cols = slice(h * hd, (h + 1) * hd)
        s = _dot_nt(hq[:, cols], _load_mem_head(mk_ref, bi, h, heads, hd))
        e = jnp.exp2(s - jnp.max(s, axis=-1, keepdims=True))
        l = jnp.sum(e, axis=-1, keepdims=True)
        xo = _dot(e.astype(BF16), _load_mem_head(mv_ref, bi, h, heads, hd)) / l
        put(cols, xo.astype(BF16))


def _post_kernel(x_ref, o_ref, y_ref, wout_ref, gx_ref, wxq_ref, mk_ref, mv_ref, wxo_ref,
                 out_ref, xo_ref, *, heads):
    x1, hq = _mix_and_query(x_ref, o_ref, y_ref, wout_ref, gx_ref, wxq_ref, heads)

    def put(cols, xo):
        xo_ref[:, cols] = xo

    _cross_attend(hq, mk_ref, mv_ref, 0, heads, put)
    out_ref[...] = x1 + _dot(xo_ref[...], wxo_ref[...])


def _post_rows_kernel(x_ref, o_ref, y_ref, wout_ref, gx_ref, wxq_ref, mk_ref, mv_ref, wxo_ref,
                      out_ref, x1_ref, hq_ref, xo_ref, *, heads, seq_len):
    b = pl.program_id(0)
    nb = mk_ref.shape[0]

    @pl.when(b == 0)
    def _():
        x1_ref[...], hq_ref[...] = _mix_and_query(x_ref, o_ref, y_ref, wout_ref, gx_ref,
                                                  wxq_ref, heads)

    for bi in range(nb):
        rows = pl.ds(pl.multiple_of((b * nb + bi) * seq_len, seq_len), seq_len)

        def put(cols, xo, rows=rows):
            xo_ref[rows, cols] = xo

        _cross_attend(hq_ref[rows, :], mk_ref, mv_ref, bi, heads, put)

    @pl.when(b == pl.num_programs(0) - 1)
    def _():
        out_ref[...] = x1_ref[...] + _dot(xo_ref[...], wxo_ref[...])


def _post_rows(x2d, o2d, y2d, w_out, g_x, w_xq, mk3, mv3, w_xo, *, batch, seq_len, heads, nb):
    n, d = x2d.shape
    mem = pl.BlockSpec((nb,) + mk3.shape[1:], lambda b: (b, 0, 0))
    return pl.pallas_call(
        functools.partial(_post_rows_kernel, heads=heads, seq_len=seq_len),
        grid=(batch // nb,),
        in_specs=[_const_spec(x2d.shape), _const_spec(o2d.shape), _const_spec(y2d.shape),
                  _const_spec(w_out.shape), _const_spec((1, d)), _const_spec(w_xq.shape),
                  mem, mem, _const_spec(w_xo.shape)],
        out_specs=pl.BlockSpec((n, d), lambda b: (0, 0)),
        out_shape=jax.ShapeDtypeStruct((n, d), F32),
        scratch_shapes=[pltpu.VMEM((n, d), F32), pltpu.VMEM((n, d), BF16),
                        pltpu.VMEM((n, d), BF16)],
        compiler_params=_params(1), name="post_sample",
    )(x2d, o2d, y2d, w_out, g_x, w_xq, mk3, mv3, w_xo)


def _post(x2d, o4, y2d, w_out, g_x, w_xq, mk3, mv3, w_xo, *, batch, seq_len, tm, heads):
    n, d = x2d.shape
    cw = y2d.shape[-1]
    nj = seq_len // tm
    row = lambda b, j: (b * nj + j, 0)
    mem = pl.BlockSpec((1,) + mk3.shape[1:], lambda b, j: (b, 0, 0))
    o_spec = pl.BlockSpec((None, o4.shape[1], tm, LANES), lambda b, j: (b, 0, j, 0))
    return pl.pallas_call(
        functools.partial(_post_kernel, heads=heads),
        grid=(batch, nj),
        in_specs=[pl.BlockSpec((tm, d), row), o_spec,
                  pl.BlockSpec((tm, cw), row), _const_spec(w_out.shape), _const_spec((1, d)),
                  _const_spec(w_xq.shape), mem, mem, _const_spec(w_xo.shape)],
        out_specs=pl.BlockSpec((tm, d), row),
        out_shape=jax.ShapeDtypeStruct((n, d), F32),
        scratch_shapes=[pltpu.VMEM((tm, d), BF16)],
        compiler_params=_params(2), name="post",
    )(x2d, o4, y2d, w_out, g_x, w_xq, mk3, mv3, w_xo)


FFN_DOWN_PARTS = 4
FF_CHUNK = 1024


def _ffn_body(x_ref, g_ref, wup_ref, wgate_ref, wconv_ref, wdown_ref, gfin_ref, y_ref,
              a_ref, fix, emit_up, final_norm):
    x = x_ref[...]
    hf = _rms(x, g_ref[...]).astype(BF16)
    dff = wup_ref.shape[-1]
    for c0 in range(0, dff, FF_CHUNK):
        cols = slice(c0, min(c0 + FF_CHUNK, dff))
        up = _dot(hf, wup_ref[:, cols])
        emit_up(cols, up)
        uc = _conv3(up, wconv_ref[:, cols], fix(cols, up.shape))
        gate = _dot(hf, wgate_ref[:, cols])
        a_ref[:, cols] = (uc * jax.nn.sigmoid(uc) * gate).astype(BF16)
    part = x.shape[0] // FFN_DOWN_PARTS
    for r0 in range(0, x.shape[0], part):
        x3 = x[r0:r0 + part] + _dot(a_ref[r0:r0 + part, :], wdown_ref[...])
        y_ref[r0:r0 + part, :] = _rms(x3, gfin_ref[...]) if final_norm else x3


def _ffn_carry_kernel(x_ref, g_ref, wup_ref, wgate_ref, wconv_ref, wdown_ref, gfin_ref,
                      y_ref, tail_ref, a_ref, carry_ref, new_ref, *, final_norm):
    @pl.when(pl.program_id(1) == 0)
    def _():
        carry_ref[...] = jnp.zeros_like(carry_ref)

    def emit_up(cols, up):
        new_ref[:, cols] = up[up.shape[0] - SUBLANES:, :]

    fix = lambda cols, shape: _carry_fix(carry_ref, cols, shape)
    _ffn_body(x_ref, g_ref, wup_ref, wgate_ref, wconv_ref, wdown_ref, gfin_ref, y_ref,
              a_ref, fix, emit_up, final_norm)
    carry_ref[...] = new_ref[...]
    tail_ref[0] = new_ref[...]


def _ffn_state_kernel(x_ref, g_ref, wup_ref, wgate_ref, wconv_ref, wdown_ref, gfin_ref,
                      sp1_ref, sp2_ref, y_ref, up_ref, a_ref, *, seq_len, final_norm):
    def emit_up(cols, up):
        up_ref[:, cols] = up

    fix = lambda cols, shape: _state_fix(sp1_ref[:, cols], sp2_ref[:, cols], seq_len, shape)
    _ffn_body(x_ref, g_ref, wup_ref, wgate_ref, wconv_ref, wdown_ref, gfin_ref, y_ref,
              a_ref, fix, emit_up, final_norm)


def _ffn(x2d, g_ffn, w_up, w_gate, w_conv, w_down, g_final, *, batch, seq_len, tm, final_norm,
         state=None):
    n, d = x2d.shape
    dff = w_up.shape[-1]
    weights = [_const_spec((1, d)), _const_spec(w_up.shape), _const_spec(w_gate.shape),
               _const_spec(w_conv.shape), _const_spec(w_down.shape), _const_spec((1, d))]
    if state is None:
        nj = seq_len // tm
        row = lambda b, j: (b * nj + j, 0)
        return pl.pallas_call(
            functools.partial(_ffn_carry_kernel, final_norm=final_norm),
            grid=(batch, nj),
            in_specs=[pl.BlockSpec((tm, d), row)] + weights,
            out_specs=[pl.BlockSpec((tm, d), row),
                       pl.BlockSpec((1, SUBLANES, dff), lambda b, j: (b, 0, 0))],
            out_shape=[jax.ShapeDtypeStruct((n, d), F32),
                       jax.ShapeDtypeStruct((batch, SUBLANES, dff), F32)],
            scratch_shapes=[pltpu.VMEM((tm, dff), BF16), pltpu.VMEM((SUBLANES, dff), F32),
                            pltpu.VMEM((SUBLANES, dff), F32)],
            compiler_params=_params(2), name="ffn_prompt",
        )(x2d, g_ffn, w_up, w_gate, w_conv, w_down, g_final)
    sp1, sp2 = state
    row = lambda i: (i, 0)
    return pl.pallas_call(
        functools.partial(_ffn_state_kernel, seq_len=seq_len, final_norm=final_norm),
        grid=(n // tm,),
        in_specs=[pl.BlockSpec((tm, d), row)] + weights + [pl.BlockSpec((tm, dff), row)] * 2,
        out_specs=[pl.BlockSpec((tm, d), row), pl.BlockSpec((tm, dff), row)],
        out_shape=[jax.ShapeDtypeStruct((n, d), F32), jax.ShapeDtypeStruct((n, dff), F32)],
        scratch_shapes=[pltpu.VMEM((tm, dff), BF16)],
        compiler_params=_params(1), name="ffn_sample",
    )(x2d, g_ffn, w_up, w_gate, w_conv, w_down, g_final, sp1, sp2)


def _rope_tables(pos, qk_head_dim):
    rot = qk_head_dim // 4
    half = rot // 2
    inv = 1.0 / (ROPE_THETA ** (jnp.arange(half, dtype=F32) * 2.0 / rot))
    ang = pos.astype(F32)[:, None] * inv[None, :]
    cos, sin = jnp.cos(ang), jnp.sin(ang)
    t = pos.shape[0]
    zeros = lambda w: jnp.zeros((t, w), F32)
    c = jnp.concatenate([cos, cos, jnp.ones((t, qk_head_dim - rot), F32)], axis=1)
    a = jnp.concatenate([-sin, zeros(qk_head_dim - half)], axis=1)
    b = jnp.concatenate([zeros(half), sin, zeros(qk_head_dim - rot)], axis=1)
    rep = LANES // qk_head_dim
    return tuple(jnp.tile(m, (1, rep)) for m in (c, a, b))


def _expand_state(state, seq_len):
    b, k, c = state.shape
    sp1 = jnp.pad(state[:, k - 1:], ((0, 0), (0, seq_len - 1), (0, 0)))
    sp2 = jnp.pad(state, ((0, 0), (0, seq_len - k), (0, 0)))
    return sp1.reshape(b * seq_len, c), sp2.reshape(b * seq_len, c)


def kernel(x_prompt, x_sample, cache_attn_k, cache_attn_v, state_short_conv, state_ffn_conv,
           cache_mem_k, cache_mem_v, mem_prompt, g_mix, w_in, lam_q1, lam_k1, lam_q2, lam_k2,
           g_sub, w_sc, w_out, g_mem, g_x, w_xq, w_xk, w_xv, w_xo, g_ffn, w_up, w_gate,
           w_ffconv, w_down, g_final):
    depth = w_in.shape[0]
    bp, seq, d = x_prompt.shape
    bs, t_new, _ = x_sample.shape
    past = cache_attn_k.shape[2]
    heads_a, qk_dim = cache_attn_k.shape[3], cache_attn_k.shape[5]
    v_dim = cache_attn_v.shape[4]
    cw = state_short_conv.shape[-1]
    dff = state_ffn_conv.shape[-1]
    n_mem, heads_x, x_dim = cache_mem_k.shape[2:]
    qc = kc = heads_a * 2 * qk_dim
    vc = heads_a * v_dim
    dims = (qc, kc, vc, cw, qk_dim ** -0.5 * math.log2(math.e), qk_dim // 8)
    assert 2 * qk_dim == LANES and v_dim == LANES
    assert Q_TILE % CHUNK == 0 and CHUNK & (CHUNK - 1) == 0 and Q_TILE & (Q_TILE - 1) == 0

    tabs_p = _rope_tables(jnp.arange(seq, dtype=jnp.int32), qk_dim)
    tabs_s = tuple(jnp.tile(m, (bs, 1)) for m in
                   _rope_tables(past + jnp.arange(t_new, dtype=jnp.int32), qk_dim))
    row = lambda v: v.reshape(1, -1)

    hp = x_prompt.reshape(bp * seq, d)
    hs = x_sample.reshape(bs * t_new, d)
    outs_p = [[] for _ in range(6)]
    outs_s = [[] for _ in range(4)]
    for l in range(depth):
        lam_init = _lambda_init(l)
        wi, wo = w_in[l].astype(BF16), w_out[l].astype(BF16)
        wq, wxo_b = w_xq[l].astype(BF16), w_xo[l].astype(BF16)
        wkv = jnp.concatenate([w_xk[l], w_xv[l]], axis=1).astype(BF16)
        wu, wg, wd = w_up[l].astype(BF16), w_gate[l].astype(BF16), w_down[l].astype(BF16)
        lams = tuple(row(v[l]) for v in (lam_q1, lam_k1, lam_q2, lam_k2))
        gsub = row(g_sub[l])

        mk8, mv8, mk, mv = _mem_kv(mem_prompt.reshape(bp * n_mem, d), row(g_mem[l]), wkv,
                                   MEM_TILE, heads_x)
        q, kt, v4, vb, ysc, sc_tail = _in_proj(hp, row(g_mix[l]), wi, tabs_p, w_sc[l], dims,
                                               batch=bp, seq_len=seq, tm=PROJ_TILE)
        o = _attn_prompt(q, kt, vb, lams, gsub, tq=Q_TILE, lam_init=lam_init)
        x2 = _post(hp, o, ysc, wo, row(g_x[l]), wq,
                   mk.reshape(bp, n_mem, d), mv.reshape(bp, n_mem, d), wxo_b,
                   batch=bp, seq_len=seq, tm=POST_TILE, heads=heads_x)
        hp, ff_tail = _ffn(x2, row(g_ffn[l]), wu, wg, w_ffconv[l], wd, row(g_final),
                           batch=bp, seq_len=seq, tm=FFN_TILE, final_norm=l == depth - 1)
        outs_p[0].append(jnp.transpose(kt.reshape(bp, heads_a, 2, qk_dim, seq), (0, 4, 1, 2, 3)))
        outs_p[1].append(v4.reshape(bp, seq, heads_a, v_dim))
        outs_p[2].append(sc_tail[:, SUBLANES - 2:])
        outs_p[3].append(ff_tail[:, SUBLANES - 2:])
        outs_p[4].append(_from_mem_layout(mk8, bp, n_mem, heads_x, x_dim))
        outs_p[5].append(_from_mem_layout(mv8, bp, n_mem, heads_x, x_dim))

        n_s = bs * t_new
        qs, ks, vs, yscs, u_s = _in_proj(
            hs, row(g_mix[l]), wi, tabs_s, w_sc[l], dims, batch=bs, seq_len=t_new, tm=n_s,
            state=_expand_state(state_short_conv[l], t_new))
        o_s = _attn_sample(qs.reshape(bs, t_new, qc), ks.reshape(bs, t_new, kc),
                           vs.reshape(bs, t_new, vc),
                           jnp.transpose(cache_attn_k[l].reshape(bs, past, kc), (0, 2, 1)),
                           cache_attn_v[l].reshape(bs, past * heads_a, v_dim), lams, gsub,
                           lam_init=lam_init, nb=SAMPLE_ATTN_SEQS)
        x2s = _post_rows(hs, o_s.reshape(n_s, vc), yscs, wo, row(g_x[l]), wq,
                         _to_mem_layout(cache_mem_k[l], x_dim // LANES),
                         _to_mem_layout(cache_mem_v[l], x_dim // LANES), wxo_b,
                         batch=bs, seq_len=t_new, heads=heads_x, nb=SAMPLE_POST_SEQS)
        hs, up_s = _ffn(x2s, row(g_ffn[l]), wu, wg, w_ffconv[l], wd, row(g_final),
                        batch=bs, seq_len=t_new, tm=n_s, final_norm=l == depth - 1,
                        state=_expand_state(state_ffn_conv[l], t_new))
        outs_s[0].append(ks.reshape(bs, t_new, heads_a, 2, qk_dim))
        outs_s[1].append(vs.reshape(bs, t_new, heads_a, v_dim))
        outs_s[2].append(u_s.reshape(bs, t_new, cw)[:, t_new - 2:])
        outs_s[3].append(up_s.reshape(bs, t_new, dff)[:, t_new - 2:])

    return (hp.reshape(bp, seq, d), hs.reshape(bs, t_new, d),
            *(jnp.stack(o) for o in outs_p), *(jnp.stack(o) for o in outs_s))
```

```python
import functools
import math

import jax
import jax.numpy as jnp
from jax import lax
from jax.experimental import pallas as pl
from jax.experimental.pallas import tpu as pltpu

EPS = 1e-6
CHUNK = 64
ROPE_THETA = 500000.0
LANES = 128
SUBLANES = 8
VMEM_LIMIT = 56 * 1024 * 1024
MEM_TILE = 1024
PROJ_TILE = 1024
POST_TILE = 1024
FFN_TILE = 1024
Q_TILE = 128
ATTN_HEADS_PER_STEP = 4
ATTN_LOOKAHEAD = 2
SAMPLE_ATTN_SEQS = 4
SAMPLE_POST_SEQS = 4
BF16 = jnp.bfloat16
F32 = jnp.float32
NEG_INF = float("-inf")


def _lambda_init(layer_idx):
    return 0.8 - 0.6 * math.exp(-0.3 * layer_idx)


def _rms(x, g):
    return x * lax.rsqrt(jnp.mean(x * x, axis=-1, keepdims=True) + EPS) * g


def _dot(a, b):
    return jnp.dot(a, b, preferred_element_type=F32)


def _dot_nt(a, b):
    return lax.dot_general(a, b, (((1,), (1,)), ((), ())), preferred_element_type=F32)


def _params(n_grid):
    return pltpu.CompilerParams(dimension_semantics=("arbitrary",) * n_grid,
                                vmem_limit_bytes=VMEM_LIMIT)


def _const_spec(shape):
    nd = len(shape)
    return pl.BlockSpec(shape, lambda *_: (0,) * nd, pipeline_mode=pl.Buffered(1))


def _conv3(u, w, fix):
    p1 = pltpu.roll(u, 1, axis=0)
    p2 = pltpu.roll(u, 2, axis=0)
    p1, p2 = fix(p1, p2)
    return w[0:1, :] * p2 + w[1:2, :] * p1 + w[2:3, :] * u


def _carry_fix(carry_ref, cols, shape):
    row = lax.broadcasted_iota(jnp.int32, shape, 0)
    c0 = carry_ref[SUBLANES - 2:SUBLANES - 1, cols]
    c1 = carry_ref[SUBLANES - 1:SUBLANES, cols]

    def fix(p1, p2):
        p1 = jnp.where(row == 0, c1, p1)
        p2 = jnp.where(row == 0, c0, jnp.where(row == 1, c1, p2))
        return p1, p2
    return fix


def _state_fix(sp1, sp2, seq_len, shape):
    row = lax.broadcasted_iota(jnp.int32, shape, 0)
    t = lax.rem(row, seq_len)

    def fix(p1, p2):
        return jnp.where(t == 0, sp1, p1), jnp.where(t < 2, sp2, p2)
    return fix


def _store_mem_layout(ref, blk, heads):
    rows, width = blk.shape
    lt = width // heads // LANES
    for h in range(heads):
        for t in range(lt):
            c0 = (h * lt + t) * LANES
            ref[pl.ds(t * heads + h, rows, stride=heads * lt), :] = blk[:, c0:c0 + LANES]


def _load_mem_head(ref, bi, h, heads, hd):
    if ref.shape[-1] != LANES:
        return ref[bi, :, h * hd:(h + 1) * hd]
    lt = hd // LANES
    rows = ref.shape[1] // (heads * lt)
    parts = [ref[bi, pl.ds(t * heads + h, rows, stride=heads * lt), :] for t in range(lt)]
    return jnp.concatenate(parts, axis=1).astype(BF16)


def _to_mem_layout(x, lt):
    b, n, heads, hd = x.shape
    x = jnp.transpose(x.reshape(b, n, heads, lt, LANES), (0, 1, 3, 2, 4))
    return x.reshape(b, n * heads * lt, LANES)


def _from_mem_layout(x8, b, n, heads, hd):
    lt = hd // LANES
    x = jnp.transpose(x8.reshape(b, n, lt, heads, LANES), (0, 1, 3, 2, 4))
    return x.reshape(b, n, heads, hd)


def _memkv_kernel(m_ref, g_ref, w_ref, k8_ref, v8_ref, kb_ref, vb_ref, *, heads):
    d = kb_ref.shape[-1]
    h = _rms(m_ref[...], g_ref[...]).astype(BF16)
    kv = _dot(h, w_ref[...])
    for part, (o8, ob) in enumerate(((k8_ref, kb_ref), (v8_ref, vb_ref))):
        blk = kv[:, part * d:(part + 1) * d]
        ob[...] = blk.astype(BF16)
        _store_mem_layout(o8, blk, heads)


def _mem_kv(mem2d, g_mem, w_xkv, tm, heads):
    n, d = mem2d.shape
    group = d // LANES
    row = lambda i: (i, 0)
    return pl.pallas_call(
        functools.partial(_memkv_kernel, heads=heads),
        grid=(n // tm,),
        in_specs=[pl.BlockSpec((tm, d), row), _const_spec((1, d)), _const_spec(w_xkv.shape)],
        out_specs=[pl.BlockSpec((tm * group, LANES), row)] * 2 + [pl.BlockSpec((tm, d), row)] * 2,
        out_shape=[jax.ShapeDtypeStruct((n * group, LANES), F32)] * 2
        + [jax.ShapeDtypeStruct((n, d), BF16)] * 2,
        compiler_params=_params(1),
        name="mem_kv",
    )(mem2d, g_mem, w_xkv)


def _inproj_body(x_ref, g_ref, w_ref, cos_ref, sa_ref, sb_ref, wsc_ref,
                 put_q, put_k, put_v, y_ref, fix, dims):
    qc, kc, vc, cw, q_scale, rot_half = dims
    h = _rms(x_ref[...], g_ref[...]).astype(BF16)
    cos, sa, sb = cos_ref[...], sa_ref[...], sb_ref[...]

    def rope(t):
        return (t * cos + pltpu.roll(t, LANES - rot_half, axis=1) * sa
                + pltpu.roll(t, rot_half, axis=1) * sb)

    o = qc + kc
    cx = _dot(h, w_ref[:, o + vc + cw:])
    u = cx[:, :cw] * cx[:, cw:]
    conv = _conv3(u, wsc_ref[...], fix(u.shape))
    qk = _dot(h, w_ref[:, :qc + kc])
    for c in range(qc // LANES):
        blk = rope(qk[:, c * LANES:(c + 1) * LANES])
        put_q(c, (blk * q_scale).astype(BF16))
    for c in range(kc // LANES):
        lo = qc + c * LANES
        put_k(c, rope(qk[:, lo:lo + LANES]))
    put_v(_dot(h, w_ref[:, o:o + vc]))
    bg = _dot(h, w_ref[:, o + vc:o + vc + cw])
    y_ref[...] = (bg * conv).astype(BF16)
    return u


def _inproj_carry_kernel(x_ref, g_ref, w_ref, cos_ref, sa_ref, sb_ref, wsc_ref,
                         q_ref, kt_ref, v4_ref, vb_ref, y_ref, tail_ref, carry_ref, *, dims):
    @pl.when(pl.program_id(1) == 0)
    def _():
        carry_ref[...] = jnp.zeros_like(carry_ref)

    def put_q(c, blk):
        q_ref[c] = blk

    def put_k(c, blk):
        kt_ref[c * LANES:(c + 1) * LANES, :] = blk.T

    def put_v(v):
        heads = v.shape[1] // LANES
        for hd in range(heads):
            vh = v[:, hd * LANES:(hd + 1) * LANES]
            vb_ref[hd] = vh.astype(BF16)
            v4_ref[pl.ds(hd, v.shape[0], stride=heads), :] = vh

    fix = lambda shape: _carry_fix(carry_ref, slice(None), shape)
    u = _inproj_body(x_ref, g_ref, w_ref, cos_ref, sa_ref, sb_ref, wsc_ref,
                     put_q, put_k, put_v, y_ref, fix, dims)
    last = u[u.shape[0] - SUBLANES:, :]
    carry_ref[...] = last
    tail_ref[0] = last


def _inproj_state_kernel(x_ref, g_ref, w_ref, cos_ref, sa_ref, sb_ref, wsc_ref,
                         sp1_ref, sp2_ref, q_ref, k_ref, v_ref, y_ref, u_ref,
                         *, dims, seq_len):
    def put_q(c, blk):
        q_ref[:, c * LANES:(c + 1) * LANES] = blk

    def put_k(c, blk):
        k_ref[:, c * LANES:(c + 1) * LANES] = blk

    def put_v(v):
        v_ref[...] = v

    fix = lambda shape: _state_fix(sp1_ref[...], sp2_ref[...], seq_len, shape)
    u_ref[...] = _inproj_body(x_ref, g_ref, w_ref, cos_ref, sa_ref, sb_ref, wsc_ref,
                              put_q, put_k, put_v, y_ref, fix, dims)


def _in_proj(x2d, g, w, tabs, w_sc, dims, *, batch, seq_len, tm, state=None):
    n, d = x2d.shape
    qc, kc, vc, cw = dims[:4]
    common_in = [None, _const_spec((1, d)), _const_spec(w.shape), None, None, None,
                 _const_spec(w_sc.shape)]
    if state is None:
        nj = seq_len // tm
        heads = vc // LANES
        row = lambda b, j: (b * nj + j, 0)
        tab = pl.BlockSpec((tm, LANES), lambda b, j: (j, 0))
        in_specs = list(common_in)
        in_specs[0] = pl.BlockSpec((tm, d), row)
        in_specs[3:6] = [tab, tab, tab]
        per_head = pl.BlockSpec((None, heads, tm, LANES), lambda b, j: (b, 0, j, 0))
        out_specs = [per_head,
                     pl.BlockSpec((None, kc, tm), lambda b, j: (b, 0, j)),
                     pl.BlockSpec((tm * heads, LANES), row),
                     per_head, pl.BlockSpec((tm, cw), row),
                     pl.BlockSpec((1, SUBLANES, cw), lambda b, j: (b, 0, 0))]
        out_shape = [jax.ShapeDtypeStruct((batch, heads, seq_len, LANES), BF16),
                     jax.ShapeDtypeStruct((batch, kc, seq_len), F32),
                     jax.ShapeDtypeStruct((n * heads, LANES), F32),
                     jax.ShapeDtypeStruct((batch, heads, seq_len, LANES), BF16),
                     jax.ShapeDtypeStruct((n, cw), BF16),
                     jax.ShapeDtypeStruct((batch, SUBLANES, cw), F32)]
        return pl.pallas_call(
            functools.partial(_inproj_carry_kernel, dims=dims),
            grid=(batch, nj), in_specs=in_specs, out_specs=out_specs, out_shape=out_shape,
            scratch_shapes=[pltpu.VMEM((SUBLANES, cw), F32)],
            compiler_params=_params(2), name="in_proj_prompt",
        )(x2d, g, w, *tabs, w_sc)
    sp1, sp2 = state
    row = lambda i: (i, 0)
    tab = pl.BlockSpec((tm, LANES), row)
    in_specs = list(common_in)
    in_specs[0] = pl.BlockSpec((tm, d), row)
    in_specs[3:6] = [tab, tab, tab]
    in_specs += [pl.BlockSpec((tm, cw), row)] * 2
    out_specs = [pl.BlockSpec((tm, c), row) for c in (qc, kc, vc, cw, cw)]
    out_shape = [jax.ShapeDtypeStruct((n, qc), BF16), jax.ShapeDtypeStruct((n, kc), F32),
                 jax.ShapeDtypeStruct((n, vc), F32), jax.ShapeDtypeStruct((n, cw), BF16),
                 jax.ShapeDtypeStruct((n, cw), F32)]
    return pl.pallas_call(
        functools.partial(_inproj_state_kernel, dims=dims, seq_len=seq_len),
        grid=(n // tm,), in_specs=in_specs, out_specs=out_specs, out_shape=out_shape,
        compiler_params=_params(1), name="in_proj_sample",
    )(x2d, g, w, *tabs, w_sc, sp1, sp2)


def _diff_lambda(lq1_ref, lk1_ref, lq2_ref, lk2_ref, lam_init):
    a = jnp.sum(lq1_ref[...] * lk1_ref[...], axis=-1, keepdims=True)
    b = jnp.sum(lq2_ref[...] * lk2_ref[...], axis=-1, keepdims=True)
    return jnp.exp(a) - jnp.exp(b) + lam_init


def _stack_maps(qt):
    lane = lax.broadcasted_iota(jnp.int32, qt.shape, 1)
    first = lane < (LANES // 2)
    zero = jnp.zeros_like(qt)
    return jnp.concatenate([jnp.where(first, qt, zero), jnp.where(first, zero, qt)], axis=0)


def _finish_head(acc, l, lam, g, lam_init, tq):
    o = acc[:tq] / l[:tq] - lam * (acc[tq:] / l[tq:])
    return _rms(o, g) * (1.0 - lam_init)


def _attn_prompt_kernel(q_ref, k_ref, v_ref, lq1_ref, lk1_ref, lq2_ref, lk2_ref, g_ref,
                        o_ref, kb_ref, vt_ref, *, tq, lam_init):
    nh, seq, hd = v_ref.shape
    for h in range(nh):
        kb_ref[h] = k_ref[h * hd:(h + 1) * hd, :].T.astype(BF16)
        vt_ref[h, :hd, :] = v_ref[h].astype(F32).T.astype(BF16)
        vt_ref[h, hd:, :] = jnp.ones((vt_ref.shape[1] - hd, seq), BF16)
    lam = _diff_lambda(lq1_ref, lk1_ref, lq2_ref, lk2_ref, lam_init)
    g = g_ref[...]
    r = lax.broadcasted_iota(jnp.int32, (tq, 2 * tq), 0)
    c = lax.broadcasted_iota(jnp.int32, (tq, 2 * tq), 1)
    shift = CHUNK.bit_length() - 1
    diag_bias = jnp.where((r >> shift) <= ((c & (tq - 1)) >> shift), 0.0, NEG_INF)

    nq = seq // tq
    order = list(range(0, nq, 2)) + list(range(nq - 1 - (nq % 2), 0, -2))
    def scores(h, i):
        lo = i * tq
        qm = _stack_maps(q_ref[h, lo:lo + tq, :])
        s_d = _dot_nt(kb_ref[h, lo:lo + tq, :], qm) + diag_bias
        m = jnp.max(s_d, axis=0, keepdims=True)
        s_o = None
        if i:
            s_o = _dot_nt(kb_ref[h, :lo, :], qm)
            m = jnp.maximum(m, jnp.max(s_o, axis=0, keepdims=True))
        return s_d, s_o, m

    def finish(h, i, s_d, s_o, m):
        lo = i * tq
        acc = _dot(vt_ref[h, :, lo:lo + tq], jnp.exp2(s_d - m).astype(BF16))
        if i:
            acc = acc + _dot(vt_ref[h, :, :lo], jnp.exp2(s_o - m).astype(BF16))
        num, l = acc[:hd, :], acc[hd:hd + 1, :]
        ot = num[:, :tq] / l[:, :tq] - lam * (num[:, tq:] / l[:, tq:])
        ot = ot * lax.rsqrt(jnp.mean(ot * ot, axis=0, keepdims=True) + EPS) * g
        o_ref[h, lo:lo + tq, :] = (ot * (1.0 - lam_init)).T.astype(BF16)

    tasks = [(h, i) for i in order for h in range(nh)]
    pending = [scores(*t) for t in tasks[:ATTN_LOOKAHEAD]]
    for n, t in enumerate(tasks):
        if n + ATTN_LOOKAHEAD < len(tasks):
            pending.append(scores(*tasks[n + ATTN_LOOKAHEAD]))
        finish(*t, *pending.pop(0))


def _attn_prompt(q4, kt3, v4, lams, g_sub, *, tq, lam_init):
    b, heads, seq, _ = q4.shape
    nh = ATTN_HEADS_PER_STEP
    blk = lambda: pl.BlockSpec((None, nh, seq, LANES), lambda i, h: (i, h, 0, 0))
    lam_spec = _const_spec(lams[0].shape)
    g_col = g_sub.reshape(-1, 1)
    ones_rows = 2 * SUBLANES
    return pl.pallas_call(
        functools.partial(_attn_prompt_kernel, tq=tq, lam_init=lam_init),
        grid=(b, heads // nh),
        in_specs=[blk(), pl.BlockSpec((None, nh * LANES, seq), lambda i, h: (i, h, 0)), blk(),
                  lam_spec, lam_spec, lam_spec, lam_spec, _const_spec(g_col.shape)],
        out_specs=blk(),
        out_shape=jax.ShapeDtypeStruct(q4.shape, BF16),
        scratch_shapes=[pltpu.VMEM((nh, seq, LANES), BF16),
                        pltpu.VMEM((nh, LANES + ones_rows, seq), BF16)],
        compiler_params=_params(2), name="attn_prompt",
    )(q4, kt3, v4, *lams, g_col)


def _attn_sample_kernel(q_ref, kn_ref, vn_ref, kp_ref, vp_ref, lq1_ref, lk1_ref, lq2_ref,
                        lk2_ref, g_ref, o_ref, *, lam_init):
    t_new = q_ref.shape[1]
    heads = q_ref.shape[2] // LANES
    past = kp_ref.shape[2]
    lam = _diff_lambda(lq1_ref, lk1_ref, lq2_ref, lk2_ref, lam_init)
    g = g_ref[...]
    pad = jnp.zeros((LANES - t_new, LANES), BF16)
    lane = lax.broadcasted_iota(jnp.int32, (2 * t_new, LANES), 1)
    for bi, h in [(bi, h) for bi in range(q_ref.shape[0]) for h in range(heads)]:
        cols = slice(h * LANES, (h + 1) * LANES)
        qm = _stack_maps(q_ref[bi, :, cols])
        kn = jnp.concatenate([kn_ref[bi, :, cols].astype(BF16), pad], axis=0)
        vn = jnp.concatenate([vn_ref[bi, :, cols].astype(BF16), pad], axis=0)
        s_p = _dot(qm, kp_ref[bi, cols, :].astype(BF16))
        vp = vp_ref[bi, pl.ds(h, past, stride=heads), :].astype(BF16)
        s_n = jnp.where(lane < t_new, _dot_nt(qm, kn), NEG_INF)
        m = jnp.maximum(jnp.max(s_p, axis=-1, keepdims=True),
                        jnp.max(s_n, axis=-1, keepdims=True))
        e_p = jnp.exp2(s_p - m)
        e_n = jnp.exp2(s_n - m)
        l = jnp.sum(e_p, axis=-1, keepdims=True) + jnp.sum(e_n, axis=-1, keepdims=True)
        acc = _dot(e_p.astype(BF16), vp) + _dot(e_n.astype(BF16), vn)
        o_ref[bi, :, cols] = _finish_head(acc, l, lam, g, lam_init, t_new).astype(BF16)


def _attn_sample(q3, kn3, vn3, kpt3, vp4, lams, g_sub, *, lam_init, nb):
    b, t_new, width = q3.shape
    new = lambda: pl.BlockSpec((nb, t_new, width), lambda i: (i, 0, 0))
    whole = lambda a: pl.BlockSpec((nb,) + a.shape[1:], lambda i: (i, 0, 0))
    lam_spec = _const_spec(lams[0].shape)
    return pl.pallas_call(
        functools.partial(_attn_sample_kernel, lam_init=lam_init),
        grid=(b // nb,),
        in_specs=[new(), new(), new(), whole(kpt3), whole(vp4), lam_spec, lam_spec, lam_spec,
                  lam_spec, _const_spec(g_sub.shape)],
        out_specs=new(),
        out_shape=jax.ShapeDtypeStruct((b, t_new, width), BF16),
        compiler_params=_params(1), name="attn_sample",
    )(q3, kn3, vn3, kpt3, vp4, *lams, g_sub)


def _mix(x_ref, o_ref, y_ref, wout_ref, rows):
    if len(o_ref.shape) == 3:
        o = jnp.concatenate([o_ref[h, rows, :] for h in range(o_ref.shape[0])], axis=1)
    else:
        o = o_ref[rows, :]
    aw = o.shape[-1]
    return (x_ref[rows, :] + _dot(o, wout_ref[:aw, :])
            + _dot(y_ref[rows, :], wout_ref[aw:, :]))


def _query(x1, gx_ref, wxq_ref, heads):
    hd = x1.shape[-1] // heads
    scale = hd ** -0.5 * math.log2(math.e)
    return (_dot(_rms(x1, gx_ref[...]).astype(BF16), wxq_ref[...]) * scale).astype(BF16)


def _mix_and_query(x_ref, o_ref, y_ref, wout_ref, gx_ref, wxq_ref, heads):
    x1 = _mix(x_ref, o_ref, y_ref, wout_ref, slice(None))
    return x1, _query(x1, gx_ref, wxq_ref, heads)


def _cross_attend(hq, mk_ref, mv_ref, bi, heads, put):
    hd = hq.shape[-1] // heads
    for h in range(heads):
        cols = slice(h * hd, (h + 1) * hd)
        s = _dot_nt(hq[:, cols], _load_mem_head(mk_ref, bi, h, heads, hd))
        e = jnp.exp2(s - jnp.max(s, axis=-1, keepdims=True))
        l = jnp.sum(e, axis=-1, keepdims=True)
        xo = _dot(e.astype(BF16), _load_mem_head(mv_ref, bi, h, heads, hd)) / l
        put(cols, xo.astype(BF16))


def _post_kernel(x_ref, o_ref, y_ref, wout_ref, gx_ref, wxq_ref, mk_ref, mv_ref, wxo_ref,
                 out_ref, xo_ref, *, heads):
    n = x_ref.shape[0]
    parts = [slice(0, n // 2), slice(n // 2, n)]
    x1 = [_mix(x_ref, o_ref, y_ref, wout_ref, p) for p in parts]
    hq = [_query(x, gx_ref, wxq_ref, heads) for x in x1]
    for p, q in zip(parts, hq):
        def put(cols, xo, p=p):
            xo_ref[p, cols] = xo

        _cross_attend(q, mk_ref, mv_ref, 0, heads, put)
    for p, x in zip(parts, x1):
        out_ref[p, :] = x + _dot(xo_ref[p, :], wxo_ref[...])


def _post_rows_kernel(x_ref, o_ref, y_ref, wout_ref, gx_ref, wxq_ref, mk_ref, mv_ref, wxo_ref,
                      out_ref, x1_ref, hq_ref, xo_ref, *, heads, seq_len):
    b = pl.program_id(0)
    nb = mk_ref.shape[0]

    @pl.when(b == 0)
    def _():
        x1_ref[...], hq_ref[...] = _mix_and_query(x_ref, o_ref, y_ref, wout_ref, gx_ref,
                                                  wxq_ref, heads)

    for bi in range(nb):
        rows = pl.ds(pl.multiple_of((b * nb + bi) * seq_len, seq_len), seq_len)

        def put(cols, xo, rows=rows):
            xo_ref[rows, cols] = xo

        _cross_attend(hq_ref[rows, :], mk_ref, mv_ref, bi, heads, put)

    @pl.when(b == pl.num_programs(0) - 1)
    def _():
        out_ref[...] = x1_ref[...] + _dot(xo_ref[...], wxo_ref[...])


def _post_rows(x2d, o2d, y2d, w_out, g_x, w_xq, mk3, mv3, w_xo, *, batch, seq_len, heads, nb):
    n, d = x2d.shape
    mem = pl.BlockSpec((nb,) + mk3.shape[1:], lambda b: (b, 0, 0))
    return pl.pallas_call(
        functools.partial(_post_rows_kernel, heads=heads, seq_len=seq_len),
        grid=(batch // nb,),
        in_specs=[_const_spec(x2d.shape), _const_spec(o2d.shape), _const_spec(y2d.shape),
                  _const_spec(w_out.shape), _const_spec((1, d)), _const_spec(w_xq.shape),
                  mem, mem, _const_spec(w_xo.shape)],
        out_specs=pl.BlockSpec((n, d), lambda b: (0, 0)),
        out_shape=jax.ShapeDtypeStruct((n, d), F32),
        scratch_shapes=[pltpu.VMEM((n, d), F32), pltpu.VMEM((n, d), BF16),
                        pltpu.VMEM((n, d), BF16)],
        compiler_params=_params(1), name="post_sample",
    )(x2d, o2d, y2d, w_out, g_x, w_xq, mk3, mv3, w_xo)


def _post(x2d, o4, y2d, w_out, g_x, w_xq, mk3, mv3, w_xo, *, batch, seq_len, tm, heads):
    n, d = x2d.shape
    cw = y2d.shape[-1]
    nj = seq_len // tm
    row = lambda b, j: (b * nj + j, 0)
    mem = pl.BlockSpec((1,) + mk3.shape[1:], lambda b, j: (b, 0, 0))
    o_spec = pl.BlockSpec((None, o4.shape[1], tm, LANES), lambda b, j: (b, 0, j, 0))
    return pl.pallas_call(
        functools.partial(_post_kernel, heads=heads),
        grid=(batch, nj),
        in_specs=[pl.BlockSpec((tm, d), row), o_spec,
                  pl.BlockSpec((tm, cw), row), _const_spec(w_out.shape), _const_spec((1, d)),
                  _const_spec(w_xq.shape), mem, mem, _const_spec(w_xo.shape)],
        out_specs=pl.BlockSpec((tm, d), row),
        out_shape=jax.ShapeDtypeStruct((n, d), F32),
        scratch_shapes=[pltpu.VMEM((tm, d), BF16)],
        compiler_params=_params(2), name="post",
    )(x2d, o4, y2d, w_out, g_x, w_xq, mk3, mv3, w_xo)


FFN_DOWN_PARTS = 8
FF_CHUNK = 1024


def _ffn_body(x_ref, g_ref, wup_ref, wgate_ref, wconv_ref, wdown_ref, gfin_ref, y_ref,
              a_ref, fix, emit_up, final_norm):
    x = x_ref[...]
    hf = _rms(x, g_ref[...]).astype(BF16)
    dff = wup_ref.shape[-1]
    for c0 in range(0, dff, FF_CHUNK):
        cols = slice(c0, min(c0 + FF_CHUNK, dff))
        up = _dot(hf, wup_ref[:, cols])
        emit_up(cols, up)
        uc = _conv3(up, wconv_ref[:, cols], fix(cols, up.shape))
        gate = _dot(hf, wgate_ref[:, cols])
        a_ref[:, cols] = (uc * jax.nn.sigmoid(uc) * gate).astype(BF16)
    part = x.shape[0] // FFN_DOWN_PARTS
    for r0 in range(0, x.shape[0], part):
        x3 = x[r0:r0 + part] + _dot(a_ref[r0:r0 + part, :], wdown_ref[...])
        y_ref[r0:r0 + part, :] = _rms(x3, gfin_ref[...]) if final_norm else x3


def _ffn_carry_kernel(x_ref, g_ref, wup_ref, wgate_ref, wconv_ref, wdown_ref, gfin_ref,
                      y_ref, tail_ref, a_ref, carry_ref, new_ref, *, final_norm):
    @pl.when(pl.program_id(1) == 0)
    def _():
        carry_ref[...] = jnp.zeros_like(carry_ref)

    def emit_up(cols, up):
        new_ref[:, cols] = up[up.shape[0] - SUBLANES:, :]

    fix = lambda cols, shape: _carry_fix(carry_ref, cols, shape)
    _ffn_body(x_ref, g_ref, wup_ref, wgate_ref, wconv_ref, wdown_ref, gfin_ref, y_ref,
              a_ref, fix, emit_up, final_norm)
    carry_ref[...] = new_ref[...]
    tail_ref[0] = new_ref[...]


def _ffn_state_kernel(x_ref, g_ref, wup_ref, wgate_ref, wconv_ref, wdown_ref, gfin_ref,
                      sp1_ref, sp2_ref, y_ref, up_ref, a_ref, *, seq_len, final_norm):
    def emit_up(cols, up):
        up_ref[:, cols] = up

    fix = lambda cols, shape: _state_fix(sp1_ref[:, cols], sp2_ref[:, cols], seq_len, shape)
    _ffn_body(x_ref, g_ref, wup_ref, wgate_ref, wconv_ref, wdown_ref, gfin_ref, y_ref,
              a_ref, fix, emit_up, final_norm)


def _ffn(x2d, g_ffn, w_up, w_gate, w_conv, w_down, g_final, *, batch, seq_len, tm, final_norm,
         state=None):
    n, d = x2d.shape
    dff = w_up.shape[-1]
    weights = [_const_spec((1, d)), _const_spec(w_up.shape), _const_spec(w_gate.shape),
               _const_spec(w_conv.shape), _const_spec(w_down.shape), _const_spec((1, d))]
    if state is None:
        nj = seq_len // tm
        row = lambda b, j: (b * nj + j, 0)
        return pl.pallas_call(
            functools.partial(_ffn_carry_kernel, final_norm=final_norm),
            grid=(batch, nj),
            in_specs=[pl.BlockSpec((tm, d), row)] + weights,
            out_specs=[pl.BlockSpec((tm, d), row),
                       pl.BlockSpec((1, SUBLANES, dff), lambda b, j: (b, 0, 0))],
            out_shape=[jax.ShapeDtypeStruct((n, d), F32),
                       jax.ShapeDtypeStruct((batch, SUBLANES, dff), F32)],
            scratch_shapes=[pltpu.VMEM((tm, dff), BF16), pltpu.VMEM((SUBLANES, dff), F32),
                            pltpu.VMEM((SUBLANES, dff), F32)],
            compiler_params=_params(2), name="ffn_prompt",
        )(x2d, g_ffn, w_up, w_gate, w_conv, w_down, g_final)
    sp1, sp2 = state
    row = lambda i: (i, 0)
    return pl.pallas_call(
        functools.partial(_ffn_state_kernel, seq_len=seq_len, final_norm=final_norm),
        grid=(n // tm,),
        in_specs=[pl.BlockSpec((tm, d), row)] + weights + [pl.BlockSpec((tm, dff), row)] * 2,
        out_specs=[pl.BlockSpec((tm, d), row), pl.BlockSpec((tm, dff), row)],
        out_shape=[jax.ShapeDtypeStruct((n, d), F32), jax.ShapeDtypeStruct((n, dff), F32)],
        scratch_shapes=[pltpu.VMEM((tm, dff), BF16)],
        compiler_params=_params(1), name="ffn_sample",
    )(x2d, g_ffn, w_up, w_gate, w_conv, w_down, g_final, sp1, sp2)


def _rope_tables(pos, qk_head_dim):
    rot = qk_head_dim // 4
    half = rot // 2
    inv = 1.0 / (ROPE_THETA ** (jnp.arange(half, dtype=F32) * 2.0 / rot))
    ang = pos.astype(F32)[:, None] * inv[None, :]
    cos, sin = jnp.cos(ang), jnp.sin(ang)
    t = pos.shape[0]
    zeros = lambda w: jnp.zeros((t, w), F32)
    c = jnp.concatenate([cos, cos, jnp.ones((t, qk_head_dim - rot), F32)], axis=1)
    a = jnp.concatenate([-sin, zeros(qk_head_dim - half)], axis=1)
    b = jnp.concatenate([zeros(half), sin, zeros(qk_head_dim - rot)], axis=1)
    rep = LANES // qk_head_dim
    return tuple(jnp.tile(m, (1, rep)) for m in (c, a, b))


def _expand_state(state, seq_len):
    b, k, c = state.shape
    sp1 = jnp.pad(state[:, k - 1:], ((0, 0), (0, seq_len - 1), (0, 0)))
    sp2 = jnp.pad(state, ((0, 0), (0, seq_len - k), (0, 0)))
    return sp1.reshape(b * seq_len, c), sp2.reshape(b * seq_len, c)


def kernel(x_prompt, x_sample, cache_attn_k, cache_attn_v, state_short_conv, state_ffn_conv,
           cache_mem_k, cache_mem_v, mem_prompt, g_mix, w_in, lam_q1, lam_k1, lam_q2, lam_k2,
           g_sub, w_sc, w_out, g_mem, g_x, w_xq, w_xk, w_xv, w_xo, g_ffn, w_up, w_gate,
           w_ffconv, w_down, g_final):
    depth = w_in.shape[0]
    bp, seq, d = x_prompt.shape
    bs, t_new, _ = x_sample.shape
    past = cache_attn_k.shape[2]
    heads_a, qk_dim = cache_attn_k.shape[3], cache_attn_k.shape[5]
    v_dim = cache_attn_v.shape[4]
    cw = state_short_conv.shape[-1]
    dff = state_ffn_conv.shape[-1]
    n_mem, heads_x, x_dim = cache_mem_k.shape[2:]
    qc = kc = heads_a * 2 * qk_dim
    vc = heads_a * v_dim
    dims = (qc, kc, vc, cw, qk_dim ** -0.5 * math.log2(math.e), qk_dim // 8)
    assert 2 * qk_dim == LANES and v_dim == LANES
    assert Q_TILE % CHUNK == 0 and CHUNK & (CHUNK - 1) == 0 and Q_TILE & (Q_TILE - 1) == 0

    tabs_p = _rope_tables(jnp.arange(seq, dtype=jnp.int32), qk_dim)
    tabs_s = tuple(jnp.tile(m, (bs, 1)) for m in
                   _rope_tables(past + jnp.arange(t_new, dtype=jnp.int32), qk_dim))
    row = lambda v: v.reshape(1, -1)

    hp = x_prompt.reshape(bp * seq, d)
    hs = x_sample.reshape(bs * t_new, d)
    outs_p = [[] for _ in range(6)]
    outs_s = [[] for _ in range(4)]
    for l in range(depth):
        lam_init = _lambda_init(l)
        wi, wo = w_in[l].astype(BF16), w_out[l].astype(BF16)
        wq, wxo_b = w_xq[l].astype(BF16), w_xo[l].astype(BF16)
        wkv = jnp.concatenate([w_xk[l], w_xv[l]], axis=1).astype(BF16)
        wu, wg, wd = w_up[l].astype(BF16), w_gate[l].astype(BF16), w_down[l].astype(BF16)
        lams = tuple(row(v[l]) for v in (lam_q1, lam_k1, lam_q2, lam_k2))
        gsub = row(g_sub[l])

        mk8, mv8, mk, mv = _mem_kv(mem_prompt.reshape(bp * n_mem, d), row(g_mem[l]), wkv,
                                   MEM_TILE, heads_x)
        q, kt, v4, vb, ysc, sc_tail = _in_proj(hp, row(g_mix[l]), wi, tabs_p, w_sc[l], dims,
                                               batch=bp, seq_len=seq, tm=PROJ_TILE)
        o = _attn_prompt(q, kt, vb, lams, gsub, tq=Q_TILE, lam_init=lam_init)
        x2 = _post(hp, o, ysc, wo, row(g_x[l]), wq,
                   mk.reshape(bp, n_mem, d), mv.reshape(bp, n_mem, d), wxo_b,
                   batch=bp, seq_len=seq, tm=POST_TILE, heads=heads_x)
        hp, ff_tail = _ffn(x2, row(g_ffn[l]), wu, wg, w_ffconv[l], wd, row(g_final),
                           batch=bp, seq_len=seq, tm=FFN_TILE, final_norm=l == depth - 1)
        outs_p[0].append(jnp.transpose(kt.reshape(bp, heads_a, 2, qk_dim, seq), (0, 4, 1, 2, 3)))
        outs_p[1].append(v4.reshape(bp, seq, heads_a, v_dim))
        outs_p[2].append(sc_tail[:, SUBLANES - 2:])
        outs_p[3].append(ff_tail[:, SUBLANES - 2:])
        outs_p[4].append(_from_mem_layout(mk8, bp, n_mem, heads_x, x_dim))
        outs_p[5].append(_from_mem_layout(mv8, bp, n_mem, heads_x, x_dim))

        n_s = bs * t_new
        qs, ks, vs, yscs, u_s = _in_proj(
            hs, row(g_mix[l]), wi, tabs_s, w_sc[l], dims, batch=bs, seq_len=t_new, tm=n_s,
            state=_expand_state(state_short_conv[l], t_new))
        o_s = _attn_sample(qs.reshape(bs, t_new, qc), ks.reshape(bs, t_new, kc),
                           vs.reshape(bs, t_new, vc),
                           jnp.transpose(cache_attn_k[l].reshape(bs, past, kc), (0, 2, 1)),
                           cache_attn_v[l].reshape(bs, past * heads_a, v_dim), lams, gsub,
                           lam_init=lam_init, nb=SAMPLE_ATTN_SEQS)
        x2s = _post_rows(hs, o_s.reshape(n_s, vc), yscs, wo, row(g_x[l]), wq,
                         _to_mem_layout(cache_mem_k[l], x_dim // LANES),
                         _to_mem_layout(cache_mem_v[l], x_dim // LANES), wxo_b,
                         batch=bs, seq_len=t_new, heads=heads_x, nb=SAMPLE_POST_SEQS)
        hs, up_s = _ffn(x2s, row(g_ffn[l]), wu, wg, w_ffconv[l], wd, row(g_final),
                        batch=bs, seq_len=t_new, tm=n_s, final_norm=l == depth - 1,
                        state=_expand_state(state_ffn_conv[l], t_new))
        outs_s[0].append(ks.reshape(bs, t_new, heads_a, 2, qk_dim))
        outs_s[1].append(vs.reshape(bs, t_new, heads_a, v_dim))
        outs_s[2].append(u_s.reshape(bs, t_new, cw)[:, t_new - 2:])
        outs_s[3].append(up_s.reshape(bs, t_new, dff)[:, t_new - 2:])

    return (hp.reshape(bp, seq, d), hs.reshape(bs, t_new, d),
            *(jnp.stack(o) for o in outs_p), *(jnp.stack(o) for o in outs_s))
```
